```python
import math
import jax, jax.numpy as jnp
from jax import lax
import numpy as np

D_MODEL = 1024
BATCH = 2
SEQ = 8192
DEPTH = 1
DEC_BATCH = 128
DEC_SEQ = 4
PAST_LEN = 8192
PAGE_SIZE = 128

HEAD_DIM = 64
N_HEADS = D_MODEL // HEAD_DIM
A_Q_HEADS = N_HEADS // 2
A_KV_HEADS = A_Q_HEADS // 4
A_GROUP = A_Q_HEADS // A_KV_HEADS
B_HEADS = N_HEADS - A_Q_HEADS
A_WINDOW = 128
BLOCK = 128
B_PATTERNS = ((128, 1), (512, 4), (2048, 16))
B_WINDOW = max(w for w, _ in B_PATTERNS)
D_FF = -(-8 * D_MODEL // (3 * 256)) * 256
ROPE_THETA = 10000.0
EPS = 1e-6
A_QW = A_Q_HEADS * HEAD_DIM
A_KVW = A_KV_HEADS * HEAD_DIM
B_W = B_HEADS * HEAD_DIM
IN_COLS = A_QW + 2 * A_KVW + 3 * B_W
SPLIT_IDX = (A_QW, A_QW + A_KVW, A_QW + 2 * A_KVW, A_QW + 2 * A_KVW + B_W, A_QW + 2 * A_KVW + 2 * B_W)
SCALE = HEAD_DIM ** -0.5

kernel_name = 'hymba_swa_sink_dilated_decode_step'


def rms_norm(x, g):
    xf = x.astype(jnp.float32)
    y = xf * lax.rsqrt(jnp.mean(xf * xf, axis=-1, keepdims=True) + EPS)
    return (y * g.astype(jnp.float32)).astype(x.dtype)


def rotary(x, pos):
    half = HEAD_DIM // 2
    inv = jnp.exp(-math.log(ROPE_THETA) * jnp.arange(half, dtype=jnp.float32) * (2.0 / HEAD_DIM))
    ang = pos.astype(jnp.float32)[:, None] * inv[None, :]
    cos = jnp.cos(ang)[:, None, :]
    sin = jnp.sin(ang)[:, None, :]
    xf = x.astype(jnp.float32)
    x1, x2 = xf[..., :half], xf[..., half:]
    return jnp.concatenate([x1 * cos - x2 * sin, x2 * cos + x1 * sin], axis=-1).astype(x.dtype)


def _adaln(c, w, b):
    mod = jax.nn.silu(c) @ w + b
    return jnp.split(mod[:, None, :], 6, axis=-1)


def _modulate(h, shift, scale):
    return h * (1 + scale) + shift


def _band_blocks(t, n_blocks, extra_end):
    pad = [(0, 0), (BLOCK, extra_end)] + [(0, 0)] * (t.ndim - 2)
    tb = jnp.pad(t, pad).reshape((t.shape[0], n_blocks + 1, BLOCK) + t.shape[2:])
    return jnp.concatenate([tb[:, :-1], tb[:, 1:]], axis=2)


def _band_mask(n_blocks, max_dist):
    qi = jnp.arange(BLOCK)[:, None]
    ki = jnp.arange(2 * BLOCK)[None, :]
    rel = qi + BLOCK - ki
    kpos = jnp.arange(n_blocks)[:, None, None] * BLOCK - BLOCK + ki
    return (rel >= 0) & (rel <= max_dist) & (kpos >= 0)


def _sink_probs(s, mask, sink):
    s = jnp.where(mask, s, -jnp.inf)
    m = jnp.maximum(jnp.max(s, axis=-1, keepdims=True), sink)
    p = jnp.exp(s - m)
    return p / (jnp.sum(p, axis=-1, keepdims=True) + jnp.exp(sink - m))


def _lse_probs(s, mask):
    s = jnp.where(mask, s, -jnp.inf)
    m = jnp.max(s, axis=-1, keepdims=True)
    p = jnp.exp(s - m)
    l = jnp.sum(p, axis=-1, keepdims=True)
    return p / l, (m + jnp.log(l))[..., 0]


def swa_sink_prompt(q, k, v, sinks):
    n, S = q.shape[0], q.shape[1]
    nb = S // BLOCK
    qb = q.astype(jnp.float32).reshape(n, nb, BLOCK, A_KV_HEADS, A_GROUP, HEAD_DIM)
    kb = _band_blocks(k.astype(jnp.float32), nb, 0)
    vb = _band_blocks(v.astype(jnp.float32), nb, 0)
    s = jnp.einsum('bnqhge,bnkhe->bnhgqk', qb, kb) * SCALE
    mask = _band_mask(nb, A_WINDOW - 1)[None, :, None, None]
    sink = sinks.astype(jnp.float32).reshape(A_KV_HEADS, A_GROUP)[None, None, :, :, None, None]
    p = _sink_probs(s, mask, sink)
    o = jnp.einsum('bnhgqk,bnkhe->bnqhge', p, vb)
    return o.reshape(n, S, A_QW).astype(q.dtype)


def swa_sink_sample(q, k_new, v_new, k_cache, v_cache, sinks):
    n, T = q.shape[0], q.shape[1]
    WA = k_cache.shape[1]
    k_all = jnp.concatenate([k_cache.astype(k_new.dtype), k_new], axis=1)
    v_all = jnp.concatenate([v_cache.astype(v_new.dtype), v_new], axis=1)
    qg = q.astype(jnp.float32).reshape(n, T, A_KV_HEADS, A_GROUP, HEAD_DIM)
    s = jnp.einsum('bqhge,bkhe->bhgqk', qg, k_all.astype(jnp.float32)) * SCALE
    rel = (WA + jnp.arange(T))[:, None] - jnp.arange(WA + T)[None, :]
    mask = ((rel >= 0) & (rel < A_WINDOW))[None, None, None]
    sink = sinks.astype(jnp.float32).reshape(A_KV_HEADS, A_GROUP)[None, :, :, None, None]
    p = _sink_probs(s, mask, sink)
    o = jnp.einsum('bhgqk,bkhe->bqhge', p, v_all.astype(jnp.float32))
    return o.reshape(n, T, A_QW).astype(q.dtype), k_all[:, -WA:], v_all[:, -WA:]


def _dilated_branch_prompt(q, k, v, window, dil):
    n, S, H, Dh = q.shape
    J = window // dil
    M = S // dil
    Mp = -(-M // BLOCK) * BLOCK
    nb = Mp // BLOCK
    split = lambda t: t.reshape(n, M, dil, H, Dh)
    qb = jnp.pad(split(q), ((0, 0), (0, Mp - M), (0, 0), (0, 0), (0, 0))).reshape(n, nb, BLOCK, dil, H, Dh)
    kb = _band_blocks(split(k), nb, Mp - M)
    vb = _band_blocks(split(v), nb, Mp - M)
    s = jnp.einsum('bnqrhe,bnkrhe->bnrhqk', qb, kb) * SCALE
    p, lse = _lse_probs(s, _band_mask(nb, J)[None, :, None, None])
    o = jnp.einsum('bnrhqk,bnkrhe->bnqrhe', p, vb).reshape(n, Mp, dil, H, Dh)[:, :M].reshape(n, S, H, Dh)
    lse = jnp.transpose(lse, (0, 1, 4, 2, 3)).reshape(n, Mp, dil, H)[:, :M].reshape(n, S, H)
    return o, lse


def _mix_dilations(outs, lses):
    alpha = jax.nn.softmax(jnp.stack(lses), axis=0)
    o = jnp.sum(alpha[..., None] * jnp.stack(outs), axis=0)
    return o.reshape(o.shape[0], o.shape[1], B_W)


def dilated_prompt(q, k, v):
    qf, kf, vf = q.astype(jnp.float32), k.astype(jnp.float32), v.astype(jnp.float32)
    outs, lses = [], []
    for window, dil in B_PATTERNS:
        o, lse = _dilated_branch_prompt(qf, kf, vf, window, dil)
        outs.append(o)
        lses.append(lse)
    return _mix_dilations(outs, lses).astype(q.dtype)


def dilated_sample(q, k_new, v_new, k_cache, v_cache):
    T = q.shape[1]
    WB = k_cache.shape[1]
    k_all = jnp.concatenate([k_cache.astype(k_new.dtype), k_new], axis=1)
    v_all = jnp.concatenate([v_cache.astype(v_new.dtype), v_new], axis=1)
    qf, kf, vf = q.astype(jnp.float32), k_all.astype(jnp.float32), v_all.astype(jnp.float32)
    outs, lses = [], []
    for window, dil in B_PATTERNS:
        J = window // dil
        idx = (WB + jnp.arange(T))[:, None] - dil * jnp.arange(J + 1)[None, :]
        valid = idx >= 0
        idx = jnp.maximum(idx, 0)
        kg, vg = kf[:, idx], vf[:, idx]
        s = jnp.einsum('bqhe,bqjhe->bhqj', qf, kg) * SCALE
        p, lse = _lse_probs(s, valid[None, None])
        outs.append(jnp.einsum('bhqj,bqjhe->bqhe', p, vg))
        lses.append(jnp.transpose(lse, (0, 2, 1)))
    return _mix_dilations(outs, lses).astype(q.dtype), k_all[:, -WB:], v_all[:, -WB:]


def _attn_inputs(x, shift, scale, g, w_in, pos):
    h = _modulate(rms_norm(x, g), shift, scale)
    proj = h @ w_in
    n, L = x.shape[0], x.shape[1]
    qa, ka, va, qb, kb, vb = jnp.split(proj, SPLIT_IDX, axis=-1)
    heads = lambda t, nh: t.reshape(n, L, nh, HEAD_DIM)
    return (rotary(heads(qa, A_Q_HEADS), pos), rotary(heads(ka, A_KV_HEADS), pos), heads(va, A_KV_HEADS),
            rotary(heads(qb, B_HEADS), pos), rotary(heads(kb, B_HEADS), pos), heads(vb, B_HEADS))


def _merge(oa, ob, g_oa, g_ob, w_o):
    return jnp.concatenate([rms_norm(oa, g_oa), rms_norm(ob, g_ob)], axis=-1) @ w_o


def _ffn_sub(x, shift, scale, gate, g, w_gate, w_up, w_down):
    h = _modulate(rms_norm(x, g), shift, scale)
    return x + gate * ((jax.nn.silu(h @ w_gate) * (h @ w_up)) @ w_down)


def setup_inputs(seed: int = 0) -> dict:
    key = jax.random.key(seed)
    ks = jax.random.split(key, 24)
    f32 = jnp.float32
    nrm = lambda k, shape, s: jax.random.normal(k, shape, f32) * s
    WA = min(A_WINDOW, PAST_LEN)
    WB = min(B_WINDOW, PAST_LEN)
    D = D_MODEL
    return {
        'x_prompt': nrm(ks[0], (BATCH, SEQ, D), 1.0),
        'x_sample': nrm(ks[1], (DEC_BATCH, DEC_SEQ, D), 1.0),
        'c_prompt': nrm(ks[2], (BATCH, D), 1.0),
        'c_sample': nrm(ks[3], (DEC_BATCH, D), 1.0),
        'cache_a_k': nrm(ks[4], (DEPTH, DEC_BATCH, WA, A_KV_HEADS, HEAD_DIM), 1.0),
        'cache_a_v': nrm(ks[5], (DEPTH, DEC_BATCH, WA, A_KV_HEADS, HEAD_DIM), 1.0),
        'cache_b_k': nrm(ks[6], (DEPTH, DEC_BATCH, WB, B_HEADS, HEAD_DIM), 1.0),
        'cache_b_v': nrm(ks[7], (DEPTH, DEC_BATCH, WB, B_HEADS, HEAD_DIM), 1.0),
        'w_ada': nrm(ks[8], (DEPTH, D, 6 * D), 0.5 * D ** -0.5),
        'b_ada': nrm(ks[9], (DEPTH, 6 * D), 0.01),
        'g_attn': 1.0 + nrm(ks[10], (DEPTH, D), 0.01),
        'w_in': nrm(ks[11], (DEPTH, D, IN_COLS), D ** -0.5),
        'sinks': nrm(ks[12], (DEPTH, A_Q_HEADS), 1.0),
        'g_out_a': 1.0 + nrm(ks[13], (DEPTH, A_QW), 0.01),
        'g_out_b': 1.0 + nrm(ks[14], (DEPTH, B_W), 0.01),
        'w_o': nrm(ks[15], (DEPTH, A_QW + B_W, D), (A_QW + B_W) ** -0.5),
        'g_ffn': 1.0 + nrm(ks[16], (DEPTH, D), 0.01),
        'w_gate': nrm(ks[17], (DEPTH, D, D_FF), D ** -0.5),
        'w_up': nrm(ks[18], (DEPTH, D, D_FF), D ** -0.5),
        'w_down': nrm(ks[19], (DEPTH, D_FF, D), D_FF ** -0.5),
        'g_final': 1.0 + nrm(ks[20], (D,), 0.01),
    }


def reference(x_prompt, x_sample, c_prompt, c_sample, cache_a_k, cache_a_v, cache_b_k, cache_b_v,
              w_ada, b_ada, g_attn, w_in, sinks, g_out_a, g_out_b, w_o, g_ffn, w_gate, w_up, w_down, g_final):
    S = x_prompt.shape[1]
    pos_p = jnp.arange(S, dtype=jnp.int32)
    pos_s = PAST_LEN + jnp.arange(x_sample.shape[1], dtype=jnp.int32)
    xp, xs = x_prompt, x_sample
    pa_k, pa_v, pb_k, pb_v, sa_k, sa_v, sb_k, sb_v = ([] for _ in range(8))
    for l in range(DEPTH):
        sh_a_p, sc_a_p, gt_a_p, sh_f_p, sc_f_p, gt_f_p = _adaln(c_prompt, w_ada[l], b_ada[l])
        sh_a_s, sc_a_s, gt_a_s, sh_f_s, sc_f_s, gt_f_s = _adaln(c_sample, w_ada[l], b_ada[l])
        qa, ka, va, qb, kb, vb = _attn_inputs(xp, sh_a_p, sc_a_p, g_attn[l], w_in[l], pos_p)
        oa = swa_sink_prompt(qa, ka, va, sinks[l])
        ob = dilated_prompt(qb, kb, vb)
        xp = xp + gt_a_p * _merge(oa, ob, g_out_a[l], g_out_b[l], w_o[l])
        xp = _ffn_sub(xp, sh_f_p, sc_f_p, gt_f_p, g_ffn[l], w_gate[l], w_up[l], w_down[l])
        pa_k.append(ka[:, -min(A_WINDOW, S):])
        pa_v.append(va[:, -min(A_WINDOW, S):])
        pb_k.append(kb[:, -min(B_WINDOW, S):])
        pb_v.append(vb[:, -min(B_WINDOW, S):])
        qa, ka, va, qb, kb, vb = _attn_inputs(xs, sh_a_s, sc_a_s, g_attn[l], w_in[l], pos_s)
        oa, na_k, na_v = swa_sink_sample(qa, ka, va, cache_a_k[l], cache_a_v[l], sinks[l])
        ob, nb_k, nb_v = dilated_sample(qb, kb, vb, cache_b_k[l], cache_b_v[l])
        xs = xs + gt_a_s * _merge(oa, ob, g_out_a[l], g_out_b[l], w_o[l])
        xs = _ffn_sub(xs, sh_f_s, sc_f_s, gt_f_s, g_ffn[l], w_gate[l], w_up[l], w_down[l])
        sa_k.append(na_k)
        sa_v.append(na_v)
        sb_k.append(nb_k)
        sb_v.append(nb_v)
    y_prompt = rms_norm(xp, g_final)
    y_sample = rms_norm(xs, g_final)
    return (y_prompt, y_sample, jnp.stack(pa_k), jnp.stack(pa_v), jnp.stack(pb_k), jnp.stack(pb_v),
            jnp.stack(sa_k), jnp.stack(sa_v), jnp.stack(sb_k), jnp.stack(sb_v))
```

```python
import functools
import math

import jax
import jax.numpy as jnp
from jax import lax
from jax.experimental import pallas as pl
from jax.experimental.pallas import tpu as pltpu

HEAD_DIM = 64
A_Q_HEADS = 8
A_KV_HEADS = 2
A_GROUP = A_Q_HEADS // A_KV_HEADS
B_HEADS = 8
A_WINDOW = 128
BLOCK = 128
B_PATTERNS = ((128, 1), (512, 4), (2048, 16))
B_WINDOW = 2048
ROPE_THETA = 10000.0
EPS = 1e-6
A_QW = A_Q_HEADS * HEAD_DIM
A_KVW = A_KV_HEADS * HEAD_DIM
B_W = B_HEADS * HEAD_DIM
SCALE = HEAD_DIM ** -0.5
LANES = 128
NEG = -1e30
VMEM_LIMIT = 56 * 1024 * 1024

_BF = jnp.bfloat16
_F32 = jnp.float32


def _params(sem, vmem=VMEM_LIMIT):
    return pltpu.CompilerParams(dimension_semantics=sem, vmem_limit_bytes=vmem)


def _resident(shape):
    nd = len(shape)
    return pl.BlockSpec(shape, lambda *_: (0,) * nd, pipeline_mode=pl.Buffered(1))


def _rms(x, g):
    return x * lax.rsqrt(jnp.mean(x * x, axis=-1, keepdims=True) + EPS) * g


def _adaln_kernel(c_ref, w_ref, b_ref, o_ref):
    c = c_ref[...]
    a = (c * jax.nn.sigmoid(c)).astype(_BF)
    o_ref[...] = jnp.dot(a, w_ref[...], preferred_element_type=_F32) + b_ref[...]


def _adaln(c_all, w_bf, b):
    m, d = c_all.shape
    n = w_bf.shape[1]
    tn = 1536
    return pl.pallas_call(
        _adaln_kernel,
        grid=(n // tn,),
        in_specs=[pl.BlockSpec((m, d), lambda j: (0, 0)),
                  pl.BlockSpec((d, tn), lambda j: (0, j)),
                  pl.BlockSpec((1, tn), lambda j: (0, j))],
        out_specs=pl.BlockSpec((m, tn), lambda j: (0, j)),
        out_shape=jax.ShapeDtypeStruct((m, n), _F32),
        compiler_params=_params(("arbitrary",)),
        name="adaln",
    )(c_all, w_bf, b)


def _rope(x, cos, sin_signed, first_half):
    sw = jnp.where(first_half, pltpu.roll(x, 96, 1), pltpu.roll(x, 32, 1))
    return x * cos + sw * sin_signed


def _inproj_kernel(x_ref, sh_ref, sc_ref, g_ref, w_ref, cos_ref, sin_ref,
                   qa_ref, ka_ref, va_ref, qb_ref, kb_ref, vb_ref,
                   kab_ref, vab_ref, kbb_ref, vbb_ref):
    d = x_ref.shape[-1]
    x = x_ref[...].reshape(-1, d)
    tm = x.shape[0]
    sh = sh_ref[...].reshape(-1, d)
    sc = sc_ref[...].reshape(-1, d)
    h = _rms(x, g_ref[...]) * (1.0 + sc) + sh
    proj = jnp.dot(h.astype(_BF), w_ref[...], preferred_element_type=_F32)
    cos = cos_ref[...].reshape(-1, LANES)
    sin = sin_ref[...].reshape(-1, LANES)
    lane = lax.broadcasted_iota(jnp.int32, (1, LANES), 1)
    first_half = (lane % HEAD_DIM) < (HEAD_DIM // 2)

    def rot(col0, ncols):
        parts = [_rope(proj[:, col0 + j * LANES: col0 + (j + 1) * LANES], cos, sin, first_half)
                 for j in range(ncols // LANES)]
        return parts[0] if len(parts) == 1 else jnp.concatenate(parts, axis=1)

    o = 0
    qa = rot(o, A_QW); o += A_QW
    ka = rot(o, A_KVW); o += A_KVW
    va = proj[:, o:o + A_KVW]; o += A_KVW
    qb = rot(o, B_W); o += B_W
    kb = rot(o, B_W); o += B_W
    vb = proj[:, o:o + B_W]

    def put(ref, val):
        ref[...] = val.astype(ref.dtype).reshape(ref.shape)

    put(qa_ref, qa * SCALE)
    put(qb_ref, qb * SCALE)
    put(ka_ref, ka); put(va_ref, va); put(kb_ref, kb); put(vb_ref, vb)
    put(kab_ref, ka); put(vab_ref, va); put(kbb_ref, kb); put(vbb_ref, vb)


def _inproj(x, sh, sc, g, w_bf, cos, sin, *, tm, prompt):
    n, l, d = x.shape
    grid = (n, l // tm)
    row = lambda c: pl.BlockSpec((1, tm, c), lambda b, i: (b, i, 0))
    if prompt:
        mod_spec = pl.BlockSpec((1, 1, d), lambda b, i: (b, 0, 0))
        tab_spec = pl.BlockSpec((tm, LANES), lambda b, i: (i, 0))
    else:
        mod_spec = pl.BlockSpec((tm, d), lambda b, i: (i, 0))
        tab_spec = pl.BlockSpec((1, 1, LANES), lambda b, i: (b, 0, 0))
    shp = lambda c, dt: jax.ShapeDtypeStruct((n, l, c), dt)
    return pl.pallas_call(
        _inproj_kernel,
        grid=grid,
        in_specs=[row(d), mod_spec, mod_spec, _resident((1, d)), _resident(w_bf.shape), tab_spec, tab_spec],
        out_specs=[row(A_QW), row(A_KVW), row(A_KVW), row(B_W), row(B_W), row(B_W),
                   row(A_KVW), row(A_KVW), row(B_W), row(B_W)],
        out_shape=[shp(A_QW, _BF), shp(A_KVW, _F32), shp(A_KVW, _F32), shp(B_W, _BF), shp(B_W, _F32),
                   shp(B_W, _F32), shp(A_KVW, _BF), shp(A_KVW, _BF), shp(B_W, _BF), shp(B_W, _BF)],
        compiler_params=_params(("parallel", "parallel")),
        name="in_proj_prompt" if prompt else "in_proj_sample",
    )(x, sh, sc, g, w_bf, cos, sin)


def _band_valid(rows, max_dist, first_block):
    qi = lax.broadcasted_iota(jnp.int32, (rows, 2 * BLOCK), 0) & (BLOCK - 1)
    ki = lax.broadcasted_iota(jnp.int32, (rows, 2 * BLOCK), 1)
    rel = qi + BLOCK - ki
    return (rel >= 0) & (rel <= max_dist) & ((ki >= BLOCK) | jnp.logical_not(first_block))


def _band_attn_kernel(q_ref, kp_ref, kc_ref, vp_ref, vc_ref, o_ref, lse_ref, *, max_dist):
    first_block = pl.program_id(2) == 0
    valid = _band_valid(2 * BLOCK, max_dist, first_block)
    lane = lax.broadcasted_iota(jnp.int32, (1, LANES), 1)
    lo = lane < HEAD_DIM
    hcol = lax.broadcasted_iota(jnp.int32, (1, B_HEADS), 1)
    lse_all = jnp.zeros((BLOCK, B_HEADS), _F32)
    zero = jnp.zeros((), _BF)
    for c in range(B_W // LANES):
        cs = slice(c * LANES, (c + 1) * LANES)
        q2 = q_ref[0, :, cs]
        k2 = jnp.concatenate([kp_ref[0, :, cs], kc_ref[0, :, cs]], axis=0)
        v2 = jnp.concatenate([vp_ref[0, :, cs], vc_ref[0, :, cs]], axis=0)
        qs = jnp.concatenate([jnp.where(lo, q2, zero), jnp.where(lo, zero, q2)], axis=0)
        s = lax.dot_general(qs, k2, (((1,), (1,)), ((), ())), preferred_element_type=_F32)
        s = jnp.where(valid, s, NEG)
        m = jnp.max(s, axis=-1, keepdims=True)
        p = jnp.exp(s - m)
        l = jnp.sum(p, axis=-1, keepdims=True)
        pv = jnp.dot(p.astype(_BF), v2, preferred_element_type=_F32)
        on = pv / l
        o_ref[0, :, cs] = jnp.where(lo, on[:BLOCK], on[BLOCK:])
        lse = m + jnp.log(l)
        lse_all = jnp.where(hcol == 2 * c, lse[:BLOCK], lse_all)
        lse_all = jnp.where(hcol == 2 * c + 1, lse[BLOCK:], lse_all)
    lse_ref[0, 0] = lse_all


def _band_attn(q, k, v, dil):
    b, s, w = q.shape
    m = s // dil
    nb = m // BLOCK
    view = lambda t: t.reshape(b, m, dil * w)
    cur = pl.BlockSpec((1, BLOCK, w), lambda bb, r, i: (bb, i, r))
    prev = pl.BlockSpec((1, BLOCK, w), lambda bb, r, i: (bb, jnp.maximum(i - 1, 0), r))
    window, _ = [p for p in B_PATTERNS if p[1] == dil][0]
    o, lse = pl.pallas_call(
        functools.partial(_band_attn_kernel, max_dist=window // dil),
        grid=(b, dil, nb),
        in_specs=[cur, prev, cur, prev, cur],
        out_specs=[cur, pl.BlockSpec((1, 1, BLOCK, B_HEADS), lambda bb, r, i: (bb, r, i, 0))],
        out_shape=[jax.ShapeDtypeStruct((b, m, dil * w), _F32),
                   jax.ShapeDtypeStruct((b, dil, m, B_HEADS), _F32)],
        compiler_params=_params(("parallel", "parallel", "arbitrary")),
        name=f"band_attn_d{dil}",
    )(view(q), view(k), view(k), view(v), view(v))
    return o.reshape(b, s, w), lse.transpose(0, 2, 1, 3).reshape(b, s, B_HEADS)


def _swa_kernel(sink_ref, q_ref, kp_ref, kc_ref, vp_ref, vc_ref, o_ref):
    first_block = pl.program_id(1) == 0
    rows = A_Q_HEADS * BLOCK
    valid = _band_valid(rows, A_WINDOW - 1, first_block)
    lane = lax.broadcasted_iota(jnp.int32, (1, LANES), 1)
    lo = lane < HEAD_DIM
    k2 = jnp.concatenate([kp_ref[0], kc_ref[0]], axis=0)
    v2 = jnp.concatenate([vp_ref[0], vc_ref[0]], axis=0)
    tiles, sinks = [], []
    for h in range(A_Q_HEADS):
        g, half = h // A_GROUP, h % 2
        qc = q_ref[0, :, (h // 2) * LANES:(h // 2 + 1) * LANES].astype(_F32)
        if half != g:
            qc = pltpu.roll(qc, HEAD_DIM, 1)
        keep = lo if g == 0 else jnp.logical_not(lo)
        tiles.append(jnp.where(keep, qc, 0.0).astype(_BF))
        sinks.append(jnp.full((BLOCK, 1), sink_ref[h], _F32))
    qs = jnp.concatenate(tiles, axis=0)
    sink = jnp.concatenate(sinks, axis=0)
    s = lax.dot_general(qs, k2, (((1,), (1,)), ((), ())), preferred_element_type=_F32)
    s = jnp.where(valid, s, NEG)
    m = jnp.maximum(jnp.max(s, axis=-1, keepdims=True), sink)
    p = jnp.exp(s - m)
    l = jnp.sum(p, axis=-1, keepdims=True) + jnp.exp(sink - m)
    on = jnp.dot(p.astype(_BF), v2, preferred_element_type=_F32) / l
    for c in range(A_QW // LANES):
        halves = []
        for half in range(2):
            h = 2 * c + half
            t = on[h * BLOCK:(h + 1) * BLOCK]
            if half != h // A_GROUP:
                t = pltpu.roll(t, HEAD_DIM, 1)
            halves.append(t)
        o_ref[0, :, c * LANES:(c + 1) * LANES] = jnp.where(lo, halves[0], halves[1])


def _swa_prompt(q, k, v, sinks):
    b, s, _ = q.shape
    nb = s // BLOCK
    qspec = pl.BlockSpec((1, BLOCK, A_QW), lambda bb, i: (bb, i, 0))
    cur = pl.BlockSpec((1, BLOCK, A_KVW), lambda bb, i: (bb, i, 0))
    prev = pl.BlockSpec((1, BLOCK, A_KVW), lambda bb, i: (bb, jnp.maximum(i - 1, 0), 0))
    return pl.pallas_call(
        _swa_kernel,
        grid=(b, nb),
        in_specs=[pl.BlockSpec(memory_space=pltpu.SMEM), qspec, prev, cur, prev, cur],
        out_specs=qspec,
        out_shape=jax.ShapeDtypeStruct((b, s, A_QW), _F32),
        compiler_params=_params(("parallel", "arbitrary")),
        name="swa_prompt",
    )(sinks, q, k, k, v, v)


def _attend(q, parts, sink=None):
    ss = [jnp.sum(kk * q[None], axis=-1, keepdims=True) for kk, _ in parts]
    m = functools.reduce(jnp.maximum, [jnp.max(s, axis=0) for s in ss])
    if sink is not None:
        m = jnp.maximum(m, sink)
    ps = [jnp.exp(s - m[None]) for s in ss]
    l = functools.reduce(jnp.add, [jnp.sum(p, axis=0) for p in ps])
    if sink is not None:
        l = l + jnp.exp(sink - m)
    o = functools.reduce(jnp.add, [jnp.sum(p * vv, axis=0) for p, (_, vv) in zip(ps, parts)])
    return o / l, m + jnp.log(l)


def _sample_attn_kernel(sink_ref, qa_ref, kan_ref, van_ref, cak_ref, cav_ref,
                        qb_ref, kbn_ref, vbn_ref, kt_ref, vt_ref, k16_ref, v16_ref,
                        oa_ref, ob_ref, nak_ref, nav_ref, *, t_new, wa, wb):
    for i in range(t_new):
        lo_row = max(wa + i - (A_WINDOW - 1), 0)
        for g in range(A_KV_HEADS):
            hs = slice(g * A_GROUP, (g + 1) * A_GROUP)
            q = qa_ref[0, i, hs, :]
            parts = []
            if lo_row < wa:
                parts.append((cak_ref[0, 0, lo_row:wa, g:g + 1, :], cav_ref[0, 0, lo_row:wa, g:g + 1, :]))
            n0 = max(lo_row - wa, 0)
            parts.append((kan_ref[0, n0:i + 1, g:g + 1, :], van_ref[0, n0:i + 1, g:g + 1, :]))
            sink = jnp.concatenate([jnp.full((1, 1), sink_ref[g * A_GROUP + u], _F32) for u in range(A_GROUP)], axis=0)
            o, _ = _attend(q, parts, sink)
            oa_ref[0, i, hs, :] = o
    if wa > t_new:
        nak_ref[0, 0, 0:wa - t_new] = cak_ref[0, 0, t_new:wa]
        nav_ref[0, 0, 0:wa - t_new] = cav_ref[0, 0, t_new:wa]
    nak_ref[0, 0, wa - t_new:wa] = kan_ref[0]
    nav_ref[0, 0, wa - t_new:wa] = van_ref[0]

    tail = kt_ref.shape[2]
    for i in range(t_new):
        q = qb_ref[0, i]
        outs, lses = [], []
        for window, dil in B_PATTERNS:
            new_rows = [r for r in range(i, -1, -dil)]
            parts = [(kbn_ref[0, r:r + 1], vbn_ref[0, r:r + 1]) for r in new_rows]
            js = [j for j in range(1, window // dil + 1) if 0 <= wb + i - dil * j < wb]
            if js:
                first, last = wb + i - dil * js[-1], wb + i - dil * js[0]
                cnt = len(js)
                if dil == 16:
                    assert first == i and cnt == k16_ref.shape[2]
                    parts.append((k16_ref[0, 0, :, i], v16_ref[0, 0, :, i]))
                elif dil == 1:
                    a = first - (wb - tail)
                    parts.append((kt_ref[0, 0, a:a + cnt], vt_ref[0, 0, a:a + cnt]))
                else:
                    a = first - (wb - tail)
                    assert a == i % dil and cnt * dil == tail
                    kk = kt_ref[0, 0].reshape(cnt, dil, B_HEADS, HEAD_DIM)[:, a]
                    vv = vt_ref[0, 0].reshape(cnt, dil, B_HEADS, HEAD_DIM)[:, a]
                    parts.append((kk, vv))
            o, lse = _attend(q, parts)
            outs.append(o)
            lses.append(lse)
        mx = functools.reduce(jnp.maximum, lses)
        es = [jnp.exp(z - mx) for z in lses]
        den = functools.reduce(jnp.add, es)
        ob_ref[0, i] = functools.reduce(jnp.add, [e * o for e, o in zip(es, outs)]) / den


def _sample_attn(sinks, qa, ka_new, va_new, cache_a_k, cache_a_v, qb, kb_new, vb_new, cache_b_k, cache_b_v):
    n, t_new = qa.shape[0], qa.shape[1]
    wa, wb = cache_a_k.shape[2], cache_b_k.shape[2]
    tail = 512
    assert wb == B_WINDOW and wb % 16 == 0 and t_new <= 16
    new = lambda h: pl.BlockSpec((1, t_new, h, HEAD_DIM), lambda i: (i, 0, 0, 0))
    ca = pl.BlockSpec((1, 1, wa, A_KV_HEADS, HEAD_DIM), lambda i: (0, i, 0, 0, 0))
    ct = pl.BlockSpec((1, 1, tail, B_HEADS, HEAD_DIM), lambda i: (0, i, wb // tail - 1, 0, 0))
    c16 = pl.BlockSpec((1, 1, wb // 16, t_new, B_HEADS, HEAD_DIM), lambda i: (0, i, 0, 0, 0, 0))
    k16 = cache_b_k.reshape(1, n, wb // 16, 16, B_HEADS, HEAD_DIM)
    v16 = cache_b_v.reshape(1, n, wb // 16, 16, B_HEADS, HEAD_DIM)
    return pl.pallas_call(
        functools.partial(_sample_attn_kernel, t_new=t_new, wa=wa, wb=wb),
        grid=(n,),
        in_specs=[pl.BlockSpec(memory_space=pltpu.SMEM), new(A_Q_HEADS), new(A_KV_HEADS), new(A_KV_HEADS), ca, ca,
                  new(B_HEADS), new(B_HEADS), new(B_HEADS), ct, ct, c16, c16],
        out_specs=[new(A_Q_HEADS), new(B_HEADS), ca, ca],
        out_shape=[jax.ShapeDtypeStruct((n, t_new, A_Q_HEADS, HEAD_DIM), _F32),
                   jax.ShapeDtypeStruct((n, t_new, B_HEADS, HEAD_DIM), _F32),
                   jax.ShapeDtypeStruct(cache_a_k.shape, _F32),
                   jax.ShapeDtypeStruct(cache_a_v.shape, _F32)],
        compiler_params=_params(("parallel",)),
        name="sample_attn",
    )(sinks, qa, ka_new, va_new, cache_a_k, cache_a_v, qb, kb_new, vb_new, cache_b_k, cache_b_v, k16, v16)


_SHIFT_CHUNKS = 8


def _cache_shift_kernel(ck_ref, cv_ref, nk_ref, nv_ref, ok_ref, ov_ref, sem):
    n, w, t_new = ck_ref.shape[1], ck_ref.shape[2], nk_ref.shape[1]
    step = n // _SHIFT_CHUNKS
    copies = []
    for c in range(_SHIFT_CHUNKS):
        ns = slice(c * step, (c + 1) * step)
        for j, (cache, new, out) in enumerate(((ck_ref, nk_ref, ok_ref), (cv_ref, nv_ref, ov_ref))):
            copies.append(pltpu.make_async_copy(cache.at[0, ns, t_new:w], out.at[0, ns, 0:w - t_new],
                                                sem.at[c, 2 * j]))
            copies.append(pltpu.make_async_copy(new.at[ns], out.at[0, ns, w - t_new:w], sem.at[c, 2 * j + 1]))
    for cp in copies:
        cp.start()
    for cp in copies:
        cp.wait()


def _cache_shift(cache_k, cache_v, new_k, new_v):
    any_spec = pl.BlockSpec(memory_space=pl.ANY)
    return pl.pallas_call(
        _cache_shift_kernel,
        in_specs=[any_spec] * 4,
        out_specs=[any_spec] * 2,
        out_shape=[jax.ShapeDtypeStruct(cache_k.shape, cache_k.dtype),
                   jax.ShapeDtypeStruct(cache_v.shape, cache_v.dtype)],
        scratch_shapes=[pltpu.SemaphoreType.DMA((_SHIFT_CHUNKS, 4))],
        name="cache_shift",
    )(cache_k, cache_v, new_k, new_v)


_FF_CHUNKS = 2


def _merge_ffn_kernel(*refs, mix):
    if mix:
        (x_ref, oa_ref, o1_ref, o2_ref, o3_ref, l1_ref, l2_ref, l3_ref, e_ref,
         gta_ref, shf_ref, scf_ref, gtf_ref, goa_ref, gob_ref, wo_ref, gf_ref, wg_ref, wu_ref, wd_ref, gfin_ref,
         y_ref) = refs
    else:
        (x_ref, oa_ref, ob_ref,
         gta_ref, shf_ref, scf_ref, gtf_ref, goa_ref, gob_ref, wo_ref, gf_ref, wg_ref, wu_ref, wd_ref, gfin_ref,
         y_ref) = refs
    d = x_ref.shape[-1]
    two = lambda r: r[...].reshape(-1, r.shape[-1])
    x = two(x_ref)
    if mix:
        lses = [two(l1_ref), two(l2_ref), two(l3_ref)]
        mx = jnp.maximum(jnp.maximum(lses[0], lses[1]), lses[2])
        es = [jnp.exp(z - mx) for z in lses]
        den = es[0] + es[1] + es[2]
        ob = jnp.zeros((x.shape[0], B_W), _F32)
        for e, o_ref in zip(es, (o1_ref, o2_ref, o3_ref)):
            alpha = e / den
            hi = alpha.astype(_BF)
            lo = (alpha - hi.astype(_F32)).astype(_BF)
            wide = (jnp.dot(hi, e_ref[...], preferred_element_type=_F32)
                    + jnp.dot(lo, e_ref[...], preferred_element_type=_F32))
            ob = ob + wide * two(o_ref)
    else:
        ob = two(ob_ref)
    merged = jnp.concatenate([_rms(two(oa_ref), goa_ref[...]), _rms(ob, gob_ref[...])], axis=1)
    attn = jnp.dot(merged.astype(_BF), wo_ref[...], preferred_element_type=_F32)
    x1 = x + two(gta_ref) * attn
    h = (_rms(x1, gf_ref[...]) * (1.0 + two(scf_ref)) + two(shf_ref)).astype(_BF)
    dff = wg_ref.shape[1]
    fc = dff // _FF_CHUNKS
    acc = jnp.zeros_like(x1)
    for c in range(_FF_CHUNKS):
        cs = slice(c * fc, (c + 1) * fc)
        gt = jnp.dot(h, wg_ref[:, cs], preferred_element_type=_F32)
        up = jnp.dot(h, wu_ref[:, cs], preferred_element_type=_F32)
        act = (gt * jax.nn.sigmoid(gt) * up).astype(_BF)
        acc = acc + jnp.dot(act, wd_ref[cs, :], preferred_element_type=_F32)
    x2 = x1 + two(gtf_ref) * acc
    y_ref[...] = _rms(x2, gfin_ref[...]).reshape(y_ref.shape)


def _merge_ffn(x, oa, obs, lses, mods, goa, gob, wo, gf, wg, wu, wd, gfin, *, tm, prompt):
    n, l, d = x.shape
    row = lambda c: pl.BlockSpec((1, tm, c), lambda b, i: (b, i, 0))
    if prompt:
        mod_spec = pl.BlockSpec((1, 1, d), lambda b, i: (b, 0, 0))
    else:
        mod_spec = pl.BlockSpec((tm, d), lambda b, i: (i, 0))
    ins = [x, oa] + list(obs)
    specs = [row(d), row(A_QW)] + [row(B_W)] * len(obs)
    if prompt:
        expand = (jnp.arange(B_W)[None, :] // HEAD_DIM == jnp.arange(B_HEADS)[:, None]).astype(_BF)
        ins += list(lses) + [expand]
        specs += [row(B_HEADS)] * len(lses) + [_resident(expand.shape)]
    ins += list(mods) + [goa, gob, wo, gf, wg, wu, wd, gfin]
    specs += [mod_spec] * 4 + [_resident(a.shape) for a in (goa, gob, wo, gf, wg, wu, wd, gfin)]
    return pl.pallas_call(
        functools.partial(_merge_ffn_kernel, mix=prompt),
        grid=(n, l // tm),
        in_specs=specs,
        out_specs=row(d),
        out_shape=jax.ShapeDtypeStruct((n, l, d), _F32),
        compiler_params=_params(("parallel", "parallel")),
        name="merge_ffn_prompt" if prompt else "merge_ffn_sample",
    )(*ins)


def _rope_tables(pos):
    half = HEAD_DIM // 2
    inv = jnp.exp(-math.log(ROPE_THETA) * jnp.arange(half, dtype=_F32) * (2.0 / HEAD_DIM))
    ang = pos.astype(_F32)[:, None] * inv[None, :]
    cos, sin = jnp.cos(ang), jnp.sin(ang)
    return jnp.tile(cos, (1, LANES // half)), jnp.tile(jnp.concatenate([-sin, sin], axis=1), (1, LANES // HEAD_DIM))


def kernel(x_prompt, x_sample, c_prompt, c_sample, cache_a_k, cache_a_v, cache_b_k, cache_b_v, w_ada, b_ada, g_attn, w_in, sinks, g_out_a, g_out_b, w_o, g_ffn, w_gate, w_up, w_down, g_final):
    nb, s, d = x_prompt.shape
    ns, t_new, _ = x_sample.shape
    depth = w_ada.shape[0]
    assert depth == 1, "single trunk layer"
    past_len = 8192
    l = 0
    bf = lambda w: w.astype(_BF)
    row = lambda g: g.reshape(1, -1)

    c_all = jnp.concatenate([c_prompt, c_sample], axis=0)
    pad = (-c_all.shape[0]) % 8
    c_all = jnp.pad(c_all, ((0, pad), (0, 0)))
    mod = _adaln(c_all, bf(w_ada[l]), b_ada[l].reshape(1, -1))
    mod_p = [m.reshape(nb, 1, d) for m in jnp.split(mod[:nb], 6, axis=-1)]
    mod_s = jnp.split(mod[nb:nb + ns], 6, axis=-1)

    cos_p, sin_p = _rope_tables(jnp.arange(s, dtype=jnp.int32))
    cos_s, sin_s = _rope_tables(past_len + jnp.arange(t_new, dtype=jnp.int32))
    w_in_bf = bf(w_in[l])

    (qa, ka, va, qb, kb, vb, ka_bf, va_bf, kb_bf, vb_bf) = _inproj(
        x_prompt, mod_p[0], mod_p[1], row(g_attn[l]), w_in_bf, cos_p, sin_p, tm=512, prompt=True)
    oa = _swa_prompt(qa, ka_bf, va_bf, sinks[l])
    obs, lses = zip(*[_band_attn(qb, kb_bf, vb_bf, dil) for _, dil in B_PATTERNS])
    ffn_w = (row(g_out_a[l]), row(g_out_b[l]), bf(w_o[l]), row(g_ffn[l]), bf(w_gate[l]), bf(w_up[l]), bf(w_down[l]),
             row(g_final))
    y_prompt = _merge_ffn(x_prompt, oa, obs, lses, mod_p[2:6], *ffn_w, tm=512, prompt=True)
    wa_p, wb_p = min(A_WINDOW, s), min(B_WINDOW, s)
    pa_k = ka[:, s - wa_p:].reshape(1, nb, wa_p, A_KV_HEADS, HEAD_DIM)
    pa_v = va[:, s - wa_p:].reshape(1, nb, wa_p, A_KV_HEADS, HEAD_DIM)
    pb_k = kb[:, s - wb_p:].reshape(1, nb, wb_p, B_HEADS, HEAD_DIM)
    pb_v = vb[:, s - wb_p:].reshape(1, nb, wb_p, B_HEADS, HEAD_DIM)

    xs_t = x_sample.transpose(1, 0, 2)
    (qa_s, ka_s, va_s, qb_s, kb_s, vb_s, _, _, _, _) = _inproj(
        xs_t, mod_s[0], mod_s[1], row(g_attn[l]), w_in_bf, cos_s.reshape(t_new, 1, LANES),
        sin_s.reshape(t_new, 1, LANES), tm=ns, prompt=False)
    heads = lambda t, h: t.astype(_F32).reshape(t_new, ns, h, HEAD_DIM).transpose(1, 0, 2, 3)
    ka_n, va_n = heads(ka_s, A_KV_HEADS), heads(va_s, A_KV_HEADS)
    kb_n, vb_n = heads(kb_s, B_HEADS), heads(vb_s, B_HEADS)
    oa_s, ob_s, sa_k, sa_v = _sample_attn(sinks[l], heads(qa_s, A_Q_HEADS), ka_n, va_n, cache_a_k, cache_a_v,
                                          heads(qb_s, B_HEADS), kb_n, vb_n, cache_b_k, cache_b_v)
    sb_k, sb_v = _cache_shift(cache_b_k, cache_b_v, kb_n, vb_n)
    tok = lambda t: t.transpose(1, 0, 2, 3).reshape(t_new, ns, -1)
    y_s = _merge_ffn(xs_t, tok(oa_s), [tok(ob_s)], [], mod_s[2:6], *ffn_w, tm=ns, prompt=False)
    y_sample = y_s.transpose(1, 0, 2)

    return (y_prompt, y_sample, pa_k, pa_v, pb_k, pb_v, sa_k, sa_v, sb_k, sb_v)
```

```python
import functools
import math

import jax
import jax.numpy as jnp
from jax import lax
from jax.experimental import pallas as pl
from jax.experimental.pallas import tpu as pltpu

HEAD_DIM = 64
A_Q_HEADS = 8
A_KV_HEADS = 2
A_GROUP = A_Q_HEADS // A_KV_HEADS
B_HEADS = 8
A_WINDOW = 128
BLOCK = 128
B_PATTERNS = ((128, 1), (512, 4), (2048, 16))
B_WINDOW = 2048
PAST_LEN = 8192
ROPE_THETA = 10000.0
EPS = 1e-6
A_QW = A_Q_HEADS * HEAD_DIM
A_KVW = A_KV_HEADS * HEAD_DIM
B_W = B_HEADS * HEAD_DIM
SCALE = HEAD_DIM ** -0.5
LANES = 128
NEG = -1e30
VMEM_LIMIT = 56 * 1024 * 1024
Q_BLOCKS = 4

_BF = jnp.bfloat16
_F32 = jnp.float32


def _params(sem, vmem=VMEM_LIMIT):
    return pltpu.CompilerParams(dimension_semantics=sem, vmem_limit_bytes=vmem)


def _resident(shape):
    nd = len(shape)
    return pl.BlockSpec(shape, lambda *_: (0,) * nd, pipeline_mode=pl.Buffered(1))


def _rms(x, g):
    return x * lax.rsqrt(jnp.mean(x * x, axis=-1, keepdims=True) + EPS) * g


def _log2(n):
    assert n > 0 and n & (n - 1) == 0, "power of two expected"
    return n.bit_length() - 1


def _div(x, n):
    return x >> _log2(n)


def _multiple(x, n):
    _log2(n)
    return (x & (n - 1)) == 0


def _adaln_kernel(c_ref, w_ref, b_ref, o_ref):
    c = c_ref[...]
    a = (c * jax.nn.sigmoid(c)).astype(_BF)
    o_ref[...] = jnp.dot(a, w_ref[...], preferred_element_type=_F32) + b_ref[...]


def _adaln(c_all, w_bf, b):
    m, d = c_all.shape
    n = w_bf.shape[1]
    tn = 1536
    return pl.pallas_call(
        _adaln_kernel,
        grid=(n // tn,),
        in_specs=[pl.BlockSpec((m, d), lambda j: (0, 0)),
                  pl.BlockSpec((d, tn), lambda j: (0, j)),
                  pl.BlockSpec((1, tn), lambda j: (0, j))],
        out_specs=pl.BlockSpec((m, tn), lambda j: (0, j)),
        out_shape=jax.ShapeDtypeStruct((m, n), _F32),
        compiler_params=_params(("arbitrary",)),
        name="adaln",
    )(c_all, w_bf, b)


def _rope(x, cos, sin_signed, first_half):
    sw = jnp.where(first_half, pltpu.roll(x, 96, 1), pltpu.roll(x, 32, 1))
    return x * cos + sw * sin_signed


def _project(x_ref, sh_ref, sc_ref, g_ref, w_ref, cos_ref, sin_ref):
    d = x_ref.shape[-1]
    x = x_ref[...].reshape(-1, d)
    sh = sh_ref[...].reshape(-1, d)
    sc = sc_ref[...].reshape(-1, d)
    h = _rms(x, g_ref[...]) * (1.0 + sc) + sh
    proj = jnp.dot(h.astype(_BF), w_ref[...], preferred_element_type=_F32)
    cos = cos_ref[...].reshape(-1, LANES)
    sin = sin_ref[...].reshape(-1, LANES)
    lane = lax.broadcasted_iota(jnp.int32, (1, LANES), 1)
    first_half = (lane & (HEAD_DIM - 1)) < (HEAD_DIM // 2)

    def rot(col0, ncols):
        parts = [_rope(proj[:, col0 + j * LANES: col0 + (j + 1) * LANES], cos, sin, first_half)
                 for j in range(ncols // LANES)]
        return parts[0] if len(parts) == 1 else jnp.concatenate(parts, axis=1)

    o = 0
    qa = rot(o, A_QW) * SCALE; o += A_QW
    ka = rot(o, A_KVW); o += A_KVW
    va = proj[:, o:o + A_KVW]; o += A_KVW
    qb = rot(o, B_W) * SCALE; o += B_W
    kb = rot(o, B_W); o += B_W
    vb = proj[:, o:o + B_W]
    return qa, ka, va, qb, kb, vb


def _put(ref, val):
    ref[...] = val.astype(ref.dtype).reshape(ref.shape)


def _inproj_prompt_kernel(x_ref, sh_ref, sc_ref, g_ref, w_ref, cos_ref, sin_ref,
                          qa_ref, ka_ref, va_ref, q1_ref, k1_ref, v1_ref, q4_ref, k4_ref, v4_ref,
                          q16_ref, k16_ref, v16_ref, kat_ref, vat_ref, kbt_ref, vbt_ref,
                          sq_ref, sk_ref, sv_ref):
    qa, ka, va, qb, kb, vb = _project(x_ref, sh_ref, sc_ref, g_ref, w_ref, cos_ref, sin_ref)
    tm = qa.shape[0]
    _put(qa_ref, qa); _put(ka_ref, ka); _put(va_ref, va)
    _put(q1_ref, qb); _put(k1_ref, kb); _put(v1_ref, vb)
    ta = kat_ref.shape[1]
    _put(kat_ref, ka[tm - ta:]); _put(vat_ref, va[tm - ta:])
    _put(kbt_ref, kb); _put(vbt_ref, vb)
    for val, stage, r4, r16 in ((qb, sq_ref, q4_ref, q16_ref), (kb, sk_ref, k4_ref, k16_ref),
                                (vb, sv_ref, v4_ref, v16_ref)):
        for c in range(B_W // LANES):
            stage[c] = val[:, c * LANES:(c + 1) * LANES]
        for dil, out in ((4, r4), (16, r16)):
            n = tm // dil
            for r in range(dil):
                for c in range(B_W // LANES):
                    out[0, r, :, c * LANES:(c + 1) * LANES] = stage[c, pl.ds(r, n, stride=dil), :].astype(out.dtype)


def _inproj_prompt(x, sh, sc, g, w_bf, cos, sin, *, tm, tail_a, tail_b):
    b, s, d = x.shape
    nt = s // tm
    assert tail_b % tm == 0 and tail_a <= tm
    row = lambda c: pl.BlockSpec((1, tm, c), lambda bb, i: (bb, i, 0))
    res = lambda dil: pl.BlockSpec((1, dil, tm // dil, B_W), lambda bb, i: (bb, 0, i, 0))
    mod_spec = pl.BlockSpec((1, 1, d), lambda bb, i: (bb, 0, 0))
    tab_spec = pl.BlockSpec((tm, LANES), lambda bb, i: (i, 0))
    tail_a_spec = pl.BlockSpec((1, tail_a, A_KVW), lambda bb, i: (bb, 0, 0))
    tail_b_spec = pl.BlockSpec((1, tm, B_W), lambda bb, i: (bb, jnp.maximum(i - (nt - tail_b // tm), 0), 0))
    shp = lambda c, dt: jax.ShapeDtypeStruct((b, s, c), dt)
    rshp = lambda dil: jax.ShapeDtypeStruct((b, dil, s // dil, B_W), _BF)
    return pl.pallas_call(
        _inproj_prompt_kernel,
        grid=(b, nt),
        in_specs=[row(d), mod_spec, mod_spec, _resident((1, d)), _resident(w_bf.shape), tab_spec, tab_spec],
        out_specs=[row(A_QW), row(A_KVW), row(A_KVW)] + [res(1)] * 3 + [res(4)] * 3 + [res(16)] * 3
                  + [tail_a_spec] * 2 + [tail_b_spec] * 2,
        out_shape=[shp(A_QW, _BF), shp(A_KVW, _BF), shp(A_KVW, _BF)] + [rshp(1)] * 3 + [rshp(4)] * 3 + [rshp(16)] * 3
                  + [jax.ShapeDtypeStruct((b, tail_a, A_KVW), _F32)] * 2
                  + [jax.ShapeDtypeStruct((b, tail_b, B_W), _F32)] * 2,
        scratch_shapes=[pltpu.VMEM((B_W // LANES, tm, LANES), _F32)] * 3,
        compiler_params=_params(("parallel", "arbitrary")),
        name="in_proj_prompt",
    )(x, sh, sc, g, w_bf, cos, sin)


def _inproj_sample_kernel(x_ref, sh_ref, sc_ref, g_ref, w_ref, cos_ref, sin_ref,
                          qa_ref, ka_ref, va_ref, qb_ref, kb_ref, vb_ref):
    vals = _project(x_ref, sh_ref, sc_ref, g_ref, w_ref, cos_ref, sin_ref)
    for ref, val in zip((qa_ref, ka_ref, va_ref, qb_ref, kb_ref, vb_ref), vals):
        _put(ref, val)


def _inproj_sample(x, sh, sc, g, w_bf, cos, sin):
    t, n, d = x.shape
    row = lambda c: pl.BlockSpec((1, n, c), lambda j: (j, 0, 0))
    mod_spec = pl.BlockSpec((n, d), lambda j: (0, 0))
    tab_spec = pl.BlockSpec((1, 1, LANES), lambda j: (j, 0, 0))
    widths = (A_QW, A_KVW, A_KVW, B_W, B_W, B_W)
    return pl.pallas_call(
        _inproj_sample_kernel,
        grid=(t,),
        in_specs=[row(d), mod_spec, mod_spec, _resident((1, d)), _resident(w_bf.shape), tab_spec, tab_spec],
        out_specs=[row(c) for c in widths],
        out_shape=[jax.ShapeDtypeStruct((t, n, c), _F32) for c in widths],
        compiler_params=_params(("parallel",)),
        name="in_proj_sample",
    )(x, sh, sc, g, w_bf, cos, sin)


def _band_bias(max_dist, drop_prev):
    qi = lax.broadcasted_iota(jnp.int32, (BLOCK, 2 * BLOCK), 0)
    ki = lax.broadcasted_iota(jnp.int32, (BLOCK, 2 * BLOCK), 1)
    rel = qi + BLOCK - ki
    valid = (rel >= 0) & (rel <= max_dist)
    if drop_prev is not None:
        valid = valid & ((ki >= BLOCK) | jnp.logical_not(drop_prev))
    return jnp.where(valid, 0.0, NEG).astype(_F32)


def _kv_window(prev_ref, cur_ref, j, lead, cols):
    if j == 0:
        return jnp.concatenate([prev_ref[lead + (slice(None), cols)], cur_ref[lead + (slice(0, BLOCK), cols)]], axis=0)
    return cur_ref[lead + (slice((j - 1) * BLOCK, (j + 1) * BLOCK), cols)]


def _band_attn_kernel(q_ref, kp_ref, kc_ref, vp_ref, vc_ref, o_ref, lse_ref, *, max_dist, nq):
    first_step = pl.program_id(2) == 0
    bias = _band_bias(max_dist, None)
    bias0 = _band_bias(max_dist, first_step)
    lane = lax.broadcasted_iota(jnp.int32, (1, LANES), 1)
    lo = lane < HEAD_DIM
    hcol = lax.broadcasted_iota(jnp.int32, (1, B_HEADS), 1)
    zero = jnp.zeros((), _BF)
    for j in range(nq):
        rows = slice(j * BLOCK, (j + 1) * BLOCK)
        bj = bias0 if j == 0 else bias
        bj = jnp.concatenate([bj, bj], axis=0)
        lse_all = jnp.zeros((BLOCK, B_HEADS), _F32)
        for c in range(B_W // LANES):
            cs = slice(c * LANES, (c + 1) * LANES)
            q2 = q_ref[0, 0, rows, cs]
            k2 = _kv_window(kp_ref, kc_ref, j, (0, 0), cs)
            v2 = _kv_window(vp_ref, vc_ref, j, (0, 0), cs)
            qs = jnp.concatenate([jnp.where(lo, q2, zero), jnp.where(lo, zero, q2)], axis=0)
            s = lax.dot_general(qs, k2, (((1,), (1,)), ((), ())), preferred_element_type=_F32) + bj
            m = jnp.max(s, axis=-1, keepdims=True)
            p = jnp.exp(s - m)
            l = jnp.sum(p, axis=-1, keepdims=True)
            on = jnp.dot(p.astype(_BF), v2, preferred_element_type=_F32) * (1.0 / l)
            o_ref[0, 0, rows, cs] = jnp.where(lo, on[:BLOCK], on[BLOCK:])
            lse = m + jnp.log(l)
            lse_all = jnp.where(hcol == 2 * c, lse[:BLOCK], lse_all)
            lse_all = jnp.where(hcol == 2 * c + 1, lse[BLOCK:], lse_all)
        lse_ref[0, 0, rows, :] = lse_all


def _band_attn(q, k, v, dil):
    b, _, m, w = q.shape
    nq = min(Q_BLOCKS, m // BLOCK)
    tq = nq * BLOCK
    cur = pl.BlockSpec((1, 1, tq, w), lambda bb, r, i: (bb, r, i, 0))
    prev = pl.BlockSpec((1, 1, BLOCK, w), lambda bb, r, i: (bb, r, jnp.maximum(i * nq - 1, 0), 0))
    window, _ = [p for p in B_PATTERNS if p[1] == dil][0]
    return pl.pallas_call(
        functools.partial(_band_attn_kernel, max_dist=window // dil, nq=nq),
        grid=(b, dil, m // tq),
        in_specs=[cur, prev, cur, prev, cur],
        out_specs=[cur, pl.BlockSpec((1, 1, tq, B_HEADS), lambda bb, r, i: (bb, r, i, 0))],
        out_shape=[jax.ShapeDtypeStruct((b, dil, m, w), _F32),
                   jax.ShapeDtypeStruct((b, dil, m, B_HEADS), _F32)],
        compiler_params=_params(("parallel", "parallel", "arbitrary")),
        name=f"band_attn_d{dil}",
    )(q, k, k, v, v)


def _swa_kernel(sink_ref, q_ref, kp_ref, kc_ref, vp_ref, vc_ref, o_ref, *, nq):
    first_step = pl.program_id(1) == 0
    bias = _band_bias(A_WINDOW - 1, None)
    bias0 = _band_bias(A_WINDOW - 1, first_step)
    lane = lax.broadcasted_iota(jnp.int32, (1, LANES), 1)
    lo = lane < HEAD_DIM
    for j in range(nq):
        rows = slice(j * BLOCK, (j + 1) * BLOCK)
        bj = bias0 if j == 0 else bias
        bj = jnp.concatenate([bj] * A_GROUP, axis=0)
        k2 = _kv_window(kp_ref, kc_ref, j, (0,), slice(None))
        v2 = _kv_window(vp_ref, vc_ref, j, (0,), slice(None))
        outs = []
        for g in range(A_KV_HEADS):
            keep = lo if g == 0 else jnp.logical_not(lo)
            tiles, sinks = [], []
            for u in range(A_GROUP):
                h = g * A_GROUP + u
                qc = q_ref[0, rows, (h // 2) * LANES:(h // 2 + 1) * LANES].astype(_F32)
                if h % 2 != g:
                    qc = pltpu.roll(qc, HEAD_DIM, 1)
                tiles.append(jnp.where(keep, qc, 0.0).astype(_BF))
                sinks.append(jnp.full((BLOCK, 1), sink_ref[h], _F32))
            qs = jnp.concatenate(tiles, axis=0)
            sink = jnp.concatenate(sinks, axis=0)
            s = lax.dot_general(qs, k2, (((1,), (1,)), ((), ())), preferred_element_type=_F32) + bj
            m = jnp.maximum(jnp.max(s, axis=-1, keepdims=True), sink)
            p = jnp.exp(s - m)
            l = jnp.sum(p, axis=-1, keepdims=True) + jnp.exp(sink - m)
            on = jnp.dot(p.astype(_BF), v2, preferred_element_type=_F32) * (1.0 / l)
            for u in range(A_GROUP):
                t = on[u * BLOCK:(u + 1) * BLOCK]
                if (g * A_GROUP + u) % 2 != g:
                    t = pltpu.roll(t, HEAD_DIM, 1)
                outs.append(t)
        for c in range(A_QW // LANES):
            o_ref[0, rows, c * LANES:(c + 1) * LANES] = jnp.where(lo, outs[2 * c], outs[2 * c + 1])


def _swa_prompt(q, k, v, sinks):
    b, s, _ = q.shape
    nq = Q_BLOCKS
    tq = nq * BLOCK
    qspec = pl.BlockSpec((1, tq, A_QW), lambda bb, i: (bb, i, 0))
    cur = pl.BlockSpec((1, tq, A_KVW), lambda bb, i: (bb, i, 0))
    prev = pl.BlockSpec((1, BLOCK, A_KVW), lambda bb, i: (bb, jnp.maximum(i * nq - 1, 0), 0))
    return pl.pallas_call(
        functools.partial(_swa_kernel, nq=nq),
        grid=(b, s // tq),
        in_specs=[pl.BlockSpec(memory_space=pltpu.SMEM), qspec, prev, cur, prev, cur],
        out_specs=qspec,
        out_shape=jax.ShapeDtypeStruct((b, s, A_QW), _F32),
        compiler_params=_params(("parallel", "arbitrary")),
        name="swa_prompt",
    )(sinks, q, k, k, v, v)


def _roll_in(x_ref, tail_ref, o_ref, t_new):
    width = x_ref.shape[-1]
    lane = lax.broadcasted_iota(jnp.int32, (1, LANES), 1)
    keep = lane < LANES - t_new
    nxt = pltpu.roll(x_ref[0, :, 0:LANES], LANES - t_new, 1)
    for c in range(width // LANES):
        cur = nxt
        if c + 1 < width // LANES:
            nxt = pltpu.roll(x_ref[0, :, (c + 1) * LANES:(c + 2) * LANES], LANES - t_new, 1)
        else:
            nxt = tail_ref[0]
        o_ref[0, :, c * LANES:(c + 1) * LANES] = jnp.where(keep, cur, nxt)


def _sample_step_kernel(sink_ref, qa_ref, kan_ref, van_ref, cak_ref, cav_ref, akt_ref, avt_ref,
                        qb_ref, kbn_ref, vbn_ref, cbk_ref, cbv_ref, bkt_ref, bvt_ref,
                        oa_ref, ob_ref, nak_ref, nav_ref, nbk_ref, nbv_ref, *, t_new, wa, wb):
    rows_a = t_new * A_GROUP
    ra = lax.broadcasted_iota(jnp.int32, (rows_a, 1), 0)
    qi_a = _div(ra, A_GROUP)
    wl = lax.broadcasted_iota(jnp.int32, (1, wa), 1)
    rel_a = wa + qi_a - wl
    valid_a = (rel_a >= 0) & (rel_a < A_WINDOW)
    for g in range(A_KV_HEADS):
        ds_ = slice(g * HEAD_DIM, (g + 1) * HEAD_DIM)
        q = qa_ref[0, g]
        s = jnp.dot(q.astype(_BF), cak_ref[0, ds_, :].astype(_BF), preferred_element_type=_F32)
        s = jnp.where(valid_a, s, NEG)
        s_new = [jnp.where(qi_a >= j, jnp.sum(q * kan_ref[0, g, j:j + 1, :], axis=-1, keepdims=True), NEG)
                 for j in range(t_new)]
        sink = jnp.zeros((rows_a, 1), _F32)
        for u in range(A_GROUP):
            sink = jnp.where((ra & (A_GROUP - 1)) == u, sink_ref[g * A_GROUP + u], sink)
        m = functools.reduce(jnp.maximum, [jnp.max(s, axis=-1, keepdims=True), sink] + s_new)
        p = jnp.exp(s - m)
        p_new = [jnp.exp(z - m) for z in s_new]
        l = functools.reduce(jnp.add, [jnp.sum(p, axis=-1, keepdims=True), jnp.exp(sink - m)] + p_new)
        o = lax.dot_general(p.astype(_BF), cav_ref[0, ds_, :].astype(_BF), (((1,), (1,)), ((), ())),
                            preferred_element_type=_F32)
        for j in range(t_new):
            o = o + p_new[j] * van_ref[0, g, j:j + 1, :]
        oa_ref[0, g] = o * (1.0 / l)
    _roll_in(cak_ref, akt_ref, nak_ref, t_new)
    _roll_in(cav_ref, avt_ref, nav_ref, t_new)

    rows_b = t_new * B_HEADS
    sub = lax.broadcasted_iota(jnp.int32, (B_HEADS, B_W), 0)
    own = _div(lax.broadcasted_iota(jnp.int32, (B_HEADS, B_W), 1), HEAD_DIM) == sub
    qbd = jnp.concatenate([jnp.where(own, qb_ref[0, i:i + 1, :], 0.0) for i in range(t_new)], axis=0)
    s_all = jnp.dot(qbd.astype(_BF), cbk_ref[0].astype(_BF), preferred_element_type=_F32)
    qi = _div(lax.broadcasted_iota(jnp.int32, (rows_b, 1), 0), B_HEADS)
    s_new = [jnp.sum(qbd * kbn_ref[0, j:j + 1, :], axis=-1, keepdims=True) for j in range(t_new)]
    pats = []
    for window, dil in B_PATTERNS:
        lo_lane = max(wb - (-(-window // LANES) * LANES), 0)
        wl = lo_lane + lax.broadcasted_iota(jnp.int32, (1, wb - lo_lane), 1)
        rel = wb + qi - wl
        valid = (rel <= window) & _multiple(rel, dil)
        s = jnp.where(valid, s_all[:, lo_lane:], NEG)
        sn = [jnp.where((qi - j >= 0) & _multiple(qi - j, dil), s_new[j], NEG) for j in range(t_new)]
        m = functools.reduce(jnp.maximum, [jnp.max(s, axis=-1, keepdims=True)] + sn)
        p = jnp.exp(s - m)
        pn = [jnp.exp(z - m) for z in sn]
        l = functools.reduce(jnp.add, [jnp.sum(p, axis=-1, keepdims=True)] + pn)
        pats.append((lo_lane, p, pn, l, m + jnp.log(l)))
    mx = functools.reduce(jnp.maximum, [t[4] for t in pats])
    es = [jnp.exp(t[4] - mx) for t in pats]
    den = functools.reduce(jnp.add, es)
    coef = [e / (den * t[3]) for e, t in zip(es, pats)]
    starts = sorted({t[0] for t in pats} | {wb})
    blocks = []
    for a, b in zip(starts[:-1], starts[1:]):
        acc = None
        for c, (lo_lane, p, _, _, _) in zip(coef, pats):
            if lo_lane <= a:
                term = c * p[:, a - lo_lane:b - lo_lane]
                acc = term if acc is None else acc + term
        blocks.append(acc)
    lead = starts[0]
    p_comb = jnp.concatenate(blocks, axis=1) if len(blocks) > 1 else blocks[0]
    o_full = lax.dot_general(p_comb.astype(_BF), cbv_ref[0, :, lead:].astype(_BF), (((1,), (1,)), ((), ())),
                             preferred_element_type=_F32)
    for j in range(t_new):
        pj = functools.reduce(jnp.add, [c * t[2][j] for c, t in zip(coef, pats)])
        o_full = o_full + pj * vbn_ref[0, j:j + 1, :]
    for i in range(t_new):
        blk = jnp.where(own, o_full[i * B_HEADS:(i + 1) * B_HEADS], 0.0)
        ob_ref[0, i:i + 1, :] = jnp.sum(blk, axis=0, keepdims=True)
    _roll_in(cbk_ref, bkt_ref, nbk_ref, t_new)
    _roll_in(cbv_ref, bvt_ref, nbv_ref, t_new)


def _sample_step(sinks, qa_g, ka_n, va_n, ca_k, ca_v, ak_tail, av_tail, qb_n, kb_n, vb_n, cb_k, cb_v, bk_tail, bv_tail):
    n, t_new = qb_n.shape[0], qb_n.shape[1]
    wa, wb = ca_k.shape[2], cb_k.shape[2]
    assert wa == LANES and wb % LANES == 0 and t_new < LANES
    blk = lambda a: pl.BlockSpec((1,) + a.shape[1:], lambda i: (i,) + (0,) * (a.ndim - 1))
    ins = (qa_g, ka_n, va_n, ca_k, ca_v, ak_tail, av_tail, qb_n, kb_n, vb_n, cb_k, cb_v, bk_tail, bv_tail)
    outs = [jax.ShapeDtypeStruct(qa_g.shape, _F32), jax.ShapeDtypeStruct(qb_n.shape, _F32),
            jax.ShapeDtypeStruct(ca_k.shape, _F32), jax.ShapeDtypeStruct(ca_v.shape, _F32),
            jax.ShapeDtypeStruct(cb_k.shape, _F32), jax.ShapeDtypeStruct(cb_v.shape, _F32)]
    return pl.pallas_call(
        functools.partial(_sample_step_kernel, t_new=t_new, wa=wa, wb=wb),
        grid=(n,),
        in_specs=[pl.BlockSpec(memory_space=pltpu.SMEM)] + [blk(a) for a in ins],
        out_specs=[blk(a) for a in outs],
        out_shape=outs,
        compiler_params=_params(("parallel",)),
        name="sample_step",
    )(sinks, *ins)


_FF_CHUNKS = 2


def _merge_ffn_kernel(*refs, mix):
    if mix:
        (x_ref, oa_ref, o1_ref, o4_ref, o16_ref, l1_ref, l2_ref, l3_ref, e_ref,
         gta_ref, shf_ref, scf_ref, gtf_ref, goa_ref, gob_ref, wo_ref, gf_ref, wg_ref, wu_ref, wd_ref, gfin_ref,
         y_ref, s4_ref, s16_ref) = refs
    else:
        (x_ref, oa_ref, ob_ref,
         gta_ref, shf_ref, scf_ref, gtf_ref, goa_ref, gob_ref, wo_ref, gf_ref, wg_ref, wu_ref, wd_ref, gfin_ref,
         y_ref) = refs
    two = lambda r: r[...].reshape(-1, r.shape[-1])
    x = two(x_ref)
    if mix:
        tm = x.shape[0]
        for dil, src, stage in ((4, o4_ref, s4_ref), (16, o16_ref, s16_ref)):
            for r in range(dil):
                for c in range(B_W // LANES):
                    stage[c, pl.ds(r, tm // dil, stride=dil), :] = src[0, r, :, c * LANES:(c + 1) * LANES]
        gather = lambda stage: jnp.concatenate([stage[c] for c in range(B_W // LANES)], axis=1)
        outs = [two(o1_ref), gather(s4_ref), gather(s16_ref)]
        lses = [two(l1_ref), two(l2_ref), two(l3_ref)]
        mx = jnp.maximum(jnp.maximum(lses[0], lses[1]), lses[2])
        es = [jnp.exp(z - mx) for z in lses]
        den = es[0] + es[1] + es[2]
        ob = jnp.zeros((tm, B_W), _F32)
        for e, o in zip(es, outs):
            alpha = e / den
            hi = alpha.astype(_BF)
            lo = (alpha - hi.astype(_F32)).astype(_BF)
            wide = (jnp.dot(hi, e_ref[...], preferred_element_type=_F32)
                    + jnp.dot(lo, e_ref[...], preferred_element_type=_F32))
            ob = ob + wide * o
    else:
        ob = two(ob_ref)
    merged = jnp.concatenate([_rms(two(oa_ref), goa_ref[...]), _rms(ob, gob_ref[...])], axis=1)
    attn = jnp.dot(merged.astype(_BF), wo_ref[...], preferred_element_type=_F32)
    x1 = x + two(gta_ref) * attn
    h = (_rms(x1, gf_ref[...]) * (1.0 + two(scf_ref)) + two(shf_ref)).astype(_BF)
    dff = wg_ref.shape[1]
    fc = dff // _FF_CHUNKS
    acc = jnp.zeros_like(x1)
    for c in range(_FF_CHUNKS):
        cs = slice(c * fc, (c + 1) * fc)
        gt = jnp.dot(h, wg_ref[:, cs], preferred_element_type=_F32)
        up = jnp.dot(h, wu_ref[:, cs], preferred_element_type=_F32)
        act = (gt * jax.nn.sigmoid(gt) * up).astype(_BF)
        acc = acc + jnp.dot(act, wd_ref[cs, :], preferred_element_type=_F32)
    x2 = x1 + two(gtf_ref) * acc
    y_ref[...] = _rms(x2, gfin_ref[...]).reshape(y_ref.shape)


def _merge_ffn(x, oa, obs, lses, mods, goa, gob, wo, gf, wg, wu, wd, gfin, *, tm, prompt):
    n, l, d = x.shape
    row = lambda c: pl.BlockSpec((1, tm, c), lambda b, i: (b, i, 0))
    ins, specs, scratch = [x, oa], [row(d), row(A_QW)], []
    if prompt:
        mod_spec = pl.BlockSpec((1, 1, d), lambda b, i: (b, 0, 0))
        expand = (jnp.arange(B_W)[None, :] // HEAD_DIM == jnp.arange(B_HEADS)[:, None]).astype(_BF)
        for o in obs:
            dil = o.shape[1]
            specs.append(pl.BlockSpec((1, dil, tm // dil, B_W), lambda b, i: (b, 0, i, 0)))
        ins += list(obs) + list(lses) + [expand]
        specs += [row(B_HEADS)] * len(lses) + [_resident(expand.shape)]
        scratch = [pltpu.VMEM((B_W // LANES, tm, LANES), _F32)] * 2
    else:
        mod_spec = pl.BlockSpec((tm, d), lambda b, i: (i, 0))
        ins += list(obs)
        specs += [row(B_W)] * len(obs)
    ins += list(mods) + [goa, gob, wo, gf, wg, wu, wd, gfin]
    specs += [mod_spec] * 4 + [_resident(a.shape) for a in (goa, gob, wo, gf, wg, wu, wd, gfin)]
    return pl.pallas_call(
        functools.partial(_merge_ffn_kernel, mix=prompt),
        grid=(n, l // tm),
        in_specs=specs,
        out_specs=row(d),
        out_shape=jax.ShapeDtypeStruct((n, l, d), _F32),
        scratch_shapes=scratch,
        compiler_params=_params(("parallel", "parallel")),
        name="merge_ffn_prompt" if prompt else "merge_ffn_sample",
    )(*ins)


def _rope_tables(pos):
    half = HEAD_DIM // 2
    inv = jnp.exp(-math.log(ROPE_THETA) * jnp.arange(half, dtype=_F32) * (2.0 / HEAD_DIM))
    ang = pos.astype(_F32)[:, None] * inv[None, :]
    cos, sin = jnp.cos(ang), jnp.sin(ang)
    return jnp.tile(cos, (1, LANES // half)), jnp.tile(jnp.concatenate([-sin, sin], axis=1), (1, LANES // HEAD_DIM))


def _window_on_lanes(cache):
    _, n, w, h, dh = cache.shape
    return cache.transpose(0, 1, 3, 4, 2).reshape(n, h * dh, w)


def _window_off_lanes(t, h):
    n, _, w = t.shape
    return t.reshape(1, n, h, HEAD_DIM, w).transpose(0, 1, 4, 2, 3)


def _lane_tail(new_t):
    t = new_t.shape[0]
    return jnp.pad(new_t.transpose(1, 2, 0), ((0, 0), (0, 0), (LANES - t, 0)))


def kernel(x_prompt, x_sample, c_prompt, c_sample, cache_a_k, cache_a_v, cache_b_k, cache_b_v, w_ada, b_ada, g_attn, w_in, sinks, g_out_a, g_out_b, w_o, g_ffn, w_gate, w_up, w_down, g_final):
    nb, s, d = x_prompt.shape
    ns, t_new, _ = x_sample.shape
    assert w_ada.shape[0] == 1, "single trunk layer"
    l = 0
    bf = lambda w: w.astype(_BF)
    row = lambda g: g.reshape(1, -1)

    c_all = jnp.concatenate([c_prompt, c_sample], axis=0)
    c_all = jnp.pad(c_all, ((0, (-c_all.shape[0]) % 8), (0, 0)))
    mod = _adaln(c_all, bf(w_ada[l]), b_ada[l].reshape(1, -1))
    mod_p = [m.reshape(nb, 1, d) for m in jnp.split(mod[:nb], 6, axis=-1)]
    mod_s = jnp.split(mod[nb:nb + ns], 6, axis=-1)

    cos_p, sin_p = _rope_tables(jnp.arange(s, dtype=jnp.int32))
    cos_s, sin_s = _rope_tables(PAST_LEN + jnp.arange(t_new, dtype=jnp.int32))
    w_in_bf = bf(w_in[l])
    ffn_w = (row(g_out_a[l]), row(g_out_b[l]), bf(w_o[l]), row(g_ffn[l]), bf(w_gate[l]), bf(w_up[l]), bf(w_down[l]),
             row(g_final))

    wa_p, wb_p = min(A_WINDOW, s), min(B_WINDOW, s)
    (qa, ka, va, q1, k1, v1, q4, k4, v4, q16, k16, v16, ka_t, va_t, kb_t, vb_t) = _inproj_prompt(
        x_prompt, mod_p[0], mod_p[1], row(g_attn[l]), w_in_bf, cos_p, sin_p, tm=512, tail_a=wa_p, tail_b=wb_p)
    oa = _swa_prompt(qa, ka, va, sinks[l])
    obs, lses = [], []
    for (q, k, v), (_, dil) in zip(((q1, k1, v1), (q4, k4, v4), (q16, k16, v16)), B_PATTERNS):
        o, lse = _band_attn(q, k, v, dil)
        obs.append(o)
        lses.append(lse.transpose(0, 2, 1, 3).reshape(nb, s, B_HEADS))
    y_prompt = _merge_ffn(x_prompt, oa, obs, lses, mod_p[2:6], *ffn_w, tm=512, prompt=True)
    pa_k = ka_t.reshape(1, nb, wa_p, A_KV_HEADS, HEAD_DIM)
    pa_v = va_t.reshape(1, nb, wa_p, A_KV_HEADS, HEAD_DIM)
    pb_k = kb_t.reshape(1, nb, wb_p, B_HEADS, HEAD_DIM)
    pb_v = vb_t.reshape(1, nb, wb_p, B_HEADS, HEAD_DIM)

    xs_t = x_sample.transpose(1, 0, 2)
    qa_s, ka_s, va_s, qb_s, kb_s, vb_s = _inproj_sample(
        xs_t, mod_s[0], mod_s[1], row(g_attn[l]), w_in_bf, cos_s.reshape(t_new, 1, LANES),
        sin_s.reshape(t_new, 1, LANES))
    per_n = lambda t: t.transpose(1, 0, 2)
    qa_g = (qa_s.reshape(t_new, ns, A_KV_HEADS, A_GROUP, HEAD_DIM).transpose(1, 2, 0, 3, 4)
            .reshape(ns, A_KV_HEADS, t_new * A_GROUP, HEAD_DIM))
    kv_g = lambda t: t.reshape(t_new, ns, A_KV_HEADS, HEAD_DIM).transpose(1, 2, 0, 3)
    oa_g, ob_n, na_k, na_v, nb_k, nb_v = _sample_step(
        sinks[l], qa_g, kv_g(ka_s), kv_g(va_s), _window_on_lanes(cache_a_k), _window_on_lanes(cache_a_v),
        _lane_tail(ka_s), _lane_tail(va_s), per_n(qb_s), per_n(kb_s), per_n(vb_s),
        _window_on_lanes(cache_b_k), _window_on_lanes(cache_b_v), _lane_tail(kb_s), _lane_tail(vb_s))
    oa_t = (oa_g.reshape(ns, A_KV_HEADS, t_new, A_GROUP, HEAD_DIM).transpose(2, 0, 1, 3, 4)
            .reshape(t_new, ns, A_QW))
    y_s = _merge_ffn(xs_t, oa_t, [ob_n.transpose(1, 0, 2)], [], mod_s[2:6], *ffn_w, tm=ns, prompt=False)
    y_sample = y_s.transpose(1, 0, 2)
    sa_k, sa_v = _window_off_lanes(na_k, A_KV_HEADS), _window_off_lanes(na_v, A_KV_HEADS)
    sb_k, sb_v = _window_off_lanes(nb_k, B_HEADS), _window_off_lanes(nb_v, B_HEADS)

    return (y_prompt, y_sample, pa_k, pa_v, pb_k, pb_v, sa_k, sa_v, sb_k, sb_v)
```

```python
import functools
import math

import jax
import jax.numpy as jnp
from jax import lax
from jax.experimental import pallas as pl
from jax.experimental.pallas import tpu as pltpu

HEAD_DIM = 64
A_Q_HEADS = 8
A_KV_HEADS = 2
A_GROUP = A_Q_HEADS // A_KV_HEADS
B_HEADS = 8
A_WINDOW = 128
BLOCK = 128
B_PATTERNS = ((128, 1), (512, 4), (2048, 16))
B_WINDOW = 2048
PAST_LEN = 8192
ROPE_THETA = 10000.0
EPS = 1e-6
A_QW = A_Q_HEADS * HEAD_DIM
A_KVW = A_KV_HEADS * HEAD_DIM
B_W = B_HEADS * HEAD_DIM
SCALE = HEAD_DIM ** -0.5
LANES = 128
NEG = -1e30
VMEM_LIMIT = 56 * 1024 * 1024
Q_BLOCKS = 4

_BF = jnp.bfloat16
_F32 = jnp.float32


def _params(sem, vmem=VMEM_LIMIT):
    return pltpu.CompilerParams(dimension_semantics=sem, vmem_limit_bytes=vmem)


def _resident(shape):
    nd = len(shape)
    return pl.BlockSpec(shape, lambda *_: (0,) * nd, pipeline_mode=pl.Buffered(1))


def _rms(x, g):
    return x * lax.rsqrt(jnp.mean(x * x, axis=-1, keepdims=True) + EPS) * g


def _log2(n):
    assert n > 0 and n & (n - 1) == 0, "power of two expected"
    return n.bit_length() - 1


def _div(x, n):
    return x >> _log2(n)


def _multiple(x, n):
    _log2(n)
    return (x & (n - 1)) == 0


def _adaln_kernel(c_ref, w_ref, b_ref, o_ref):
    c = c_ref[...]
    a = (c * jax.nn.sigmoid(c)).astype(_BF)
    o_ref[...] = jnp.dot(a, w_ref[...], preferred_element_type=_F32) + b_ref[...]


def _adaln(c_all, w_bf, b):
    m, d = c_all.shape
    n = w_bf.shape[1]
    tn = 1536
    return pl.pallas_call(
        _adaln_kernel,
        grid=(n // tn,),
        in_specs=[pl.BlockSpec((m, d), lambda j: (0, 0)),
                  pl.BlockSpec((d, tn), lambda j: (0, j)),
                  pl.BlockSpec((1, tn), lambda j: (0, j))],
        out_specs=pl.BlockSpec((m, tn), lambda j: (0, j)),
        out_shape=jax.ShapeDtypeStruct((m, n), _F32),
        compiler_params=_params(("arbitrary",)),
        name="adaln",
    )(c_all, w_bf, b)


def _rope(x, cos, sin_signed, first_half):
    sw = jnp.where(first_half, pltpu.roll(x, 96, 1), pltpu.roll(x, 32, 1))
    return x * cos + sw * sin_signed


def _project(x_ref, sh_ref, sc_ref, g_ref, w_ref, cos_ref, sin_ref):
    d = x_ref.shape[-1]
    x = x_ref[...].reshape(-1, d)
    sh = sh_ref[...].reshape(-1, d)
    sc = sc_ref[...].reshape(-1, d)
    h = _rms(x, g_ref[...]) * (1.0 + sc) + sh
    proj = jnp.dot(h.astype(_BF), w_ref[...], preferred_element_type=_F32)
    cos = cos_ref[...].reshape(-1, LANES)
    sin = sin_ref[...].reshape(-1, LANES)
    lane = lax.broadcasted_iota(jnp.int32, (1, LANES), 1)
    first_half = (lane & (HEAD_DIM - 1)) < (HEAD_DIM // 2)

    def rot(col0, ncols):
        parts = [_rope(proj[:, col0 + j * LANES: col0 + (j + 1) * LANES], cos, sin, first_half)
                 for j in range(ncols // LANES)]
        return parts[0] if len(parts) == 1 else jnp.concatenate(parts, axis=1)

    o = 0
    qa = rot(o, A_QW) * SCALE; o += A_QW
    ka = rot(o, A_KVW); o += A_KVW
    va = proj[:, o:o + A_KVW]; o += A_KVW
    qb = rot(o, B_W) * SCALE; o += B_W
    kb = rot(o, B_W); o += B_W
    vb = proj[:, o:o + B_W]
    return qa, ka, va, qb, kb, vb


def _put(ref, val):
    ref[...] = val.astype(ref.dtype).reshape(ref.shape)


def _inproj_prompt_kernel(x_ref, sh_ref, sc_ref, g_ref, w_ref, cos_ref, sin_ref,
                          qa_ref, ka_ref, va_ref, q1_ref, k1_ref, v1_ref, q4_ref, k4_ref, v4_ref,
                          q16_ref, k16_ref, v16_ref, kat_ref, vat_ref, kbt_ref, vbt_ref,
                          sq_ref, sk_ref, sv_ref):
    qa, ka, va, qb, kb, vb = _project(x_ref, sh_ref, sc_ref, g_ref, w_ref, cos_ref, sin_ref)
    tm = qa.shape[0]
    _put(qa_ref, qa); _put(ka_ref, ka); _put(va_ref, va)
    _put(q1_ref, qb); _put(k1_ref, kb); _put(v1_ref, vb)
    ta = kat_ref.shape[1]
    _put(kat_ref, ka[tm - ta:]); _put(vat_ref, va[tm - ta:])
    _put(kbt_ref, kb); _put(vbt_ref, vb)
    for val, stage, r4, r16 in ((qb, sq_ref, q4_ref, q16_ref), (kb, sk_ref, k4_ref, k16_ref),
                                (vb, sv_ref, v4_ref, v16_ref)):
        for c in range(B_W // LANES):
            stage[c] = val[:, c * LANES:(c + 1) * LANES]
        for dil, out in ((4, r4), (16, r16)):
            n = tm // dil
            for r in range(dil):
                for c in range(B_W // LANES):
                    out[0, r, :, c * LANES:(c + 1) * LANES] = stage[c, pl.ds(r, n, stride=dil), :].astype(out.dtype)


def _inproj_prompt(x, sh, sc, g, w_bf, cos, sin, *, tm, tail_a, tail_b):
    b, s, d = x.shape
    nt = s // tm
    assert tail_b % tm == 0 and tail_a <= tm
    row = lambda c: pl.BlockSpec((1, tm, c), lambda bb, i: (bb, i, 0))
    res = lambda dil: pl.BlockSpec((1, dil, tm // dil, B_W), lambda bb, i: (bb, 0, i, 0))
    mod_spec = pl.BlockSpec((1, 1, d), lambda bb, i: (bb, 0, 0))
    tab_spec = pl.BlockSpec((tm, LANES), lambda bb, i: (i, 0))
    tail_a_spec = pl.BlockSpec((1, tail_a, A_KVW), lambda bb, i: (bb, 0, 0))
    tail_b_spec = pl.BlockSpec((1, tm, B_W), lambda bb, i: (bb, jnp.maximum(i - (nt - tail_b // tm), 0), 0))
    shp = lambda c, dt: jax.ShapeDtypeStruct((b, s, c), dt)
    rshp = lambda dil: jax.ShapeDtypeStruct((b, dil, s // dil, B_W), _BF)
    return pl.pallas_call(
        _inproj_prompt_kernel,
        grid=(b, nt),
        in_specs=[row(d), mod_spec, mod_spec, _resident((1, d)), _resident(w_bf.shape), tab_spec, tab_spec],
        out_specs=[row(A_QW), row(A_KVW), row(A_KVW)] + [res(1)] * 3 + [res(4)] * 3 + [res(16)] * 3
                  + [tail_a_spec] * 2 + [tail_b_spec] * 2,
        out_shape=[shp(A_QW, _BF), shp(A_KVW, _BF), shp(A_KVW, _BF)] + [rshp(1)] * 3 + [rshp(4)] * 3 + [rshp(16)] * 3
                  + [jax.ShapeDtypeStruct((b, tail_a, A_KVW), _F32)] * 2
                  + [jax.ShapeDtypeStruct((b, tail_b, B_W), _F32)] * 2,
        scratch_shapes=[pltpu.VMEM((B_W // LANES, tm, LANES), _F32)] * 3,
        compiler_params=_params(("parallel", "arbitrary")),
        name="in_proj_prompt",
    )(x, sh, sc, g, w_bf, cos, sin)


def _inproj_sample_kernel(x_ref, sh_ref, sc_ref, g_ref, w_ref, cos_ref, sin_ref,
                          qa_ref, ka_ref, va_ref, qb_ref, kb_ref, vb_ref):
    vals = _project(x_ref, sh_ref, sc_ref, g_ref, w_ref, cos_ref, sin_ref)
    for ref, val in zip((qa_ref, ka_ref, va_ref, qb_ref, kb_ref, vb_ref), vals):
        _put(ref, val)


def _inproj_sample(x, sh, sc, g, w_bf, cos, sin):
    t, n, d = x.shape
    row = lambda c: pl.BlockSpec((1, n, c), lambda j: (j, 0, 0))
    mod_spec = pl.BlockSpec((n, d), lambda j: (0, 0))
    tab_spec = pl.BlockSpec((1, 1, LANES), lambda j: (j, 0, 0))
    widths = (A_QW, A_KVW, A_KVW, B_W, B_W, B_W)
    return pl.pallas_call(
        _inproj_sample_kernel,
        grid=(t,),
        in_specs=[row(d), mod_spec, mod_spec, _resident((1, d)), _resident(w_bf.shape), tab_spec, tab_spec],
        out_specs=[row(c) for c in widths],
        out_shape=[jax.ShapeDtypeStruct((t, n, c), _F32) for c in widths],
        compiler_params=_params(("parallel",)),
        name="in_proj_sample",
    )(x, sh, sc, g, w_bf, cos, sin)


def _band_bias(max_dist, drop_prev):
    qi = lax.broadcasted_iota(jnp.int32, (BLOCK, 2 * BLOCK), 0)
    ki = lax.broadcasted_iota(jnp.int32, (BLOCK, 2 * BLOCK), 1)
    rel = qi + BLOCK - ki
    valid = (rel >= 0) & (rel <= max_dist)
    if drop_prev is not None:
        valid = valid & ((ki >= BLOCK) | jnp.logical_not(drop_prev))
    return jnp.where(valid, 0.0, NEG).astype(_F32)


def _kv_window(prev_ref, cur_ref, j, lead, cols):
    if j == 0:
        return jnp.concatenate([prev_ref[lead + (slice(None), cols)], cur_ref[lead + (slice(0, BLOCK), cols)]], axis=0)
    return cur_ref[lead + (slice((j - 1) * BLOCK, (j + 1) * BLOCK), cols)]


def _band_attn_kernel(q_ref, kp_ref, kc_ref, vp_ref, vc_ref, o_ref, lse_ref, *, max_dist, nq):
    first_step = pl.program_id(2) == 0
    bias = _band_bias(max_dist, None)
    bias0 = _band_bias(max_dist, first_step)
    lane = lax.broadcasted_iota(jnp.int32, (1, LANES), 1)
    lo = lane < HEAD_DIM
    hcol = lax.broadcasted_iota(jnp.int32, (1, B_HEADS), 1)
    zero = jnp.zeros((), _BF)
    for j in range(nq):
        rows = slice(j * BLOCK, (j + 1) * BLOCK)
        bj = bias0 if j == 0 else bias
        bj = jnp.concatenate([bj, bj], axis=0)
        lse_all = jnp.zeros((BLOCK, B_HEADS), _F32)
        for c in range(B_W // LANES):
            cs = slice(c * LANES, (c + 1) * LANES)
            q2 = q_ref[0, 0, rows, cs]
            k2 = _kv_window(kp_ref, kc_ref, j, (0, 0), cs)
            v2 = _kv_window(vp_ref, vc_ref, j, (0, 0), cs)
            qs = jnp.concatenate([jnp.where(lo, q2, zero), jnp.where(lo, zero, q2)], axis=0)
            s = lax.dot_general(qs, k2, (((1,), (1,)), ((), ())), preferred_element_type=_F32) + bj
            m = jnp.max(s, axis=-1, keepdims=True)
            p = jnp.exp(s - m)
            l = jnp.sum(p, axis=-1, keepdims=True)
            on = jnp.dot(p.astype(_BF), v2, preferred_element_type=_F32) * (1.0 / l)
            o_ref[0, 0, rows, cs] = jnp.where(lo, on[:BLOCK], on[BLOCK:])
            lse = m + jnp.log(l)
            lse_all = jnp.where(hcol == 2 * c, lse[:BLOCK], lse_all)
            lse_all = jnp.where(hcol == 2 * c + 1, lse[BLOCK:], lse_all)
        lse_ref[0, 0, rows, :] = lse_all


def _band_attn(q, k, v, dil):
    b, _, m, w = q.shape
    nq = min(Q_BLOCKS, m // BLOCK)
    tq = nq * BLOCK
    cur = pl.BlockSpec((1, 1, tq, w), lambda bb, r, i: (bb, r, i, 0))
    prev = pl.BlockSpec((1, 1, BLOCK, w), lambda bb, r, i: (bb, r, jnp.maximum(i * nq - 1, 0), 0))
    window, _ = [p for p in B_PATTERNS if p[1] == dil][0]
    return pl.pallas_call(
        functools.partial(_band_attn_kernel, max_dist=window // dil, nq=nq),
        grid=(b, dil, m // tq),
        in_specs=[cur, prev, cur, prev, cur],
        out_specs=[cur, pl.BlockSpec((1, 1, tq, B_HEADS), lambda bb, r, i: (bb, r, i, 0))],
        out_shape=[jax.ShapeDtypeStruct((b, dil, m, w), _F32),
                   jax.ShapeDtypeStruct((b, dil, m, B_HEADS), _F32)],
        compiler_params=_params(("parallel", "parallel", "arbitrary")),
        name=f"band_attn_d{dil}",
    )(q, k, k, v, v)


def _swa_kernel(sink_ref, q_ref, kp_ref, kc_ref, vp_ref, vc_ref, o_ref, *, nq):
    first_step = pl.program_id(1) == 0
    bias = _band_bias(A_WINDOW - 1, None)
    bias0 = _band_bias(A_WINDOW - 1, first_step)
    lane = lax.broadcasted_iota(jnp.int32, (1, LANES), 1)
    lo = lane < HEAD_DIM
    for j in range(nq):
        rows = slice(j * BLOCK, (j + 1) * BLOCK)
        bj = bias0 if j == 0 else bias
        bj = jnp.concatenate([bj, bj], axis=0)
        k2 = _kv_window(kp_ref, kc_ref, j, (0,), slice(None))
        v2 = _kv_window(vp_ref, vc_ref, j, (0,), slice(None))
        for c in range(A_QW // LANES):
            g = (2 * c) // A_GROUP
            keep = lo if g == 0 else jnp.logical_not(lo)
            qc = q_ref[0, rows, c * LANES:(c + 1) * LANES].astype(_F32)
            qr = pltpu.roll(qc, HEAD_DIM, 1)
            tiles = [jnp.where(keep, qc if half == g else qr, 0.0).astype(_BF) for half in range(2)]
            qs = jnp.concatenate(tiles, axis=0)
            s = lax.dot_general(qs, k2, (((1,), (1,)), ((), ())), preferred_element_type=_F32) + bj
            ps, rls = [], []
            for half in range(2):
                sink = sink_ref[2 * c + half]
                sh = s[half * BLOCK:(half + 1) * BLOCK]
                m = jnp.maximum(jnp.max(sh, axis=-1, keepdims=True), sink)
                p = jnp.exp(sh - m)
                ps.append(p)
                rls.append(1.0 / (jnp.sum(p, axis=-1, keepdims=True) + jnp.exp(sink - m)))
            p = jnp.concatenate(ps, axis=0).astype(_BF)
            pv = jnp.dot(p, v2, preferred_element_type=_F32)
            halves = [pv[half * BLOCK:(half + 1) * BLOCK] * rls[half] for half in range(2)]
            halves = [t if half == g else pltpu.roll(t, HEAD_DIM, 1) for half, t in enumerate(halves)]
            o_ref[0, rows, c * LANES:(c + 1) * LANES] = jnp.where(lo, halves[0], halves[1])


def _swa_prompt(q, k, v, sinks):
    b, s, _ = q.shape
    nq = Q_BLOCKS
    tq = nq * BLOCK
    qspec = pl.BlockSpec((1, tq, A_QW), lambda bb, i: (bb, i, 0))
    cur = pl.BlockSpec((1, tq, A_KVW), lambda bb, i: (bb, i, 0))
    prev = pl.BlockSpec((1, BLOCK, A_KVW), lambda bb, i: (bb, jnp.maximum(i * nq - 1, 0), 0))
    return pl.pallas_call(
        functools.partial(_swa_kernel, nq=nq),
        grid=(b, s // tq),
        in_specs=[pl.BlockSpec(memory_space=pltpu.SMEM), qspec, prev, cur, prev, cur],
        out_specs=qspec,
        out_shape=jax.ShapeDtypeStruct((b, s, A_QW), _F32),
        compiler_params=_params(("parallel", "arbitrary")),
        name="swa_prompt",
    )(sinks, q, k, k, v, v)


def _roll_in(x_ref, new8_ref, o_ref, t_new):
    width = x_ref.shape[-1]
    new8 = new8_ref[0]
    tail = jnp.concatenate([jnp.zeros((LANES - new8.shape[0], new8.shape[1]), _F32), new8], axis=0).T
    lane = lax.broadcasted_iota(jnp.int32, (1, LANES), 1)
    keep = lane < LANES - t_new
    nxt = pltpu.roll(x_ref[0, :, 0:LANES], LANES - t_new, 1)
    for c in range(width // LANES):
        cur = nxt
        if c + 1 < width // LANES:
            nxt = pltpu.roll(x_ref[0, :, (c + 1) * LANES:(c + 2) * LANES], LANES - t_new, 1)
        else:
            nxt = tail
        o_ref[0, :, c * LANES:(c + 1) * LANES] = jnp.where(keep, cur, nxt)


def _sample_step_kernel(sink_ref, qa_ref, kan_ref, van_ref, cak_ref, cav_ref, ak8_ref, av8_ref,
                        qb_ref, kbn_ref, vbn_ref, cbk_ref, cbv_ref,
                        oa_ref, ob_ref, nak_ref, nav_ref, nbk_ref, nbv_ref, *, t_new, wa, wb):
    r0 = kbn_ref.shape[1] - t_new
    rows_a = t_new * A_GROUP
    ra = lax.broadcasted_iota(jnp.int32, (rows_a, 1), 0)
    qi_a = _div(ra, A_GROUP)
    wl = lax.broadcasted_iota(jnp.int32, (1, wa), 1)
    rel_a = wa + qi_a - wl
    valid_a = (rel_a >= 0) & (rel_a < A_WINDOW)
    for g in range(A_KV_HEADS):
        ds_ = slice(g * HEAD_DIM, (g + 1) * HEAD_DIM)
        q = qa_ref[0, g]
        s = jnp.dot(q.astype(_BF), cak_ref[0, ds_, :].astype(_BF), preferred_element_type=_F32)
        s = jnp.where(valid_a, s, NEG)
        s_new = [jnp.where(qi_a >= j, jnp.sum(q * kan_ref[0, g, j:j + 1, :], axis=-1, keepdims=True), NEG)
                 for j in range(t_new)]
        sink = jnp.zeros((rows_a, 1), _F32)
        for u in range(A_GROUP):
            sink = jnp.where((ra & (A_GROUP - 1)) == u, sink_ref[g * A_GROUP + u], sink)
        m = functools.reduce(jnp.maximum, [jnp.max(s, axis=-1, keepdims=True), sink] + s_new)
        p = jnp.exp(s - m)
        p_new = [jnp.exp(z - m) for z in s_new]
        l = functools.reduce(jnp.add, [jnp.sum(p, axis=-1, keepdims=True), jnp.exp(sink - m)] + p_new)
        o = lax.dot_general(p.astype(_BF), cav_ref[0, ds_, :].astype(_BF), (((1,), (1,)), ((), ())),
                            preferred_element_type=_F32)
        for j in range(t_new):
            o = o + p_new[j] * van_ref[0, g, j:j + 1, :]
        oa_ref[0, g] = o * (1.0 / l)
    _roll_in(cak_ref, ak8_ref, nak_ref, t_new)
    _roll_in(cav_ref, av8_ref, nav_ref, t_new)

    rows_b = t_new * B_HEADS
    sub = lax.broadcasted_iota(jnp.int32, (B_HEADS, B_W), 0)
    own = _div(lax.broadcasted_iota(jnp.int32, (B_HEADS, B_W), 1), HEAD_DIM) == sub
    qbd = jnp.concatenate([jnp.where(own, qb_ref[0, i:i + 1, :], 0.0) for i in range(t_new)], axis=0)
    s_all = jnp.dot(qbd.astype(_BF), cbk_ref[0].astype(_BF), preferred_element_type=_F32)
    qi = _div(lax.broadcasted_iota(jnp.int32, (rows_b, 1), 0), B_HEADS)
    s_new = [jnp.sum(qbd * kbn_ref[0, r0 + j:r0 + j + 1, :], axis=-1, keepdims=True) for j in range(t_new)]
    pats = []
    for window, dil in B_PATTERNS:
        lo_lane = max(wb - (-(-window // LANES) * LANES), 0)
        wl = lo_lane + lax.broadcasted_iota(jnp.int32, (1, wb - lo_lane), 1)
        rel = wb + qi - wl
        valid = (rel <= window) & _multiple(rel, dil)
        s = jnp.where(valid, s_all[:, lo_lane:], NEG)
        sn = [jnp.where((qi - j >= 0) & _multiple(qi - j, dil), s_new[j], NEG) for j in range(t_new)]
        m = functools.reduce(jnp.maximum, [jnp.max(s, axis=-1, keepdims=True)] + sn)
        p = jnp.exp(s - m)
        pn = [jnp.exp(z - m) for z in sn]
        l = functools.reduce(jnp.add, [jnp.sum(p, axis=-1, keepdims=True)] + pn)
        pats.append((lo_lane, p, pn, l, m + jnp.log(l)))
    mx = functools.reduce(jnp.maximum, [t[4] for t in pats])
    es = [jnp.exp(t[4] - mx) for t in pats]
    den = functools.reduce(jnp.add, es)
    coef = [e / (den * t[3]) for e, t in zip(es, pats)]
    starts = sorted({t[0] for t in pats} | {wb})
    blocks = []
    for a, b in zip(starts[:-1], starts[1:]):
        acc = None
        for c, (lo_lane, p, _, _, _) in zip(coef, pats):
            if lo_lane <= a:
                term = c * p[:, a - lo_lane:b - lo_lane]
                acc = term if acc is None else acc + term
        blocks.append(acc)
    lead = starts[0]
    p_comb = jnp.concatenate(blocks, axis=1) if len(blocks) > 1 else blocks[0]
    o_full = lax.dot_general(p_comb.astype(_BF), cbv_ref[0, :, lead:].astype(_BF), (((1,), (1,)), ((), ())),
                             preferred_element_type=_F32)
    for j in range(t_new):
        pj = functools.reduce(jnp.add, [c * t[2][j] for c, t in zip(coef, pats)])
        o_full = o_full + pj * vbn_ref[0, r0 + j:r0 + j + 1, :]
    for i in range(t_new):
        blk = jnp.where(own, o_full[i * B_HEADS:(i + 1) * B_HEADS], 0.0)
        ob_ref[0, i:i + 1, :] = jnp.sum(blk, axis=0, keepdims=True)
    _roll_in(cbk_ref, kbn_ref, nbk_ref, t_new)
    _roll_in(cbv_ref, vbn_ref, nbv_ref, t_new)


def _sample_step(sinks, qa_g, ka_n, va_n, ca_k, ca_v, ak8, av8, qb_n, kb8, vb8, cb_k, cb_v):
    n, t_new = qb_n.shape[0], qb_n.shape[1]
    wa, wb = ca_k.shape[2], cb_k.shape[2]
    assert wa == LANES and wb % LANES == 0 and t_new <= 8
    blk = lambda a: pl.BlockSpec((1,) + a.shape[1:], lambda i: (i,) + (0,) * (a.ndim - 1))
    ins = (qa_g, ka_n, va_n, ca_k, ca_v, ak8, av8, qb_n, kb8, vb8, cb_k, cb_v)
    outs = [jax.ShapeDtypeStruct(qa_g.shape, _F32), jax.ShapeDtypeStruct(qb_n.shape, _F32),
            jax.ShapeDtypeStruct(ca_k.shape, _F32), jax.ShapeDtypeStruct(ca_v.shape, _F32),
            jax.ShapeDtypeStruct(cb_k.shape, _F32), jax.ShapeDtypeStruct(cb_v.shape, _F32)]
    return pl.pallas_call(
        functools.partial(_sample_step_kernel, t_new=t_new, wa=wa, wb=wb),
        grid=(n,),
        in_specs=[pl.BlockSpec(memory_space=pltpu.SMEM)] + [blk(a) for a in ins],
        out_specs=[blk(a) for a in outs],
        out_shape=outs,
        compiler_params=_params(("parallel",)),
        name="sample_step",
    )(sinks, *ins)


_FF_CHUNKS = 2


def _merge_ffn_kernel(*refs, mix):
    if mix:
        (x_ref, oa_ref, o1_ref, o4_ref, o16_ref, l1_ref, l2_ref, l3_ref, e_ref,
         gta_ref, shf_ref, scf_ref, gtf_ref, goa_ref, gob_ref, wo_ref, gf_ref, wg_ref, wu_ref, wd_ref, gfin_ref,
         y_ref, s4_ref, s16_ref) = refs
    else:
        (x_ref, oa_ref, ob_ref,
         gta_ref, shf_ref, scf_ref, gtf_ref, goa_ref, gob_ref, wo_ref, gf_ref, wg_ref, wu_ref, wd_ref, gfin_ref,
         y_ref) = refs
    two = lambda r: r[...].reshape(-1, r.shape[-1])
    x = two(x_ref)
    if mix:
        tm = x.shape[0]
        for dil, src, stage in ((4, o4_ref, s4_ref), (16, o16_ref, s16_ref)):
            for r in range(dil):
                for c in range(B_W // LANES):
                    stage[c, pl.ds(r, tm // dil, stride=dil), :] = src[0, r, :, c * LANES:(c + 1) * LANES]
        gather = lambda stage: jnp.concatenate([stage[c] for c in range(B_W // LANES)], axis=1)
        outs = [two(o1_ref), gather(s4_ref), gather(s16_ref)]
        lses = [two(l1_ref), two(l2_ref), two(l3_ref)]
        mx = jnp.maximum(jnp.maximum(lses[0], lses[1]), lses[2])
        es = [jnp.exp(z - mx) for z in lses]
        den = es[0] + es[1] + es[2]
        ob = jnp.zeros((tm, B_W), _F32)
        for e, o in zip(es, outs):
            alpha = e / den
            hi = alpha.astype(_BF)
            lo = (alpha - hi.astype(_F32)).astype(_BF)
            wide = (jnp.dot(hi, e_ref[...], preferred_element_type=_F32)
                    + jnp.dot(lo, e_ref[...], preferred_element_type=_F32))
            ob = ob + wide * o
    else:
        ob = two(ob_ref)
    merged = jnp.concatenate([_rms(two(oa_ref), goa_ref[...]), _rms(ob, gob_ref[...])], axis=1)
    attn = jnp.dot(merged.astype(_BF), wo_ref[...], preferred_element_type=_F32)
    x1 = x + two(gta_ref) * attn
    h = (_rms(x1, gf_ref[...]) * (1.0 + two(scf_ref)) + two(shf_ref)).astype(_BF)
    dff = wg_ref.shape[1]
    fc = dff // _FF_CHUNKS
    acc = jnp.zeros_like(x1)
    for c in range(_FF_CHUNKS):
        cs = slice(c * fc, (c + 1) * fc)
        gt = jnp.dot(h, wg_ref[:, cs], preferred_element_type=_F32)
        up = jnp.dot(h, wu_ref[:, cs], preferred_element_type=_F32)
        act = (gt * jax.nn.sigmoid(gt) * up).astype(_BF)
        acc = acc + jnp.dot(act, wd_ref[cs, :], preferred_element_type=_F32)
    x2 = x1 + two(gtf_ref) * acc
    y_ref[...] = _rms(x2, gfin_ref[...]).reshape(y_ref.shape)


def _merge_ffn(x, oa, obs, lses, mods, goa, gob, wo, gf, wg, wu, wd, gfin, *, tm, prompt):
    n, l, d = x.shape
    row = lambda c: pl.BlockSpec((1, tm, c), lambda b, i: (b, i, 0))
    ins, specs, scratch = [x, oa], [row(d), row(A_QW)], []
    if prompt:
        mod_spec = pl.BlockSpec((1, 1, d), lambda b, i: (b, 0, 0))
        expand = (jnp.arange(B_W)[None, :] // HEAD_DIM == jnp.arange(B_HEADS)[:, None]).astype(_BF)
        for o in obs:
            dil = o.shape[1]
            specs.append(pl.BlockSpec((1, dil, tm // dil, B_W), lambda b, i: (b, 0, i, 0)))
        ins += list(obs) + list(lses) + [expand]
        specs += [row(B_HEADS)] * len(lses) + [_resident(expand.shape)]
        scratch = [pltpu.VMEM((B_W // LANES, tm, LANES), _F32)] * 2
    else:
        mod_spec = pl.BlockSpec((tm, d), lambda b, i: (i, 0))
        ins += list(obs)
        specs += [row(B_W)] * len(obs)
    ins += list(mods) + [goa, gob, wo, gf, wg, wu, wd, gfin]
    specs += [mod_spec] * 4 + [_resident(a.shape) for a in (goa, gob, wo, gf, wg, wu, wd, gfin)]
    return pl.pallas_call(
        functools.partial(_merge_ffn_kernel, mix=prompt),
        grid=(n, l // tm),
        in_specs=specs,
        out_specs=row(d),
        out_shape=jax.ShapeDtypeStruct((n, l, d), _F32),
        scratch_shapes=scratch,
        compiler_params=_params(("parallel", "parallel")),
        name="merge_ffn_prompt" if prompt else "merge_ffn_sample",
    )(*ins)


def _rope_tables(pos):
    half = HEAD_DIM // 2
    inv = jnp.exp(-math.log(ROPE_THETA) * jnp.arange(half, dtype=_F32) * (2.0 / HEAD_DIM))
    ang = pos.astype(_F32)[:, None] * inv[None, :]
    cos, sin = jnp.cos(ang), jnp.sin(ang)
    return jnp.tile(cos, (1, LANES // half)), jnp.tile(jnp.concatenate([-sin, sin], axis=1), (1, LANES // HEAD_DIM))


def _window_on_lanes(cache):
    _, n, w, h, dh = cache.shape
    return cache.transpose(0, 1, 3, 4, 2).reshape(n, h * dh, w)


def _window_off_lanes(t, h):
    n, _, w = t.shape
    return t.reshape(1, n, h, HEAD_DIM, w).transpose(0, 1, 4, 2, 3)


def _last_rows(new_t):
    t = new_t.shape[0]
    return jnp.pad(new_t.transpose(1, 0, 2), ((0, 0), (8 - t, 0), (0, 0)))


def kernel(x_prompt, x_sample, c_prompt, c_sample, cache_a_k, cache_a_v, cache_b_k, cache_b_v, w_ada, b_ada, g_attn, w_in, sinks, g_out_a, g_out_b, w_o, g_ffn, w_gate, w_up, w_down, g_final):
    nb, s, d = x_prompt.shape
    ns, t_new, _ = x_sample.shape
    assert w_ada.shape[0] == 1, "single trunk layer"
    l = 0
    bf = lambda w: w.astype(_BF)
    row = lambda g: g.reshape(1, -1)

    c_all = jnp.concatenate([c_prompt, c_sample], axis=0)
    c_all = jnp.pad(c_all, ((0, (-c_all.shape[0]) % 8), (0, 0)))
    mod = _adaln(c_all, bf(w_ada[l]), b_ada[l].reshape(1, -1))
    mod_p = [m.reshape(nb, 1, d) for m in jnp.split(mod[:nb], 6, axis=-1)]
    mod_s = jnp.split(mod[nb:nb + ns], 6, axis=-1)

    cos_p, sin_p = _rope_tables(jnp.arange(s, dtype=jnp.int32))
    cos_s, sin_s = _rope_tables(PAST_LEN + jnp.arange(t_new, dtype=jnp.int32))
    w_in_bf = bf(w_in[l])
    ffn_w = (row(g_out_a[l]), row(g_out_b[l]), bf(w_o[l]), row(g_ffn[l]), bf(w_gate[l]), bf(w_up[l]), bf(w_down[l]),
             row(g_final))

    wa_p, wb_p = min(A_WINDOW, s), min(B_WINDOW, s)
    (qa, ka, va, q1, k1, v1, q4, k4, v4, q16, k16, v16, ka_t, va_t, kb_t, vb_t) = _inproj_prompt(
        x_prompt, mod_p[0], mod_p[1], row(g_attn[l]), w_in_bf, cos_p, sin_p, tm=512, tail_a=wa_p, tail_b=wb_p)
    oa = _swa_prompt(qa, ka, va, sinks[l])
    obs, lses = [], []
    for (q, k, v), (_, dil) in zip(((q1, k1, v1), (q4, k4, v4), (q16, k16, v16)), B_PATTERNS):
        o, lse = _band_attn(q, k, v, dil)
        obs.append(o)
        lses.append(lse.transpose(0, 2, 1, 3).reshape(nb, s, B_HEADS))
    y_prompt = _merge_ffn(x_prompt, oa, obs, lses, mod_p[2:6], *ffn_w, tm=512, prompt=True)
    pa_k = ka_t.reshape(1, nb, wa_p, A_KV_HEADS, HEAD_DIM)
    pa_v = va_t.reshape(1, nb, wa_p, A_KV_HEADS, HEAD_DIM)
    pb_k = kb_t.reshape(1, nb, wb_p, B_HEADS, HEAD_DIM)
    pb_v = vb_t.reshape(1, nb, wb_p, B_HEADS, HEAD_DIM)

    xs_t = x_sample.transpose(1, 0, 2)
    qa_s, ka_s, va_s, qb_s, kb_s, vb_s = _inproj_sample(
        xs_t, mod_s[0], mod_s[1], row(g_attn[l]), w_in_bf, cos_s.reshape(t_new, 1, LANES),
        sin_s.reshape(t_new, 1, LANES))
    per_n = lambda t: t.transpose(1, 0, 2)
    qa_g = (qa_s.reshape(t_new, ns, A_KV_HEADS, A_GROUP, HEAD_DIM).transpose(1, 2, 0, 3, 4)
            .reshape(ns, A_KV_HEADS, t_new * A_GROUP, HEAD_DIM))
    kv_g = lambda t: t.reshape(t_new, ns, A_KV_HEADS, HEAD_DIM).transpose(1, 2, 0, 3)
    oa_g, ob_n, na_k, na_v, nb_k, nb_v = _sample_step(
        sinks[l], qa_g, kv_g(ka_s), kv_g(va_s), _window_on_lanes(cache_a_k), _window_on_lanes(cache_a_v),
        _last_rows(ka_s), _last_rows(va_s), per_n(qb_s), _last_rows(kb_s), _last_rows(vb_s),
        _window_on_lanes(cache_b_k), _window_on_lanes(cache_b_v))
    oa_t = (oa_g.reshape(ns, A_KV_HEADS, t_new, A_GROUP, HEAD_DIM).transpose(2, 0, 1, 3, 4)
            .reshape(t_new, ns, A_QW))
    y_s = _merge_ffn(xs_t, oa_t, [ob_n.transpose(1, 0, 2)], [], mod_s[2:6], *ffn_w, tm=ns, prompt=False)
    y_sample = y_s.transpose(1, 0, 2)
    sa_k, sa_v = _window_off_lanes(na_k, A_KV_HEADS), _window_off_lanes(na_v, A_KV_HEADS)
    sb_k, sb_v = _window_off_lanes(nb_k, B_HEADS), _window_off_lanes(nb_v, B_HEADS)

    return (y_prompt, y_sample, pa_k, pa_v, pb_k, pb_v, sa_k, sa_v, sb_k, sb_v)
```

```python
import functools
import math

import jax
import jax.numpy as jnp
from jax import lax
from jax.experimental import pallas as pl
from jax.experimental.pallas import tpu as pltpu

HEAD_DIM = 64
A_Q_HEADS = 8
A_KV_HEADS = 2
A_GROUP = A_Q_HEADS // A_KV_HEADS
B_HEADS = 8
A_WINDOW = 128
BLOCK = 128
B_PATTERNS = ((128, 1), (512, 4), (2048, 16))
B_WINDOW = 2048
PAST_LEN = 8192
ROPE_THETA = 10000.0
EPS = 1e-6
A_QW = A_Q_HEADS * HEAD_DIM
A_KVW = A_KV_HEADS * HEAD_DIM
B_W = B_HEADS * HEAD_DIM
SCALE = HEAD_DIM ** -0.5
LANES = 128
SUBLANES = 8
NEG = -1e30
VMEM_LIMIT = 56 * 1024 * 1024
Q_BLOCKS = 4

_BF = jnp.bfloat16
_F32 = jnp.float32


def _params(sem, vmem=VMEM_LIMIT):
    return pltpu.CompilerParams(dimension_semantics=sem, vmem_limit_bytes=vmem)


def _resident(shape):
    nd = len(shape)
    return pl.BlockSpec(shape, lambda *_: (0,) * nd, pipeline_mode=pl.Buffered(1))


def _rms(x, g):
    return x * lax.rsqrt(jnp.mean(x * x, axis=-1, keepdims=True) + EPS) * g


def _log2(n):
    assert n > 0 and n & (n - 1) == 0, "power of two expected"
    return n.bit_length() - 1


def _div(x, n):
    return x >> _log2(n)


def _multiple(x, n):
    _log2(n)
    return (x & (n - 1)) == 0


def _adaln_kernel(c_ref, w_ref, b_ref, o_ref):
    c = c_ref[...]
    a = (c * jax.nn.sigmoid(c)).astype(_BF)
    o_ref[...] = jnp.dot(a, w_ref[...], preferred_element_type=_F32) + b_ref[...]


def _adaln(c_all, w_bf, b):
    m, d = c_all.shape
    n = w_bf.shape[1]
    tn = 1536
    return pl.pallas_call(
        _adaln_kernel,
        grid=(n // tn,),
        in_specs=[pl.BlockSpec((m, d), lambda j: (0, 0)),
                  pl.BlockSpec((d, tn), lambda j: (0, j)),
                  pl.BlockSpec((1, tn), lambda j: (0, j))],
        out_specs=pl.BlockSpec((m, tn), lambda j: (0, j)),
        out_shape=jax.ShapeDtypeStruct((m, n), _F32),
        compiler_params=_params(("arbitrary",)),
        name="adaln",
    )(c_all, w_bf, b)


def _rope(x, cos, sin_signed, first_half):
    sw = jnp.where(first_half, pltpu.roll(x, 96, 1), pltpu.roll(x, 32, 1))
    return x * cos + sw * sin_signed


def _project(x_ref, sh_ref, sc_ref, g_ref, w_ref, cos_ref, sin_ref):
    d = x_ref.shape[-1]
    x = x_ref[...].reshape(-1, d)
    sh = sh_ref[...].reshape(-1, d)
    sc = sc_ref[...].reshape(-1, d)
    h = _rms(x, g_ref[...]) * (1.0 + sc) + sh
    proj = jnp.dot(h.astype(_BF), w_ref[...], preferred_element_type=_F32)
    cos = cos_ref[...].reshape(-1, LANES)
    sin = sin_ref[...].reshape(-1, LANES)
    lane = lax.broadcasted_iota(jnp.int32, (1, LANES), 1)
    first_half = (lane & (HEAD_DIM - 1)) < (HEAD_DIM // 2)

    def rot(col0, ncols):
        parts = [_rope(proj[:, col0 + j * LANES: col0 + (j + 1) * LANES], cos, sin, first_half)
                 for j in range(ncols // LANES)]
        return parts[0] if len(parts) == 1 else jnp.concatenate(parts, axis=1)

    o = 0
    qa = rot(o, A_QW) * SCALE; o += A_QW
    ka = rot(o, A_KVW); o += A_KVW
    va = proj[:, o:o + A_KVW]; o += A_KVW
    qb = rot(o, B_W) * SCALE; o += B_W
    kb = rot(o, B_W); o += B_W
    vb = proj[:, o:o + B_W]
    return qa, ka, va, qb, kb, vb


def _put(ref, val):
    ref[...] = val.astype(ref.dtype).reshape(ref.shape)


def _inproj_prompt_kernel(x_ref, sh_ref, sc_ref, g_ref, w_ref, cos_ref, sin_ref,
                          qa_ref, ka_ref, va_ref, q1_ref, k1_ref, v1_ref, q4_ref, k4_ref, v4_ref,
                          q16_ref, k16_ref, v16_ref, kat_ref, vat_ref, kbt_ref, vbt_ref,
                          sq_ref, sk_ref, sv_ref):
    qa, ka, va, qb, kb, vb = _project(x_ref, sh_ref, sc_ref, g_ref, w_ref, cos_ref, sin_ref)
    tm = qa.shape[0]
    _put(qa_ref, qa); _put(ka_ref, ka); _put(va_ref, va)
    _put(q1_ref, qb); _put(k1_ref, kb); _put(v1_ref, vb)
    ta = kat_ref.shape[1]
    _put(kat_ref, ka[tm - ta:]); _put(vat_ref, va[tm - ta:])
    _put(kbt_ref, kb); _put(vbt_ref, vb)
    for val, stage, r4, r16 in ((qb, sq_ref, q4_ref, q16_ref), (kb, sk_ref, k4_ref, k16_ref),
                                (vb, sv_ref, v4_ref, v16_ref)):
        for c in range(B_W // LANES):
            stage[c] = val[:, c * LANES:(c + 1) * LANES]
        for dil, out in ((4, r4), (16, r16)):
            n = tm // dil
            for r in range(dil):
                for c in range(B_W // LANES):
                    out[0, r, :, c * LANES:(c + 1) * LANES] = stage[c, pl.ds(r, n, stride=dil), :].astype(out.dtype)


def _inproj_prompt(x, sh, sc, g, w_bf, cos, sin, *, tm, tail_a, tail_b):
    b, s, d = x.shape
    nt = s // tm
    assert tail_b % tm == 0 and tail_a <= tm
    row = lambda c: pl.BlockSpec((1, tm, c), lambda bb, i: (bb, i, 0))
    res = lambda dil: pl.BlockSpec((1, dil, tm // dil, B_W), lambda bb, i: (bb, 0, i, 0))
    mod_spec = pl.BlockSpec((1, 1, d), lambda bb, i: (bb, 0, 0))
    tab_spec = pl.BlockSpec((tm, LANES), lambda bb, i: (i, 0))
    tail_a_spec = pl.BlockSpec((1, tail_a, A_KVW), lambda bb, i: (bb, 0, 0))
    tail_b_spec = pl.BlockSpec((1, tm, B_W), lambda bb, i: (bb, jnp.maximum(i - (nt - tail_b // tm), 0), 0))
    shp = lambda c, dt: jax.ShapeDtypeStruct((b, s, c), dt)
    rshp = lambda dil: jax.ShapeDtypeStruct((b, dil, s // dil, B_W), _BF)
    return pl.pallas_call(
        _inproj_prompt_kernel,
        grid=(b, nt),
        in_specs=[row(d), mod_spec, mod_spec, _resident((1, d)), _resident(w_bf.shape), tab_spec, tab_spec],
        out_specs=[row(A_QW), row(A_KVW), row(A_KVW)] + [res(1)] * 3 + [res(4)] * 3 + [res(16)] * 3
                  + [tail_a_spec] * 2 + [tail_b_spec] * 2,
        out_shape=[shp(A_QW, _BF), shp(A_KVW, _BF), shp(A_KVW, _BF)] + [rshp(1)] * 3 + [rshp(4)] * 3 + [rshp(16)] * 3
                  + [jax.ShapeDtypeStruct((b, tail_a, A_KVW), _F32)] * 2
                  + [jax.ShapeDtypeStruct((b, tail_b, B_W), _F32)] * 2,
        scratch_shapes=[pltpu.VMEM((B_W // LANES, tm, LANES), _F32)] * 3,
        compiler_params=_params(("parallel", "arbitrary")),
        name="in_proj_prompt",
    )(x, sh, sc, g, w_bf, cos, sin)


def _inproj_sample_kernel(x_ref, sh_ref, sc_ref, g_ref, w_ref, cos_ref, sin_ref,
                          qa_ref, ka_ref, va_ref, qb_ref, kb_ref, vb_ref):
    vals = _project(x_ref, sh_ref, sc_ref, g_ref, w_ref, cos_ref, sin_ref)
    for ref, val in zip((qa_ref, ka_ref, va_ref, qb_ref, kb_ref, vb_ref), vals):
        _put(ref, val)


def _inproj_sample(x, sh, sc, g, w_bf, cos, sin):
    t, n, d = x.shape
    row = lambda c: pl.BlockSpec((1, n, c), lambda j: (j, 0, 0))
    mod_spec = pl.BlockSpec((n, d), lambda j: (0, 0))
    tab_spec = pl.BlockSpec((1, 1, LANES), lambda j: (j, 0, 0))
    widths = (A_QW, A_KVW, A_KVW, B_W, B_W, B_W)
    return pl.pallas_call(
        _inproj_sample_kernel,
        grid=(t,),
        in_specs=[row(d), mod_spec, mod_spec, _resident((1, d)), _resident(w_bf.shape), tab_spec, tab_spec],
        out_specs=[row(c) for c in widths],
        out_shape=[jax.ShapeDtypeStruct((t, n, c), _F32) for c in widths],
        compiler_params=_params(("parallel",)),
        name="in_proj_sample",
    )(x, sh, sc, g, w_bf, cos, sin)


def _band_bias(max_dist, drop_prev):
    qi = lax.broadcasted_iota(jnp.int32, (BLOCK, 2 * BLOCK), 0)
    ki = lax.broadcasted_iota(jnp.int32, (BLOCK, 2 * BLOCK), 1)
    rel = qi + BLOCK - ki
    valid = (rel >= 0) & (rel <= max_dist)
    if drop_prev is not None:
        valid = valid & ((ki >= BLOCK) | jnp.logical_not(drop_prev))
    return jnp.where(valid, 0.0, NEG).astype(_F32)


def _kv_window(prev_ref, cur_ref, j, lead, cols):
    if j == 0:
        return jnp.concatenate([prev_ref[lead + (slice(None), cols)], cur_ref[lead + (slice(0, BLOCK), cols)]], axis=0)
    return cur_ref[lead + (slice((j - 1) * BLOCK, (j + 1) * BLOCK), cols)]


def _band_attn_kernel(q_ref, kp_ref, kc_ref, vp_ref, vc_ref, o_ref, lse_ref, *, max_dist, nq):
    first_step = pl.program_id(2) == 0
    bias = _band_bias(max_dist, None)
    bias0 = _band_bias(max_dist, first_step)
    lane = lax.broadcasted_iota(jnp.int32, (1, LANES), 1)
    lo = lane < HEAD_DIM
    hcol = lax.broadcasted_iota(jnp.int32, (1, B_HEADS), 1)
    zero = jnp.zeros((), _BF)
    for j in range(nq):
        rows = slice(j * BLOCK, (j + 1) * BLOCK)
        bj = bias0 if j == 0 else bias
        bj = jnp.concatenate([bj, bj], axis=0)
        lse_all = jnp.zeros((BLOCK, B_HEADS), _F32)
        for c in range(B_W // LANES):
            cs = slice(c * LANES, (c + 1) * LANES)
            q2 = q_ref[0, 0, rows, cs]
            k2 = _kv_window(kp_ref, kc_ref, j, (0, 0), cs)
            v2 = _kv_window(vp_ref, vc_ref, j, (0, 0), cs)
            qs = jnp.concatenate([jnp.where(lo, q2, zero), jnp.where(lo, zero, q2)], axis=0)
            s = lax.dot_general(qs, k2, (((1,), (1,)), ((), ())), preferred_element_type=_F32) + bj
            m = jnp.max(s, axis=-1, keepdims=True)
            p = jnp.exp(s - m)
            l = jnp.sum(p, axis=-1, keepdims=True)
            on = jnp.dot(p.astype(_BF), v2, preferred_element_type=_F32) * (1.0 / l)
            o_ref[0, 0, rows, cs] = jnp.where(lo, on[:BLOCK], on[BLOCK:])
            lse = m + jnp.log(l)
            lse_all = jnp.where(hcol == 2 * c, lse[:BLOCK], lse_all)
            lse_all = jnp.where(hcol == 2 * c + 1, lse[BLOCK:], lse_all)
        lse_ref[0, 0, rows, :] = lse_all


def _band_attn(q, k, v, dil):
    b, _, m, w = q.shape
    nq = min(Q_BLOCKS, m // BLOCK)
    tq = nq * BLOCK
    cur = pl.BlockSpec((1, 1, tq, w), lambda bb, r, i: (bb, r, i, 0))
    prev = pl.BlockSpec((1, 1, BLOCK, w), lambda bb, r, i: (bb, r, jnp.maximum(i * nq - 1, 0), 0))
    window, _ = [p for p in B_PATTERNS if p[1] == dil][0]
    return pl.pallas_call(
        functools.partial(_band_attn_kernel, max_dist=window // dil, nq=nq),
        grid=(b, dil, m // tq),
        in_specs=[cur, prev, cur, prev, cur],
        out_specs=[cur, pl.BlockSpec((1, 1, tq, B_HEADS), lambda bb, r, i: (bb, r, i, 0))],
        out_shape=[jax.ShapeDtypeStruct((b, dil, m, w), _F32),
                   jax.ShapeDtypeStruct((b, dil, m, B_HEADS), _F32)],
        compiler_params=_params(("parallel", "parallel", "arbitrary")),
        name=f"band_attn_d{dil}",
    )(q, k, k, v, v)


def _swa_kernel(sink_ref, q_ref, kp_ref, kc_ref, vp_ref, vc_ref, o_ref, *, nq):
    first_step = pl.program_id(1) == 0
    bias = _band_bias(A_WINDOW - 1, None)
    bias0 = _band_bias(A_WINDOW - 1, first_step)
    lane = lax.broadcasted_iota(jnp.int32, (1, LANES), 1)
    lo = lane < HEAD_DIM
    for j in range(nq):
        rows = slice(j * BLOCK, (j + 1) * BLOCK)
        bj = bias0 if j == 0 else bias
        bj = jnp.concatenate([bj, bj], axis=0)
        k2 = _kv_window(kp_ref, kc_ref, j, (0,), slice(None))
        v2 = _kv_window(vp_ref, vc_ref, j, (0,), slice(None))
        for c in range(A_QW // LANES):
            g = (2 * c) // A_GROUP
            keep = lo if g == 0 else jnp.logical_not(lo)
            qc = q_ref[0, rows, c * LANES:(c + 1) * LANES].astype(_F32)
            qr = pltpu.roll(qc, HEAD_DIM, 1)
            tiles = [jnp.where(keep, qc if half == g else qr, 0.0).astype(_BF) for half in range(2)]
            qs = jnp.concatenate(tiles, axis=0)
            s = lax.dot_general(qs, k2, (((1,), (1,)), ((), ())), preferred_element_type=_F32) + bj
            ps, rls = [], []
            for half in range(2):
                sink = sink_ref[2 * c + half]
                sh = s[half * BLOCK:(half + 1) * BLOCK]
                m = jnp.maximum(jnp.max(sh, axis=-1, keepdims=True), sink)
                p = jnp.exp(sh - m)
                ps.append(p)
                rls.append(1.0 / (jnp.sum(p, axis=-1, keepdims=True) + jnp.exp(sink - m)))
            p = jnp.concatenate(ps, axis=0).astype(_BF)
            pv = jnp.dot(p, v2, preferred_element_type=_F32)
            halves = [pv[half * BLOCK:(half + 1) * BLOCK] * rls[half] for half in range(2)]
            halves = [t if half == g else pltpu.roll(t, HEAD_DIM, 1) for half, t in enumerate(halves)]
            o_ref[0, rows, c * LANES:(c + 1) * LANES] = jnp.where(lo, halves[0], halves[1])


def _swa_prompt(q, k, v, sinks):
    b, s, _ = q.shape
    nq = Q_BLOCKS
    tq = nq * BLOCK
    qspec = pl.BlockSpec((1, tq, A_QW), lambda bb, i: (bb, i, 0))
    cur = pl.BlockSpec((1, tq, A_KVW), lambda bb, i: (bb, i, 0))
    prev = pl.BlockSpec((1, BLOCK, A_KVW), lambda bb, i: (bb, jnp.maximum(i * nq - 1, 0), 0))
    return pl.pallas_call(
        functools.partial(_swa_kernel, nq=nq),
        grid=(b, s // tq),
        in_specs=[pl.BlockSpec(memory_space=pltpu.SMEM), qspec, prev, cur, prev, cur],
        out_specs=qspec,
        out_shape=jax.ShapeDtypeStruct((b, s, A_QW), _F32),
        compiler_params=_params(("parallel", "arbitrary")),
        name="swa_prompt",
    )(sinks, q, k, k, v, v)


def _roll_in(x_ref, new8_ref, o_ref, t_new):
    rows, width = x_ref.shape[-2], x_ref.shape[-1]
    new8 = new8_ref[0, 0]
    tail = jnp.concatenate([jnp.zeros((LANES - new8.shape[0], new8.shape[1]), _F32), new8], axis=0).T[:rows]
    lane = lax.broadcasted_iota(jnp.int32, (1, LANES), 1)
    keep = lane < LANES - t_new
    nxt = pltpu.roll(x_ref[0, :, 0:LANES], LANES - t_new, 1)
    for c in range(width // LANES):
        cur = nxt
        if c + 1 < width // LANES:
            nxt = pltpu.roll(x_ref[0, :, (c + 1) * LANES:(c + 2) * LANES], LANES - t_new, 1)
        else:
            nxt = tail
        o_ref[0, :, c * LANES:(c + 1) * LANES] = jnp.where(keep, cur, nxt)


def _sample_half(g, refs, *, t_new):
    (sink_ref, qa_ref, kan_ref, van_ref, cak_ref, cav_ref, ak8_ref, av8_ref,
     qb_ref, kb8_ref, vb8_ref, cbk_ref, cbv_ref,
     oa_ref, ob_ref, nak_ref, nav_ref, nbk_ref, nbv_ref) = refs
    wa, wb = cak_ref.shape[-1], cbk_ref.shape[-1]

    rows_a = t_new * A_GROUP
    ra = lax.broadcasted_iota(jnp.int32, (rows_a, 1), 0)
    qi_a = _div(ra, A_GROUP)
    rel_a = wa + qi_a - lax.broadcasted_iota(jnp.int32, (1, wa), 1)
    valid_a = (rel_a >= 0) & (rel_a < A_WINDOW)
    q = qa_ref[0, 0]
    s = jnp.dot(q.astype(_BF), cak_ref[0].astype(_BF), preferred_element_type=_F32)
    s = jnp.where(valid_a, s, NEG)
    s_new = [jnp.where(qi_a >= j, jnp.sum(q * kan_ref[0, 0, j:j + 1, :], axis=-1, keepdims=True), NEG)
             for j in range(t_new)]
    sink = jnp.zeros((rows_a, 1), _F32)
    for u in range(A_GROUP):
        sink = jnp.where((ra & (A_GROUP - 1)) == u, sink_ref[g * A_GROUP + u], sink)
    m = functools.reduce(jnp.maximum, [jnp.max(s, axis=-1, keepdims=True), sink] + s_new)
    p = jnp.exp(s - m)
    p_new = [jnp.exp(z - m) for z in s_new]
    l = functools.reduce(jnp.add, [jnp.sum(p, axis=-1, keepdims=True), jnp.exp(sink - m)] + p_new)
    o = lax.dot_general(p.astype(_BF), cav_ref[0].astype(_BF), (((1,), (1,)), ((), ())),
                        preferred_element_type=_F32)
    for j in range(t_new):
        o = o + p_new[j] * van_ref[0, 0, j:j + 1, :]
    oa_ref[0, 0] = o * (1.0 / l)
    _roll_in(cak_ref, ak8_ref, nak_ref, t_new)
    _roll_in(cav_ref, av8_ref, nav_ref, t_new)

    width_b = cbk_ref.shape[1]
    r0 = kb8_ref.shape[2] - t_new
    rows_b = t_new * SUBLANES
    sub = lax.broadcasted_iota(jnp.int32, (SUBLANES, width_b), 0)
    own = _div(lax.broadcasted_iota(jnp.int32, (SUBLANES, width_b), 1), HEAD_DIM) == sub
    qbd = jnp.concatenate([jnp.where(own, qb_ref[0, 0, i:i + 1, :], 0.0) for i in range(t_new)], axis=0)
    s_all = jnp.dot(qbd.astype(_BF), cbk_ref[0].astype(_BF), preferred_element_type=_F32)
    qi = _div(lax.broadcasted_iota(jnp.int32, (rows_b, 1), 0), SUBLANES)
    s_new = [jnp.sum(qbd * kb8_ref[0, 0, r0 + j:r0 + j + 1, :], axis=-1, keepdims=True) for j in range(t_new)]
    pats = []
    for window, dil in B_PATTERNS:
        lo_lane = max(wb - (-(-window // LANES) * LANES), 0)
        wl = lo_lane + lax.broadcasted_iota(jnp.int32, (1, wb - lo_lane), 1)
        rel = wb + qi - wl
        valid = (rel <= window) & _multiple(rel, dil)
        s = jnp.where(valid, s_all[:, lo_lane:], NEG)
        sn = [jnp.where((qi - j >= 0) & _multiple(qi - j, dil), s_new[j], NEG) for j in range(t_new)]
        m = functools.reduce(jnp.maximum, [jnp.max(s, axis=-1, keepdims=True)] + sn)
        p = jnp.exp(s - m)
        pn = [jnp.exp(z - m) for z in sn]
        l = functools.reduce(jnp.add, [jnp.sum(p, axis=-1, keepdims=True)] + pn)
        pats.append((lo_lane, p, pn, l, m + jnp.log(l)))
    mx = functools.reduce(jnp.maximum, [t[4] for t in pats])
    es = [jnp.exp(t[4] - mx) for t in pats]
    den = functools.reduce(jnp.add, es)
    coef = [e / (den * t[3]) for e, t in zip(es, pats)]
    starts = sorted({t[0] for t in pats} | {wb})
    blocks = []
    for a, b in zip(starts[:-1], starts[1:]):
        acc = None
        for c, (lo_lane, p, _, _, _) in zip(coef, pats):
            if lo_lane <= a:
                term = c * p[:, a - lo_lane:b - lo_lane]
                acc = term if acc is None else acc + term
        blocks.append(acc)
    lead = starts[0]
    p_comb = jnp.concatenate(blocks, axis=1) if len(blocks) > 1 else blocks[0]
    o_full = lax.dot_general(p_comb.astype(_BF), cbv_ref[0, :, lead:].astype(_BF), (((1,), (1,)), ((), ())),
                             preferred_element_type=_F32)
    for j in range(t_new):
        pj = functools.reduce(jnp.add, [c * t[2][j] for c, t in zip(coef, pats)])
        o_full = o_full + pj * vb8_ref[0, 0, r0 + j:r0 + j + 1, :]
    for i in range(t_new):
        blk = jnp.where(own, o_full[i * SUBLANES:(i + 1) * SUBLANES], 0.0)
        ob_ref[0, 0, i:i + 1, :] = jnp.sum(blk, axis=0, keepdims=True)
    _roll_in(cbk_ref, kb8_ref, nbk_ref, t_new)
    _roll_in(cbv_ref, vb8_ref, nbv_ref, t_new)


def _ffn_chunk(h, wg_ref, wu_ref, wd_ref, cs):
    gt = jnp.dot(h, wg_ref[:, cs], preferred_element_type=_F32)
    up = jnp.dot(h, wu_ref[:, cs], preferred_element_type=_F32)
    act = (gt * jax.nn.sigmoid(gt) * up).astype(_BF)
    return jnp.dot(act, wd_ref[cs, :], preferred_element_type=_F32)


def _attn_residual(x, oa, ob, gta, shf, scf, goa_ref, gob_ref, wo_ref, gf_ref):
    merged = jnp.concatenate([_rms(oa, goa_ref[...]), _rms(ob, gob_ref[...])], axis=1)
    x1 = x + gta * jnp.dot(merged.astype(_BF), wo_ref[...], preferred_element_type=_F32)
    return x1, (_rms(x1, gf_ref[...]) * (1.0 + scf) + shf).astype(_BF)


_TWO = lambda r: r[...].reshape(-1, r.shape[-1])
_FF_CUTS = (0, 384, 1152, 2048, 2816)
_PHASES = len(_FF_CUTS) - 1


def _ffn_and_sample_kernel(*refs, t_new):
    (x_ref, oa_ref, o1_ref, o4_ref, o16_ref, l1_ref, l2_ref, l3_ref, e_ref,
     gta_ref, shf_ref, scf_ref, gtf_ref, goa_ref, gob_ref, wo_ref, gf_ref, wg_ref, wu_ref, wd_ref, gfin_ref) = refs[:21]
    sample_in = refs[21:34]
    y_ref = refs[34]
    sample_out = refs[35:41]
    s4_ref, s16_ref, x1_ref, h_ref, acc_ref = refs[41:]
    step = pl.program_id(0)
    half = step % 2
    phase = step % _PHASES
    tm = x_ref.shape[1]

    def mix_and_project():
        for dil, src, stage in ((4, o4_ref, s4_ref), (16, o16_ref, s16_ref)):
            for r in range(dil):
                for c in range(B_W // LANES):
                    stage[c, pl.ds(r, tm // dil, stride=dil), :] = src[0, r, :, c * LANES:(c + 1) * LANES]
        gather = lambda stage: jnp.concatenate([stage[c] for c in range(B_W // LANES)], axis=1)
        outs = [_TWO(o1_ref), gather(s4_ref), gather(s16_ref)]
        lses = [_TWO(l1_ref), _TWO(l2_ref), _TWO(l3_ref)]
        mx = jnp.maximum(jnp.maximum(lses[0], lses[1]), lses[2])
        es = [jnp.exp(z - mx) for z in lses]
        den = es[0] + es[1] + es[2]
        ob = jnp.zeros((tm, B_W), _F32)
        for e, o in zip(es, outs):
            alpha = e / den
            hi = alpha.astype(_BF)
            lo = (alpha - hi.astype(_F32)).astype(_BF)
            wide = (jnp.dot(hi, e_ref[...], preferred_element_type=_F32)
                    + jnp.dot(lo, e_ref[...], preferred_element_type=_F32))
            ob = ob + wide * o
        x1, h = _attn_residual(_TWO(x_ref), _TWO(oa_ref), ob, _TWO(gta_ref), _TWO(shf_ref), _TWO(scf_ref),
                               goa_ref, gob_ref, wo_ref, gf_ref)
        x1_ref[...] = x1
        h_ref[...] = h

    for k in range(_PHASES):
        @pl.when(phase == k)
        def _(k=k):
            _sample_half(half, tuple(sample_in) + tuple(sample_out), t_new=t_new)
            if k == 0:
                mix_and_project()
            part = _ffn_chunk(h_ref[...], wg_ref, wu_ref, wd_ref, slice(_FF_CUTS[k], _FF_CUTS[k + 1]))
            if k == 0:
                acc_ref[...] = part
            elif k < _PHASES - 1:
                acc_ref[...] += part
            else:
                x2 = x1_ref[...] + _TWO(gtf_ref) * (acc_ref[...] + part)
                y_ref[...] = _rms(x2, gfin_ref[...]).reshape(y_ref.shape)


def _ffn_and_sample(x, oa, obs, lses, mods, goa, gob, wo, gf, wg, wu, wd, gfin, sinks, sample_ins, *, tm):
    b, s, d = x.shape
    n = sample_ins[0].shape[0]
    tiles = s // tm
    steps = 2 * n
    assert steps == b * tiles * _PHASES and _PHASES % 2 == 0 and wg.shape[1] == _FF_CUTS[-1]
    t_new = sample_ins[0].shape[2] // A_GROUP

    def tile(j):
        t = j // _PHASES
        return t // tiles, t % tiles

    row = lambda c: pl.BlockSpec((1, tm, c), lambda j: tile(j) + (0,))
    res = lambda dil: pl.BlockSpec((1, dil, tm // dil, B_W), lambda j: (tile(j)[0], 0, tile(j)[1], 0))
    mod_spec = pl.BlockSpec((1, 1, d), lambda j: (tile(j)[0], 0, 0))
    expand = (jnp.arange(B_W)[None, :] // HEAD_DIM == jnp.arange(B_HEADS)[:, None]).astype(_BF)
    ffn_ins = [x, oa] + list(obs) + list(lses) + [expand] + list(mods) + [goa, gob, wo, gf, wg, wu, wd, gfin]
    ffn_specs = ([row(d), row(A_QW)] + [res(o.shape[1]) for o in obs] + [row(B_HEADS)] * len(lses)
                 + [_resident(expand.shape)] + [mod_spec] * 4
                 + [_resident(a.shape) for a in (goa, gob, wo, gf, wg, wu, wd, gfin)])
    half_blk = lambda a: pl.BlockSpec((1, 1) + a.shape[2:], lambda j: (j // 2, j % 2) + (0,) * (a.ndim - 2))
    (qa_g, ka_n, va_n, ca_k, ca_v, ak8, av8, qb_h, kb8, vb8, cb_k, cb_v) = sample_ins
    sample_outs = [jax.ShapeDtypeStruct(a.shape, _F32) for a in (qa_g, qb_h, ca_k, ca_v, cb_k, cb_v)]
    cache_blk = lambda a: pl.BlockSpec((1, a.shape[1] // 2, a.shape[2]), lambda j: (j // 2, j % 2, 0))
    s_specs = [half_blk(qa_g), half_blk(ka_n), half_blk(va_n), cache_blk(ca_k), cache_blk(ca_v), half_blk(ak8),
               half_blk(av8), half_blk(qb_h), half_blk(kb8), half_blk(vb8), cache_blk(cb_k), cache_blk(cb_v)]
    o_specs = [half_blk(qa_g), half_blk(qb_h), cache_blk(ca_k), cache_blk(ca_v), cache_blk(cb_k), cache_blk(cb_v)]
    stage = pltpu.VMEM((B_W // LANES, tm, LANES), _F32)
    outs = pl.pallas_call(
        functools.partial(_ffn_and_sample_kernel, t_new=t_new),
        grid=(steps,),
        in_specs=ffn_specs + [pl.BlockSpec(memory_space=pltpu.SMEM)] + s_specs,
        out_specs=[row(d)] + o_specs,
        out_shape=[jax.ShapeDtypeStruct((b, s, d), _F32)] + sample_outs,
        scratch_shapes=[stage, stage, pltpu.VMEM((tm, d), _F32), pltpu.VMEM((tm, d), _BF), pltpu.VMEM((tm, d), _F32)],
        compiler_params=_params(("arbitrary",)),
        name="ffn_and_sample",
    )(*ffn_ins, sinks, *sample_ins)
    return outs


def _merge_ffn_sample_kernel(x_ref, oa_ref, ob_ref, gta_ref, shf_ref, scf_ref, gtf_ref,
                             goa_ref, gob_ref, wo_ref, gf_ref, wg_ref, wu_ref, wd_ref, gfin_ref, y_ref):
    x1, h = _attn_residual(_TWO(x_ref), _TWO(oa_ref), _TWO(ob_ref), gta_ref[...], shf_ref[...], scf_ref[...],
                           goa_ref, gob_ref, wo_ref, gf_ref)
    acc = jnp.zeros_like(x1)
    for k in range(_PHASES):
        acc = acc + _ffn_chunk(h, wg_ref, wu_ref, wd_ref, slice(_FF_CUTS[k], _FF_CUTS[k + 1]))
    y_ref[...] = _rms(x1 + gtf_ref[...] * acc, gfin_ref[...]).reshape(y_ref.shape)


def _merge_ffn_sample(x, oa, ob, mods, goa, gob, wo, gf, wg, wu, wd, gfin):
    t, n, d = x.shape
    row = lambda c: pl.BlockSpec((1, n, c), lambda j: (j, 0, 0))
    mod_spec = pl.BlockSpec((n, d), lambda j: (0, 0))
    weights = (goa, gob, wo, gf, wg, wu, wd, gfin)
    return pl.pallas_call(
        _merge_ffn_sample_kernel,
        grid=(t,),
        in_specs=[row(d), row(A_QW), row(B_W)] + [mod_spec] * 4 + [_resident(a.shape) for a in weights],
        out_specs=row(d),
        out_shape=jax.ShapeDtypeStruct((t, n, d), _F32),
        compiler_params=_params(("parallel",)),
        name="merge_ffn_sample",
    )(x, oa, ob, *mods, *weights)


def _rope_tables(pos):
    half = HEAD_DIM // 2
    inv = jnp.exp(-math.log(ROPE_THETA) * jnp.arange(half, dtype=_F32) * (2.0 / HEAD_DIM))
    ang = pos.astype(_F32)[:, None] * inv[None, :]
    cos, sin = jnp.cos(ang), jnp.sin(ang)
    return jnp.tile(cos, (1, LANES // half)), jnp.tile(jnp.concatenate([-sin, sin], axis=1), (1, LANES // HEAD_DIM))


def _window_on_lanes(cache):
    _, n, w, h, dh = cache.shape
    return cache.transpose(0, 1, 3, 4, 2).reshape(n, h * dh, w)


def _window_off_lanes(t, h):
    n, _, w = t.shape
    return t.reshape(1, n, h, HEAD_DIM, w).transpose(0, 1, 4, 2, 3)


def _halves_last_rows(new_t, pad_to):
    t, n, c = new_t.shape
    v = new_t.reshape(t, n, 2, c // 2).transpose(1, 2, 0, 3)
    return jnp.pad(v, ((0, 0), (0, 0), (SUBLANES - t, 0), (0, pad_to - c // 2)))


def kernel(x_prompt, x_sample, c_prompt, c_sample, cache_a_k, cache_a_v, cache_b_k, cache_b_v, w_ada, b_ada, g_attn, w_in, sinks, g_out_a, g_out_b, w_o, g_ffn, w_gate, w_up, w_down, g_final):
    nb, s, d = x_prompt.shape
    ns, t_new, _ = x_sample.shape
    assert w_ada.shape[0] == 1, "single trunk layer"
    l = 0
    bf = lambda w: w.astype(_BF)
    row = lambda g: g.reshape(1, -1)

    c_all = jnp.concatenate([c_prompt, c_sample], axis=0)
    c_all = jnp.pad(c_all, ((0, (-c_all.shape[0]) % SUBLANES), (0, 0)))
    mod = _adaln(c_all, bf(w_ada[l]), b_ada[l].reshape(1, -1))
    mod_p = [m.reshape(nb, 1, d) for m in jnp.split(mod[:nb], 6, axis=-1)]
    mod_s = jnp.split(mod[nb:nb + ns], 6, axis=-1)

    cos_p, sin_p = _rope_tables(jnp.arange(s, dtype=jnp.int32))
    cos_s, sin_s = _rope_tables(PAST_LEN + jnp.arange(t_new, dtype=jnp.int32))
    w_in_bf = bf(w_in[l])
    ffn_w = (row(g_out_a[l]), row(g_out_b[l]), bf(w_o[l]), row(g_ffn[l]), bf(w_gate[l]), bf(w_up[l]), bf(w_down[l]),
             row(g_final))

    wa_p, wb_p = min(A_WINDOW, s), min(B_WINDOW, s)
    (qa, ka, va, q1, k1, v1, q4, k4, v4, q16, k16, v16, ka_t, va_t, kb_t, vb_t) = _inproj_prompt(
        x_prompt, mod_p[0], mod_p[1], row(g_attn[l]), w_in_bf, cos_p, sin_p, tm=512, tail_a=wa_p, tail_b=wb_p)
    xs_t = x_sample.transpose(1, 0, 2)
    qa_s, ka_s, va_s, qb_s, kb_s, vb_s = _inproj_sample(
        xs_t, mod_s[0], mod_s[1], row(g_attn[l]), w_in_bf, cos_s.reshape(t_new, 1, LANES),
        sin_s.reshape(t_new, 1, LANES))

    oa = _swa_prompt(qa, ka, va, sinks[l])
    obs, lses = [], []
    for (q, k, v), (_, dil) in zip(((q1, k1, v1), (q4, k4, v4), (q16, k16, v16)), B_PATTERNS):
        o, lse = _band_attn(q, k, v, dil)
        obs.append(o)
        lses.append(lse.transpose(0, 2, 1, 3).reshape(nb, s, B_HEADS))

    qa_g = (qa_s.reshape(t_new, ns, A_KV_HEADS, A_GROUP, HEAD_DIM).transpose(1, 2, 0, 3, 4)
            .reshape(ns, A_KV_HEADS, t_new * A_GROUP, HEAD_DIM))
    kv_g = lambda t: t.reshape(t_new, ns, A_KV_HEADS, HEAD_DIM).transpose(1, 2, 0, 3)
    qb_h = qb_s.reshape(t_new, ns, 2, B_W // 2).transpose(1, 2, 0, 3)
    sample_ins = (qa_g, kv_g(ka_s), kv_g(va_s), _window_on_lanes(cache_a_k), _window_on_lanes(cache_a_v),
                  _halves_last_rows(ka_s, LANES), _halves_last_rows(va_s, LANES), qb_h,
                  _halves_last_rows(kb_s, B_W // 2), _halves_last_rows(vb_s, B_W // 2),
                  _window_on_lanes(cache_b_k), _window_on_lanes(cache_b_v))
    y_prompt, oa_g, ob_h, na_k, na_v, nb_k, nb_v = _ffn_and_sample(
        x_prompt, oa, obs, lses, mod_p[2:6], *ffn_w, sinks[l], sample_ins, tm=256)
    pa_k = ka_t.reshape(1, nb, wa_p, A_KV_HEADS, HEAD_DIM)
    pa_v = va_t.reshape(1, nb, wa_p, A_KV_HEADS, HEAD_DIM)
    pb_k = kb_t.reshape(1, nb, wb_p, B_HEADS, HEAD_DIM)
    pb_v = vb_t.reshape(1, nb, wb_p, B_HEADS, HEAD_DIM)

    oa_t = (oa_g.reshape(ns, A_KV_HEADS, t_new, A_GROUP, HEAD_DIM).transpose(2, 0, 1, 3, 4)
            .reshape(t_new, ns, A_QW))
    ob_t = ob_h.transpose(2, 0, 1, 3).reshape(t_new, ns, B_W)
    y_s = _merge_ffn_sample(xs_t, oa_t, ob_t, mod_s[2:6], *ffn_w)
    y_sample = y_s.transpose(1, 0, 2)
    sa_k, sa_v = _window_off_lanes(na_k, A_KV_HEADS), _window_off_lanes(na_v, A_KV_HEADS)
    sb_k, sb_v = _window_off_lanes(nb_k, B_HEADS), _window_off_lanes(nb_v, B_HEADS)

    return (y_prompt, y_sample, pa_k, pa_v, pb_k, pb_v, sa_k, sa_v, sb_k, sb_v)
```

```python
import functools
import math

import jax
import jax.numpy as jnp
from jax import lax
from jax.experimental import pallas as pl
from jax.experimental.pallas import tpu as pltpu

HEAD_DIM = 64
A_Q_HEADS = 8
A_KV_HEADS = 2
A_GROUP = A_Q_HEADS // A_KV_HEADS
B_HEADS = 8
A_WINDOW = 128
BLOCK = 128
B_PATTERNS = ((128, 1), (512, 4), (2048, 16))
B_WINDOW = 2048
PAST_LEN = 8192
ROPE_THETA = 10000.0
EPS = 1e-6
A_QW = A_Q_HEADS * HEAD_DIM
A_KVW = A_KV_HEADS * HEAD_DIM
B_W = B_HEADS * HEAD_DIM
SCALE = HEAD_DIM ** -0.5
LANES = 128
SUBLANES = 8
NEG = -1e30
VMEM_LIMIT = 56 * 1024 * 1024
Q_BLOCKS = 4

_BF = jnp.bfloat16
_F32 = jnp.float32


def _params(sem, vmem=VMEM_LIMIT):
    return pltpu.CompilerParams(dimension_semantics=sem, vmem_limit_bytes=vmem)


def _resident(shape):
    nd = len(shape)
    return pl.BlockSpec(shape, lambda *_: (0,) * nd, pipeline_mode=pl.Buffered(1))


def _rms(x, g):
    return x * lax.rsqrt(jnp.mean(x * x, axis=-1, keepdims=True) + EPS) * g


def _log2(n):
    assert n > 0 and n & (n - 1) == 0, "power of two expected"
    return n.bit_length() - 1


def _div(x, n):
    return x >> _log2(n)


def _multiple(x, n):
    _log2(n)
    return (x & (n - 1)) == 0


def _adaln_kernel(c_ref, w_ref, b_ref, o_ref):
    c = c_ref[...]
    a = (c * jax.nn.sigmoid(c)).astype(_BF)
    o_ref[...] = jnp.dot(a, w_ref[...], preferred_element_type=_F32) + b_ref[...]


def _adaln(c_all, w_bf, b):
    m, d = c_all.shape
    n = w_bf.shape[1]
    tn = 1536
    return pl.pallas_call(
        _adaln_kernel,
        grid=(n // tn,),
        in_specs=[pl.BlockSpec((m, d), lambda j: (0, 0)),
                  pl.BlockSpec((d, tn), lambda j: (0, j)),
                  pl.BlockSpec((1, tn), lambda j: (0, j))],
        out_specs=pl.BlockSpec((m, tn), lambda j: (0, j)),
        out_shape=jax.ShapeDtypeStruct((m, n), _F32),
        compiler_params=_params(("arbitrary",)),
        name="adaln",
    )(c_all, w_bf, b)


def _rope(x, cos, sin_signed, first_half):
    sw = jnp.where(first_half, pltpu.roll(x, 96, 1), pltpu.roll(x, 32, 1))
    return x * cos + sw * sin_signed


def _project(x_ref, sh_ref, sc_ref, g_ref, w_ref, cos_ref, sin_ref):
    d = x_ref.shape[-1]
    x = x_ref[...].reshape(-1, d)
    sh = sh_ref[...].reshape(-1, d)
    sc = sc_ref[...].reshape(-1, d)
    h = _rms(x, g_ref[...]) * (1.0 + sc) + sh
    proj = jnp.dot(h.astype(_BF), w_ref[...], preferred_element_type=_F32)
    cos = cos_ref[...].reshape(-1, LANES)
    sin = sin_ref[...].reshape(-1, LANES)
    lane = lax.broadcasted_iota(jnp.int32, (1, LANES), 1)
    first_half = (lane & (HEAD_DIM - 1)) < (HEAD_DIM // 2)

    def rot(col0, ncols):
        parts = [_rope(proj[:, col0 + j * LANES: col0 + (j + 1) * LANES], cos, sin, first_half)
                 for j in range(ncols // LANES)]
        return parts[0] if len(parts) == 1 else jnp.concatenate(parts, axis=1)

    o = 0
    qa = rot(o, A_QW) * SCALE; o += A_QW
    ka = rot(o, A_KVW); o += A_KVW
    va = proj[:, o:o + A_KVW]; o += A_KVW
    qb = rot(o, B_W) * SCALE; o += B_W
    kb = rot(o, B_W); o += B_W
    vb = proj[:, o:o + B_W]
    return qa, ka, va, qb, kb, vb


def _put(ref, val):
    ref[...] = val.astype(ref.dtype).reshape(ref.shape)


def _inproj_prompt_kernel(x_ref, sh_ref, sc_ref, g_ref, w_ref, cos_ref, sin_ref,
                          qa_ref, ka_ref, va_ref, q1_ref, k1_ref, v1_ref, q4_ref, k4_ref, v4_ref,
                          q16_ref, k16_ref, v16_ref, kat_ref, vat_ref, kbt_ref, vbt_ref,
                          sq_ref, sk_ref, sv_ref):
    qa, ka, va, qb, kb, vb = _project(x_ref, sh_ref, sc_ref, g_ref, w_ref, cos_ref, sin_ref)
    tm = qa.shape[0]
    _put(qa_ref, qa); _put(ka_ref, ka); _put(va_ref, va)
    _put(q1_ref, qb); _put(k1_ref, kb); _put(v1_ref, vb)
    ta = kat_ref.shape[1]
    _put(kat_ref, ka[tm - ta:]); _put(vat_ref, va[tm - ta:])
    _put(kbt_ref, kb); _put(vbt_ref, vb)
    for val, stage, r4, r16 in ((qb, sq_ref, q4_ref, q16_ref), (kb, sk_ref, k4_ref, k16_ref),
                                (vb, sv_ref, v4_ref, v16_ref)):
        for c in range(B_W // LANES):
            stage[c] = val[:, c * LANES:(c + 1) * LANES]
        for dil, out in ((4, r4), (16, r16)):
            n = tm // dil
            for r in range(dil):
                for c in range(B_W // LANES):
                    out[0, r, :, c * LANES:(c + 1) * LANES] = stage[c, pl.ds(r, n, stride=dil), :].astype(out.dtype)


def _inproj_prompt(x, sh, sc, g, w_bf, cos, sin, *, tm, tail_a, tail_b):
    b, s, d = x.shape
    nt = s // tm
    assert tail_b % tm == 0 and tail_a <= tm
    row = lambda c: pl.BlockSpec((1, tm, c), lambda bb, i: (bb, i, 0))
    res = lambda dil: pl.BlockSpec((1, dil, tm // dil, B_W), lambda bb, i: (bb, 0, i, 0))
    mod_spec = pl.BlockSpec((1, 1, d), lambda bb, i: (bb, 0, 0))
    tab_spec = pl.BlockSpec((tm, LANES), lambda bb, i: (i, 0))
    tail_a_spec = pl.BlockSpec((1, tail_a, A_KVW), lambda bb, i: (bb, 0, 0))
    tail_b_spec = pl.BlockSpec((1, tm, B_W), lambda bb, i: (bb, jnp.maximum(i - (nt - tail_b // tm), 0), 0))
    shp = lambda c, dt: jax.ShapeDtypeStruct((b, s, c), dt)
    rshp = lambda dil: jax.ShapeDtypeStruct((b, dil, s // dil, B_W), _BF)
    return pl.pallas_call(
        _inproj_prompt_kernel,
        grid=(b, nt),
        in_specs=[row(d), mod_spec, mod_spec, _resident((1, d)), _resident(w_bf.shape), tab_spec, tab_spec],
        out_specs=[row(A_QW), row(A_KVW), row(A_KVW)] + [res(1)] * 3 + [res(4)] * 3 + [res(16)] * 3
                  + [tail_a_spec] * 2 + [tail_b_spec] * 2,
        out_shape=[shp(A_QW, _BF), shp(A_KVW, _BF), shp(A_KVW, _BF)] + [rshp(1)] * 3 + [rshp(4)] * 3 + [rshp(16)] * 3
                  + [jax.ShapeDtypeStruct((b, tail_a, A_KVW), _F32)] * 2
                  + [jax.ShapeDtypeStruct((b, tail_b, B_W), _F32)] * 2,
        scratch_shapes=[pltpu.VMEM((B_W // LANES, tm, LANES), _F32)] * 3,
        compiler_params=_params(("parallel", "arbitrary")),
        name="in_proj_prompt",
    )(x, sh, sc, g, w_bf, cos, sin)


def _inproj_sample_kernel(x_ref, sh_ref, sc_ref, g_ref, w_ref, cos_ref, sin_ref,
                          qa_ref, ka_ref, va_ref, qb_ref, kb_ref, vb_ref):
    vals = _project(x_ref, sh_ref, sc_ref, g_ref, w_ref, cos_ref, sin_ref)
    for ref, val in zip((qa_ref, ka_ref, va_ref, qb_ref, kb_ref, vb_ref), vals):
        _put(ref, val)


def _inproj_sample(x, sh, sc, g, w_bf, cos, sin):
    t, n, d = x.shape
    row = lambda c: pl.BlockSpec((1, n, c), lambda j: (j, 0, 0))
    mod_spec = pl.BlockSpec((n, d), lambda j: (0, 0))
    tab_spec = pl.BlockSpec((1, 1, LANES), lambda j: (j, 0, 0))
    widths = (A_QW, A_KVW, A_KVW, B_W, B_W, B_W)
    return pl.pallas_call(
        _inproj_sample_kernel,
        grid=(t,),
        in_specs=[row(d), mod_spec, mod_spec, _resident((1, d)), _resident(w_bf.shape), tab_spec, tab_spec],
        out_specs=[row(c) for c in widths],
        out_shape=[jax.ShapeDtypeStruct((t, n, c), _F32) for c in widths],
        compiler_params=_params(("parallel",)),
        name="in_proj_sample",
    )(x, sh, sc, g, w_bf, cos, sin)


def _band_bias(max_dist, drop_prev):
    qi = lax.broadcasted_iota(jnp.int32, (BLOCK, 2 * BLOCK), 0)
    ki = lax.broadcasted_iota(jnp.int32, (BLOCK, 2 * BLOCK), 1)
    rel = qi + BLOCK - ki
    valid = (rel >= 0) & (rel <= max_dist)
    if drop_prev is not None:
        valid = valid & ((ki >= BLOCK) | jnp.logical_not(drop_prev))
    return jnp.where(valid, 0.0, NEG).astype(_F32)


def _kv_window(prev_ref, cur_ref, j, lead, cols):
    if j == 0:
        return jnp.concatenate([prev_ref[lead + (slice(None), cols)], cur_ref[lead + (slice(0, BLOCK), cols)]], axis=0)
    return cur_ref[lead + (slice((j - 1) * BLOCK, (j + 1) * BLOCK), cols)]


def _band_attn_kernel(q_ref, kp_ref, kc_ref, vp_ref, vc_ref, o_ref, lse_ref, *, max_dist, nq):
    first_step = pl.program_id(2) == 0
    bias = _band_bias(max_dist, None)
    bias0 = _band_bias(max_dist, first_step)
    lane = lax.broadcasted_iota(jnp.int32, (1, LANES), 1)
    lo = lane < HEAD_DIM
    hcol = lax.broadcasted_iota(jnp.int32, (1, B_HEADS), 1)
    zero = jnp.zeros((), _BF)
    for j in range(nq):
        rows = slice(j * BLOCK, (j + 1) * BLOCK)
        bj = bias0 if j == 0 else bias
        bj = jnp.concatenate([bj, bj], axis=0)
        lse_all = jnp.zeros((BLOCK, B_HEADS), _F32)
        for c in range(B_W // LANES):
            cs = slice(c * LANES, (c + 1) * LANES)
            q2 = q_ref[0, 0, rows, cs]
            k2 = _kv_window(kp_ref, kc_ref, j, (0, 0), cs)
            v2 = _kv_window(vp_ref, vc_ref, j, (0, 0), cs)
            qs = jnp.concatenate([jnp.where(lo, q2, zero), jnp.where(lo, zero, q2)], axis=0)
            s = lax.dot_general(qs, k2, (((1,), (1,)), ((), ())), preferred_element_type=_F32) + bj
            m = jnp.max(s, axis=-1, keepdims=True)
            p = jnp.exp(s - m)
            l = jnp.sum(p, axis=-1, keepdims=True)
            on = jnp.dot(p.astype(_BF), v2, preferred_element_type=_F32) * (1.0 / l)
            o_ref[0, 0, rows, cs] = jnp.where(lo, on[:BLOCK], on[BLOCK:]).astype(o_ref.dtype)
            lse = m + jnp.log(l)
            lse_all = jnp.where(hcol == 2 * c, lse[:BLOCK], lse_all)
            lse_all = jnp.where(hcol == 2 * c + 1, lse[BLOCK:], lse_all)
        lse_ref[0, 0, rows, :] = lse_all


def _band_attn(q, k, v, dil):
    b, _, m, w = q.shape
    nq = min(Q_BLOCKS, m // BLOCK)
    tq = nq * BLOCK
    cur = pl.BlockSpec((1, 1, tq, w), lambda bb, r, i: (bb, r, i, 0))
    prev = pl.BlockSpec((1, 1, BLOCK, w), lambda bb, r, i: (bb, r, jnp.maximum(i * nq - 1, 0), 0))
    window, _ = [p for p in B_PATTERNS if p[1] == dil][0]
    return pl.pallas_call(
        functools.partial(_band_attn_kernel, max_dist=window // dil, nq=nq),
        grid=(b, dil, m // tq),
        in_specs=[cur, prev, cur, prev, cur],
        out_specs=[cur, pl.BlockSpec((1, 1, tq, B_HEADS), lambda bb, r, i: (bb, r, i, 0))],
        out_shape=[jax.ShapeDtypeStruct((b, dil, m, w), _BF),
                   jax.ShapeDtypeStruct((b, dil, m, B_HEADS), _F32)],
        compiler_params=_params(("parallel", "parallel", "arbitrary")),
        name=f"band_attn_d{dil}",
    )(q, k, k, v, v)


def _swa_kernel(sink_ref, q_ref, kp_ref, kc_ref, vp_ref, vc_ref, o_ref, *, nq):
    first_step = pl.program_id(1) == 0
    bias = _band_bias(A_WINDOW - 1, None)
    bias0 = _band_bias(A_WINDOW - 1, first_step)
    lane = lax.broadcasted_iota(jnp.int32, (1, LANES), 1)
    lo = lane < HEAD_DIM
    col0 = lax.broadcasted_iota(jnp.int32, (1, 2 * BLOCK), 1) == 0
    for j in range(nq):
        rows = slice(j * BLOCK, (j + 1) * BLOCK)
        bj = bias0 if j == 0 else bias
        bj = jnp.concatenate([bj, bj], axis=0)
        k2 = _kv_window(kp_ref, kc_ref, j, (0,), slice(None))
        v2 = _kv_window(vp_ref, vc_ref, j, (0,), slice(None))
        for c in range(A_QW // LANES):
            g = (2 * c) // A_GROUP
            keep = lo if g == 0 else jnp.logical_not(lo)
            qc = q_ref[0, rows, c * LANES:(c + 1) * LANES].astype(_F32)
            qr = pltpu.roll(qc, HEAD_DIM, 1)
            tiles = [jnp.where(keep, qc if half == g else qr, 0.0).astype(_BF) for half in range(2)]
            qs = jnp.concatenate(tiles, axis=0)
            s = lax.dot_general(qs, k2, (((1,), (1,)), ((), ())), preferred_element_type=_F32) + bj
            s = jnp.concatenate([jnp.where(col0, sink_ref[2 * c + half], s[half * BLOCK:(half + 1) * BLOCK])
                                 for half in range(2)], axis=0)
            m = jnp.max(s, axis=-1, keepdims=True)
            p = jnp.exp(s - m)
            l = jnp.sum(p, axis=-1, keepdims=True)
            p = jnp.where(col0, 0.0, p).astype(_BF)
            on = jnp.dot(p, v2, preferred_element_type=_F32) * (1.0 / l)
            halves = [on[half * BLOCK:(half + 1) * BLOCK] for half in range(2)]
            halves = [t if half == g else pltpu.roll(t, HEAD_DIM, 1) for half, t in enumerate(halves)]
            _put(o_ref.at[0, rows, c * LANES:(c + 1) * LANES], jnp.where(lo, halves[0], halves[1]))


def _swa_prompt(q, k, v, sinks):
    b, s, _ = q.shape
    nq = Q_BLOCKS
    tq = nq * BLOCK
    assert A_WINDOW <= BLOCK, "the sink logit borrows a key column that no query row can reach"
    qspec = pl.BlockSpec((1, tq, A_QW), lambda bb, i: (bb, i, 0))
    cur = pl.BlockSpec((1, tq, A_KVW), lambda bb, i: (bb, i, 0))
    prev = pl.BlockSpec((1, BLOCK, A_KVW), lambda bb, i: (bb, jnp.maximum(i * nq - 1, 0), 0))
    return pl.pallas_call(
        functools.partial(_swa_kernel, nq=nq),
        grid=(b, s // tq),
        in_specs=[pl.BlockSpec(memory_space=pltpu.SMEM), qspec, prev, cur, prev, cur],
        out_specs=qspec,
        out_shape=jax.ShapeDtypeStruct((b, s, A_QW), _BF),
        compiler_params=_params(("parallel", "arbitrary")),
        name="swa_prompt",
    )(sinks, q, k, k, v, v)


def _roll_in(x_ref, new8_ref, o_ref, t_new):
    rows, width = x_ref.shape[-2], x_ref.shape[-1]
    new8 = new8_ref[0, 0]
    tail = jnp.concatenate([jnp.zeros((LANES - new8.shape[0], new8.shape[1]), _F32), new8], axis=0).T[:rows]
    lane = lax.broadcasted_iota(jnp.int32, (1, LANES), 1)
    keep = lane < LANES - t_new
    nxt = pltpu.roll(x_ref[0, :, 0:LANES], LANES - t_new, 1)
    for c in range(width // LANES):
        cur = nxt
        if c + 1 < width // LANES:
            nxt = pltpu.roll(x_ref[0, :, (c + 1) * LANES:(c + 2) * LANES], LANES - t_new, 1)
        else:
            nxt = tail
        o_ref[0, :, c * LANES:(c + 1) * LANES] = jnp.where(keep, cur, nxt)


def _sample_half(g, refs, *, t_new):
    (sink_ref, qa_ref, kan_ref, van_ref, cak_ref, cav_ref, ak8_ref, av8_ref,
     qb_ref, kb8_ref, vb8_ref, cbk_ref, cbv_ref,
     oa_ref, ob_ref, nak_ref, nav_ref, nbk_ref, nbv_ref) = refs
    wa, wb = cak_ref.shape[-1], cbk_ref.shape[-1]

    rows_a = t_new * A_GROUP
    ra = lax.broadcasted_iota(jnp.int32, (rows_a, 1), 0)
    qi_a = _div(ra, A_GROUP)
    rel_a = wa + qi_a - lax.broadcasted_iota(jnp.int32, (1, wa), 1)
    valid_a = (rel_a >= 0) & (rel_a < A_WINDOW)
    q = qa_ref[0, 0]
    s = jnp.dot(q.astype(_BF), cak_ref[0].astype(_BF), preferred_element_type=_F32)
    s = jnp.where(valid_a, s, NEG)
    s_new = [jnp.where(qi_a >= j, jnp.sum(q * kan_ref[0, 0, j:j + 1, :], axis=-1, keepdims=True), NEG)
             for j in range(t_new)]
    sink = jnp.zeros((rows_a, 1), _F32)
    for u in range(A_GROUP):
        sink = jnp.where((ra & (A_GROUP - 1)) == u, sink_ref[g * A_GROUP + u], sink)
    m = functools.reduce(jnp.maximum, [jnp.max(s, axis=-1, keepdims=True), sink] + s_new)
    p = jnp.exp(s - m)
    p_new = [jnp.exp(z - m) for z in s_new]
    l = functools.reduce(jnp.add, [jnp.sum(p, axis=-1, keepdims=True), jnp.exp(sink - m)] + p_new)
    o = lax.dot_general(p.astype(_BF), cav_ref[0].astype(_BF), (((1,), (1,)), ((), ())),
                        preferred_element_type=_F32)
    for j in range(t_new):
        o = o + p_new[j] * van_ref[0, 0, j:j + 1, :]
    oa_ref[0, 0] = o * (1.0 / l)
    _roll_in(cak_ref, ak8_ref, nak_ref, t_new)
    _roll_in(cav_ref, av8_ref, nav_ref, t_new)

    width_b = cbk_ref.shape[1]
    r0 = kb8_ref.shape[2] - t_new
    rows_b = t_new * SUBLANES
    sub = lax.broadcasted_iota(jnp.int32, (SUBLANES, width_b), 0)
    own = _div(lax.broadcasted_iota(jnp.int32, (SUBLANES, width_b), 1), HEAD_DIM) == sub
    qbd = jnp.concatenate([jnp.where(own, qb_ref[0, 0, i:i + 1, :], 0.0) for i in range(t_new)], axis=0)
    s_all = jnp.dot(qbd.astype(_BF), cbk_ref[0].astype(_BF), preferred_element_type=_F32)
    qi = _div(lax.broadcasted_iota(jnp.int32, (rows_b, 1), 0), SUBLANES)
    s_new = [jnp.sum(qbd * kb8_ref[0, 0, r0 + j:r0 + j + 1, :], axis=-1, keepdims=True) for j in range(t_new)]
    pats = []
    for window, dil in B_PATTERNS:
        lo_lane = max(wb - (-(-window // LANES) * LANES), 0)
        wl = lo_lane + lax.broadcasted_iota(jnp.int32, (1, wb - lo_lane), 1)
        rel = wb + qi - wl
        valid = (rel <= window) & _multiple(rel, dil)
        s = jnp.where(valid, s_all[:, lo_lane:], NEG)
        sn = [jnp.where((qi - j >= 0) & _multiple(qi - j, dil), s_new[j], NEG) for j in range(t_new)]
        m = functools.reduce(jnp.maximum, [jnp.max(s, axis=-1, keepdims=True)] + sn)
        p = jnp.exp(s - m)
        pn = [jnp.exp(z - m) for z in sn]
        l = functools.reduce(jnp.add, [jnp.sum(p, axis=-1, keepdims=True)] + pn)
        pats.append((lo_lane, p, pn, l, m + jnp.log(l)))
    mx = functools.reduce(jnp.maximum, [t[4] for t in pats])
    es = [jnp.exp(t[4] - mx) for t in pats]
    den = functools.reduce(jnp.add, es)
    coef = [e / (den * t[3]) for e, t in zip(es, pats)]
    starts = sorted({t[0] for t in pats} | {wb})
    blocks = []
    for a, b in zip(starts[:-1], starts[1:]):
        acc = None
        for c, (lo_lane, p, _, _, _) in zip(coef, pats):
            if lo_lane <= a:
                term = c * p[:, a - lo_lane:b - lo_lane]
                acc = term if acc is None else acc + term
        blocks.append(acc)
    lead = starts[0]
    p_comb = jnp.concatenate(blocks, axis=1) if len(blocks) > 1 else blocks[0]
    o_full = lax.dot_general(p_comb.astype(_BF), cbv_ref[0, :, lead:].astype(_BF), (((1,), (1,)), ((), ())),
                             preferred_element_type=_F32)
    for j in range(t_new):
        pj = functools.reduce(jnp.add, [c * t[2][j] for c, t in zip(coef, pats)])
        o_full = o_full + pj * vb8_ref[0, 0, r0 + j:r0 + j + 1, :]
    for i in range(t_new):
        blk = jnp.where(own, o_full[i * SUBLANES:(i + 1) * SUBLANES], 0.0)
        ob_ref[0, 0, i:i + 1, :] = jnp.sum(blk, axis=0, keepdims=True)
    _roll_in(cbk_ref, kb8_ref, nbk_ref, t_new)
    _roll_in(cbv_ref, vb8_ref, nbv_ref, t_new)


def _ffn_chunk(h, wg_ref, wu_ref, wd_ref, cs):
    gt = jnp.dot(h, wg_ref[:, cs], preferred_element_type=_F32)
    up = jnp.dot(h, wu_ref[:, cs], preferred_element_type=_F32)
    act = (gt * jax.nn.sigmoid(gt) * up).astype(_BF)
    return jnp.dot(act, wd_ref[cs, :], preferred_element_type=_F32)


def _attn_residual(x, oa, ob, gta, shf, scf, goa_ref, gob_ref, wo_ref, gf_ref):
    merged = jnp.concatenate([_rms(oa, goa_ref[...]), _rms(ob, gob_ref[...])], axis=1)
    x1 = x + gta * jnp.dot(merged.astype(_BF), wo_ref[...], preferred_element_type=_F32)
    return x1, (_rms(x1, gf_ref[...]) * (1.0 + scf) + shf).astype(_BF)


_TWO = lambda r: r[...].reshape(-1, r.shape[-1])
_FF_CUTS = (0, 384, 1152, 2048, 2816)
_PHASES = len(_FF_CUTS) - 1


def _ffn_and_sample_kernel(*refs, t_new):
    (x_ref, oa_ref, o1_ref, o4_ref, o16_ref, l1_ref, l2_ref, l3_ref, e_ref,
     gta_ref, shf_ref, scf_ref, gtf_ref, goa_ref, gob_ref, wo_ref, gf_ref, wg_ref, wu_ref, wd_ref, gfin_ref) = refs[:21]
    sample_in = refs[21:34]
    y_ref = refs[34]
    sample_out = refs[35:41]
    s4_ref, s16_ref, x1_ref, h_ref, acc_ref = refs[41:]
    step = pl.program_id(0)
    half = step % 2
    phase = step % _PHASES
    tm = x_ref.shape[1]

    def mix_and_project():
        for dil, src, stage in ((4, o4_ref, s4_ref), (16, o16_ref, s16_ref)):
            for r in range(dil):
                for c in range(B_W // LANES):
                    stage[c, pl.ds(r, tm // dil, stride=dil), :] = src[0, r, :, c * LANES:(c + 1) * LANES].astype(_F32)
        gather = lambda stage: jnp.concatenate([stage[c] for c in range(B_W // LANES)], axis=1)
        outs = [_TWO(o1_ref).astype(_F32), gather(s4_ref), gather(s16_ref)]
        lses = [_TWO(l1_ref), _TWO(l2_ref), _TWO(l3_ref)]
        mx = jnp.maximum(jnp.maximum(lses[0], lses[1]), lses[2])
        es = [jnp.exp(z - mx) for z in lses]
        den = es[0] + es[1] + es[2]
        ob = jnp.zeros((tm, B_W), _F32)
        for e, o in zip(es, outs):
            alpha = e / den
            hi = alpha.astype(_BF)
            lo = (alpha - hi.astype(_F32)).astype(_BF)
            wide = (jnp.dot(hi, e_ref[...], preferred_element_type=_F32)
                    + jnp.dot(lo, e_ref[...], preferred_element_type=_F32))
            ob = ob + wide * o
        x1, h = _attn_residual(_TWO(x_ref), _TWO(oa_ref).astype(_F32), ob, _TWO(gta_ref), _TWO(shf_ref), _TWO(scf_ref),
                               goa_ref, gob_ref, wo_ref, gf_ref)
        x1_ref[...] = x1
        h_ref[...] = h

    for k in range(_PHASES):
        @pl.when(phase == k)
        def _(k=k):
            _sample_half(half, tuple(sample_in) + tuple(sample_out), t_new=t_new)
            if k == 0:
                mix_and_project()
            part = _ffn_chunk(h_ref[...], wg_ref, wu_ref, wd_ref, slice(_FF_CUTS[k], _FF_CUTS[k + 1]))
            if k == 0:
                acc_ref[...] = part
            elif k < _PHASES - 1:
                acc_ref[...] += part
            else:
                x2 = x1_ref[...] + _TWO(gtf_ref) * (acc_ref[...] + part)
                y_ref[...] = _rms(x2, gfin_ref[...]).reshape(y_ref.shape)


def _ffn_and_sample(x, oa, obs, lses, mods, goa, gob, wo, gf, wg, wu, wd, gfin, sinks, sample_ins, *, tm):
    b, s, d = x.shape
    n = sample_ins[0].shape[0]
    tiles = s // tm
    steps = 2 * n
    assert steps == b * tiles * _PHASES and _PHASES % 2 == 0 and wg.shape[1] == _FF_CUTS[-1]
    t_new = sample_ins[0].shape[2] // A_GROUP

    def tile(j):
        t = j // _PHASES
        return t // tiles, t % tiles

    row = lambda c: pl.BlockSpec((1, tm, c), lambda j: tile(j) + (0,))
    res = lambda dil: pl.BlockSpec((1, dil, tm // dil, B_W), lambda j: (tile(j)[0], 0, tile(j)[1], 0))
    mod_spec = pl.BlockSpec((1, 1, d), lambda j: (tile(j)[0], 0, 0))
    expand = (jnp.arange(B_W)[None, :] // HEAD_DIM == jnp.arange(B_HEADS)[:, None]).astype(_BF)
    ffn_ins = [x, oa] + list(obs) + list(lses) + [expand] + list(mods) + [goa, gob, wo, gf, wg, wu, wd, gfin]
    ffn_specs = ([row(d), row(A_QW)] + [res(o.shape[1]) for o in obs] + [row(B_HEADS)] * len(lses)
                 + [_resident(expand.shape)] + [mod_spec] * 4
                 + [_resident(a.shape) for a in (goa, gob, wo, gf, wg, wu, wd, gfin)])
    half_blk = lambda a: pl.BlockSpec((1, 1) + a.shape[2:], lambda j: (j // 2, j % 2) + (0,) * (a.ndim - 2))
    (qa_g, ka_n, va_n, ca_k, ca_v, ak8, av8, qb_h, kb8, vb8, cb_k, cb_v) = sample_ins
    sample_outs = [jax.ShapeDtypeStruct(a.shape, _F32) for a in (qa_g, qb_h, ca_k, ca_v, cb_k, cb_v)]
    cache_blk = lambda a: pl.BlockSpec((1, a.shape[1] // 2, a.shape[2]), lambda j: (j // 2, j % 2, 0))
    s_specs = [half_blk(qa_g), half_blk(ka_n), half_blk(va_n), cache_blk(ca_k), cache_blk(ca_v), half_blk(ak8),
               half_blk(av8), half_blk(qb_h), half_blk(kb8), half_blk(vb8), cache_blk(cb_k), cache_blk(cb_v)]
    o_specs = [half_blk(qa_g), half_blk(qb_h), cache_blk(ca_k), cache_blk(ca_v), cache_blk(cb_k), cache_blk(cb_v)]
    stage = pltpu.VMEM((B_W // LANES, tm, LANES), _F32)
    outs = pl.pallas_call(
        functools.partial(_ffn_and_sample_kernel, t_new=t_new),
        grid=(steps,),
        in_specs=ffn_specs + [pl.BlockSpec(memory_space=pltpu.SMEM)] + s_specs,
        out_specs=[row(d)] + o_specs,
        out_shape=[jax.ShapeDtypeStruct((b, s, d), _F32)] + sample_outs,
        scratch_shapes=[stage, stage, pltpu.VMEM((tm, d), _F32), pltpu.VMEM((tm, d), _BF), pltpu.VMEM((tm, d), _F32)],
        compiler_params=_params(("arbitrary",)),
        name="ffn_and_sample",
    )(*ffn_ins, sinks, *sample_ins)
    return outs


def _merge_ffn_sample_kernel(x_ref, oa_ref, ob_ref, gta_ref, shf_ref, scf_ref, gtf_ref,
                             goa_ref, gob_ref, wo_ref, gf_ref, wg_ref, wu_ref, wd_ref, gfin_ref, y_ref):
    x1, h = _attn_residual(_TWO(x_ref), _TWO(oa_ref), _TWO(ob_ref), gta_ref[...], shf_ref[...], scf_ref[...],
                           goa_ref, gob_ref, wo_ref, gf_ref)
    acc = jnp.zeros_like(x1)
    for k in range(_PHASES):
        acc = acc + _ffn_chunk(h, wg_ref, wu_ref, wd_ref, slice(_FF_CUTS[k], _FF_CUTS[k + 1]))
    y_ref[...] = _rms(x1 + gtf_ref[...] * acc, gfin_ref[...]).reshape(y_ref.shape)


def _merge_ffn_sample(x, oa, ob, mods, goa, gob, wo, gf, wg, wu, wd, gfin):
    t, n, d = x.shape
    row = lambda c: pl.BlockSpec((1, n, c), lambda j: (j, 0, 0))
    mod_spec = pl.BlockSpec((n, d), lambda j: (0, 0))
    weights = (goa, gob, wo, gf, wg, wu, wd, gfin)
    return pl.pallas_call(
        _merge_ffn_sample_kernel,
        grid=(t,),
        in_specs=[row(d), row(A_QW), row(B_W)] + [mod_spec] * 4 + [_resident(a.shape) for a in weights],
        out_specs=row(d),
        out_shape=jax.ShapeDtypeStruct((t, n, d), _F32),
        compiler_params=_params(("parallel",)),
        name="merge_ffn_sample",
    )(x, oa, ob, *mods, *weights)


def _rope_tables(pos):
    half = HEAD_DIM // 2
    inv = jnp.exp(-math.log(ROPE_THETA) * jnp.arange(half, dtype=_F32) * (2.0 / HEAD_DIM))
    ang = pos.astype(_F32)[:, None] * inv[None, :]
    cos, sin = jnp.cos(ang), jnp.sin(ang)
    return jnp.tile(cos, (1, LANES // half)), jnp.tile(jnp.concatenate([-sin, sin], axis=1), (1, LANES // HEAD_DIM))


def _window_on_lanes(cache):
    _, n, w, h, dh = cache.shape
    return cache.transpose(0, 1, 3, 4, 2).reshape(n, h * dh, w)


def _window_off_lanes(t, h):
    n, _, w = t.shape
    return t.reshape(1, n, h, HEAD_DIM, w).transpose(0, 1, 4, 2, 3)


def _halves_last_rows(new_t, pad_to):
    t, n, c = new_t.shape
    v = new_t.reshape(t, n, 2, c // 2).transpose(1, 2, 0, 3)
    return jnp.pad(v, ((0, 0), (0, 0), (SUBLANES - t, 0), (0, pad_to - c // 2)))


def kernel(x_prompt, x_sample, c_prompt, c_sample, cache_a_k, cache_a_v, cache_b_k, cache_b_v, w_ada, b_ada, g_attn, w_in, sinks, g_out_a, g_out_b, w_o, g_ffn, w_gate, w_up, w_down, g_final):
    nb, s, d = x_prompt.shape
    ns, t_new, _ = x_sample.shape
    assert w_ada.shape[0] == 1, "single trunk layer"
    l = 0
    bf = lambda w: w.astype(_BF)
    row = lambda g: g.reshape(1, -1)

    c_all = jnp.concatenate([c_prompt, c_sample], axis=0)
    c_all = jnp.pad(c_all, ((0, (-c_all.shape[0]) % SUBLANES), (0, 0)))
    mod = _adaln(c_all, bf(w_ada[l]), b_ada[l].reshape(1, -1))
    mod_p = [m.reshape(nb, 1, d) for m in jnp.split(mod[:nb], 6, axis=-1)]
    mod_s = jnp.split(mod[nb:nb + ns], 6, axis=-1)

    cos_p, sin_p = _rope_tables(jnp.arange(s, dtype=jnp.int32))
    cos_s, sin_s = _rope_tables(PAST_LEN + jnp.arange(t_new, dtype=jnp.int32))
    w_in_bf = bf(w_in[l])
    ffn_w = (row(g_out_a[l]), row(g_out_b[l]), bf(w_o[l]), row(g_ffn[l]), bf(w_gate[l]), bf(w_up[l]), bf(w_down[l]),
             row(g_final))

    wa_p, wb_p = min(A_WINDOW, s), min(B_WINDOW, s)
    (qa, ka, va, q1, k1, v1, q4, k4, v4, q16, k16, v16, ka_t, va_t, kb_t, vb_t) = _inproj_prompt(
        x_prompt, mod_p[0], mod_p[1], row(g_attn[l]), w_in_bf, cos_p, sin_p, tm=512, tail_a=wa_p, tail_b=wb_p)
    xs_t = x_sample.transpose(1, 0, 2)
    qa_s, ka_s, va_s, qb_s, kb_s, vb_s = _inproj_sample(
        xs_t, mod_s[0], mod_s[1], row(g_attn[l]), w_in_bf, cos_s.reshape(t_new, 1, LANES),
        sin_s.reshape(t_new, 1, LANES))

    oa = _swa_prompt(qa, ka, va, sinks[l])
    obs, lses = [], []
    for (q, k, v), (_, dil) in zip(((q1, k1, v1), (q4, k4, v4), (q16, k16, v16)), B_PATTERNS):
        o, lse = _band_attn(q, k, v, dil)
        obs.append(o)
        lses.append(lse.transpose(0, 2, 1, 3).reshape(nb, s, B_HEADS))

    qa_g = (qa_s.reshape(t_new, ns, A_KV_HEADS, A_GROUP, HEAD_DIM).transpose(1, 2, 0, 3, 4)
            .reshape(ns, A_KV_HEADS, t_new * A_GROUP, HEAD_DIM))
    kv_g = lambda t: t.reshape(t_new, ns, A_KV_HEADS, HEAD_DIM).transpose(1, 2, 0, 3)
    qb_h = qb_s.reshape(t_new, ns, 2, B_W // 2).transpose(1, 2, 0, 3)
    sample_ins = (qa_g, kv_g(ka_s), kv_g(va_s), _window_on_lanes(cache_a_k), _window_on_lanes(cache_a_v),
                  _halves_last_rows(ka_s, LANES), _halves_last_rows(va_s, LANES), qb_h,
                  _halves_last_rows(kb_s, B_W // 2), _halves_last_rows(vb_s, B_W // 2),
                  _window_on_lanes(cache_b_k), _window_on_lanes(cache_b_v))
    y_prompt, oa_g, ob_h, na_k, na_v, nb_k, nb_v = _ffn_and_sample(
        x_prompt, oa, obs, lses, mod_p[2:6], *ffn_w, sinks[l], sample_ins, tm=256)
    pa_k = ka_t.reshape(1, nb, wa_p, A_KV_HEADS, HEAD_DIM)
    pa_v = va_t.reshape(1, nb, wa_p, A_KV_HEADS, HEAD_DIM)
    pb_k = kb_t.reshape(1, nb, wb_p, B_HEADS, HEAD_DIM)
    pb_v = vb_t.reshape(1, nb, wb_p, B_HEADS, HEAD_DIM)

    oa_t = (oa_g.reshape(ns, A_KV_HEADS, t_new, A_GROUP, HEAD_DIM).transpose(2, 0, 1, 3, 4)
            .reshape(t_new, ns, A_QW))
    ob_t = ob_h.transpose(2, 0, 1, 3).reshape(t_new, ns, B_W)
    y_s = _merge_ffn_sample(xs_t, oa_t, ob_t, mod_s[2:6], *ffn_w)
    y_sample = y_s.transpose(1, 0, 2)
    sa_k, sa_v = _window_off_lanes(na_k, A_KV_HEADS), _window_off_lanes(na_v, A_KV_HEADS)
    sb_k, sb_v = _window_off_lanes(nb_k, B_HEADS), _window_off_lanes(nb_v, B_HEADS)

    return (y_prompt, y_sample, pa_k, pa_v, pb_k, pb_v, sa_k, sa_v, sb_k, sb_v)
```

```python
import functools
import math

import jax
import jax.numpy as jnp
from jax import lax
from jax.experimental import pallas as pl
from jax.experimental.pallas import tpu as pltpu

HEAD_DIM = 64
A_Q_HEADS = 8
A_KV_HEADS = 2
A_GROUP = A_Q_HEADS // A_KV_HEADS
B_HEADS = 8
A_WINDOW = 128
BLOCK = 128
B_PATTERNS = ((128, 1), (512, 4), (2048, 16))
B_WINDOW = 2048
PAST_LEN = 8192
ROPE_THETA = 10000.0
EPS = 1e-6
A_QW = A_Q_HEADS * HEAD_DIM
A_KVW = A_KV_HEADS * HEAD_DIM
B_W = B_HEADS * HEAD_DIM
SCALE = HEAD_DIM ** -0.5
LANES = 128
SUBLANES = 8
NEG = -1e30
VMEM_LIMIT = 56 * 1024 * 1024
Q_BLOCKS_B = 8
Q_BLOCKS_A = 4

_BF = jnp.bfloat16
_F32 = jnp.float32


def _params(sem, vmem=VMEM_LIMIT):
    return pltpu.CompilerParams(dimension_semantics=sem, vmem_limit_bytes=vmem)


def _resident(shape):
    nd = len(shape)
    return pl.BlockSpec(shape, lambda *_: (0,) * nd, pipeline_mode=pl.Buffered(1))


def _rms(x, g):
    return x * lax.rsqrt(jnp.mean(x * x, axis=-1, keepdims=True) + EPS) * g


def _log2(n):
    assert n > 0 and n & (n - 1) == 0, "power of two expected"
    return n.bit_length() - 1


def _div(x, n):
    return x >> _log2(n)


def _multiple(x, n):
    _log2(n)
    return (x & (n - 1)) == 0


def _adaln_kernel(c_ref, w_ref, b_ref, o_ref):
    c = c_ref[...]
    a = (c * jax.nn.sigmoid(c)).astype(_BF)
    o_ref[...] = jnp.dot(a, w_ref[...].astype(_BF), preferred_element_type=_F32) + b_ref[...]


def _adaln(c_all, w, b):
    m, d = c_all.shape
    n = w.shape[1]
    tn = 768
    return pl.pallas_call(
        _adaln_kernel,
        grid=(n // tn,),
        in_specs=[pl.BlockSpec((m, d), lambda j: (0, 0)),
                  pl.BlockSpec((d, tn), lambda j: (0, j)),
                  pl.BlockSpec((1, tn), lambda j: (0, j))],
        out_specs=pl.BlockSpec((m, tn), lambda j: (0, j)),
        out_shape=jax.ShapeDtypeStruct((m, n), _F32),
        compiler_params=_params(("arbitrary",)),
        name="adaln",
    )(c_all, w, b)


def _rope(x, cos, sin_signed, first_half):
    sw = jnp.where(first_half, pltpu.roll(x, 96, 1), pltpu.roll(x, 32, 1))
    return x * cos + sw * sin_signed


def _project(x_ref, sh_ref, sc_ref, g_ref, w_ref, cos_ref, sin_ref):
    d = x_ref.shape[-1]
    x = x_ref[...].reshape(-1, d)
    sh = sh_ref[...].reshape(-1, d)
    sc = sc_ref[...].reshape(-1, d)
    h = _rms(x, g_ref[...]) * (1.0 + sc) + sh
    proj = jnp.dot(h.astype(_BF), w_ref[...], preferred_element_type=_F32)
    cos = cos_ref[...].reshape(-1, LANES)
    sin = sin_ref[...].reshape(-1, LANES)
    lane = lax.broadcasted_iota(jnp.int32, (1, LANES), 1)
    first_half = (lane & (HEAD_DIM - 1)) < (HEAD_DIM // 2)

    def rot(col0, ncols):
        parts = [_rope(proj[:, col0 + j * LANES: col0 + (j + 1) * LANES], cos, sin, first_half)
                 for j in range(ncols // LANES)]
        return parts[0] if len(parts) == 1 else jnp.concatenate(parts, axis=1)

    o = 0
    qa = rot(o, A_QW) * SCALE; o += A_QW
    ka = rot(o, A_KVW); o += A_KVW
    va = proj[:, o:o + A_KVW]; o += A_KVW
    qb = rot(o, B_W) * SCALE; o += B_W
    kb = rot(o, B_W); o += B_W
    vb = proj[:, o:o + B_W]
    return qa, ka, va, qb, kb, vb


def _put(ref, val):
    ref[...] = val.astype(ref.dtype).reshape(ref.shape)


def _inproj_prompt_kernel(x_ref, sh_ref, sc_ref, g_ref, w_ref, cos_ref, sin_ref,
                          qa_ref, ka_ref, va_ref, q1_ref, k1_ref, v1_ref, q4_ref, k4_ref, v4_ref,
                          q16_ref, k16_ref, v16_ref, kat_ref, vat_ref, kbt_ref, vbt_ref,
                          sq_ref, sk_ref, sv_ref):
    qa, ka, va, qb, kb, vb = _project(x_ref, sh_ref, sc_ref, g_ref, w_ref, cos_ref, sin_ref)
    tm = qa.shape[0]
    _put(qa_ref, qa); _put(ka_ref, ka); _put(va_ref, va)
    _put(q1_ref, qb); _put(k1_ref, kb); _put(v1_ref, vb)
    ta = kat_ref.shape[1]
    _put(kat_ref, ka[tm - ta:]); _put(vat_ref, va[tm - ta:])
    _put(kbt_ref, kb); _put(vbt_ref, vb)
    for val, stage, r4, r16 in ((qb, sq_ref, q4_ref, q16_ref), (kb, sk_ref, k4_ref, k16_ref),
                                (vb, sv_ref, v4_ref, v16_ref)):
        for c in range(B_W // LANES):
            stage[c] = val[:, c * LANES:(c + 1) * LANES]
        for dil, out in ((4, r4), (16, r16)):
            n = tm // dil
            for r in range(dil):
                for c in range(B_W // LANES):
                    out[0, r, :, c * LANES:(c + 1) * LANES] = stage[c, pl.ds(r, n, stride=dil), :].astype(out.dtype)


def _inproj_prompt(x, sh, sc, g, w_bf, cos, sin, *, tm, tail_a, tail_b):
    b, s, d = x.shape
    nt = s // tm
    assert tail_b % tm == 0 and tail_a <= tm
    row = lambda c: pl.BlockSpec((1, tm, c), lambda bb, i: (bb, i, 0))
    res = lambda dil: pl.BlockSpec((1, dil, tm // dil, B_W), lambda bb, i: (bb, 0, i, 0))
    mod_spec = pl.BlockSpec((1, 1, d), lambda bb, i: (bb, 0, 0))
    tab_spec = pl.BlockSpec((tm, LANES), lambda bb, i: (i, 0))
    tail_a_spec = pl.BlockSpec((1, tail_a, A_KVW), lambda bb, i: (bb, 0, 0))
    tail_b_spec = pl.BlockSpec((1, tm, B_W), lambda bb, i: (bb, jnp.maximum(i - (nt - tail_b // tm), 0), 0))
    shp = lambda c, dt: jax.ShapeDtypeStruct((b, s, c), dt)
    rshp = lambda dil: jax.ShapeDtypeStruct((b, dil, s // dil, B_W), _BF)
    return pl.pallas_call(
        _inproj_prompt_kernel,
        grid=(b, nt),
        in_specs=[row(d), mod_spec, mod_spec, _resident((1, d)), _resident(w_bf.shape), tab_spec, tab_spec],
        out_specs=[row(A_QW), row(A_KVW), row(A_KVW)] + [res(1)] * 3 + [res(4)] * 3 + [res(16)] * 3
                  + [tail_a_spec] * 2 + [tail_b_spec] * 2,
        out_shape=[shp(A_QW, _BF), shp(A_KVW, _BF), shp(A_KVW, _BF)] + [rshp(1)] * 3 + [rshp(4)] * 3 + [rshp(16)] * 3
                  + [jax.ShapeDtypeStruct((b, tail_a, A_KVW), _F32)] * 2
                  + [jax.ShapeDtypeStruct((b, tail_b, B_W), _F32)] * 2,
        scratch_shapes=[pltpu.VMEM((B_W // LANES, tm, LANES), _F32)] * 3,
        compiler_params=_params(("parallel", "arbitrary")),
        name="in_proj_prompt",
    )(x, sh, sc, g, w_bf, cos, sin)


def _inproj_sample_kernel(x_ref, sh_ref, sc_ref, g_ref, w_ref, cos_ref, sin_ref,
                          qa_ref, ka_ref, va_ref, qb_ref, kb_ref, vb_ref):
    vals = _project(x_ref, sh_ref, sc_ref, g_ref, w_ref, cos_ref, sin_ref)
    for ref, val in zip((qa_ref, ka_ref, va_ref, qb_ref, kb_ref, vb_ref), vals):
        _put(ref, val)


def _inproj_sample(x, sh, sc, g, w_bf, cos, sin):
    t, n, d = x.shape
    row = lambda c: pl.BlockSpec((1, n, c), lambda j: (j, 0, 0))
    mod_spec = pl.BlockSpec((n, d), lambda j: (0, 0))
    tab_spec = pl.BlockSpec((1, 1, LANES), lambda j: (j, 0, 0))
    widths = (A_QW, A_KVW, A_KVW, B_W, B_W, B_W)
    return pl.pallas_call(
        _inproj_sample_kernel,
        grid=(t,),
        in_specs=[row(d), mod_spec, mod_spec, _resident((1, d)), _resident(w_bf.shape), tab_spec, tab_spec],
        out_specs=[row(c) for c in widths],
        out_shape=[jax.ShapeDtypeStruct((t, n, c), _F32) for c in widths],
        compiler_params=_params(("parallel",)),
        name="in_proj_sample",
    )(x, sh, sc, g, w_bf, cos, sin)


def _band_bias(max_dist, drop_prev):
    qi = lax.broadcasted_iota(jnp.int32, (BLOCK, 2 * BLOCK), 0)
    ki = lax.broadcasted_iota(jnp.int32, (BLOCK, 2 * BLOCK), 1)
    rel = qi + BLOCK - ki
    valid = (rel >= 0) & (rel <= max_dist)
    if drop_prev is not None:
        valid = valid & ((ki >= BLOCK) | jnp.logical_not(drop_prev))
    return jnp.where(valid, 0.0, NEG).astype(_F32)


def _kv_window(prev_ref, cur_ref, j, lead, cols):
    if j == 0:
        return jnp.concatenate([prev_ref[lead + (slice(None), cols)], cur_ref[lead + (slice(0, BLOCK), cols)]], axis=0)
    return cur_ref[lead + (slice((j - 1) * BLOCK, (j + 1) * BLOCK), cols)]


def _band_attn_kernel(q_ref, kp_ref, kc_ref, vp_ref, vc_ref, o_ref, lse_ref, *, max_dist, nq):
    first_step = pl.program_id(2) == 0
    bias = _band_bias(max_dist, None)
    bias0 = _band_bias(max_dist, first_step)
    lane = lax.broadcasted_iota(jnp.int32, (1, LANES), 1)
    lo = lane < HEAD_DIM
    hcol = lax.broadcasted_iota(jnp.int32, (1, B_HEADS), 1)
    zero = jnp.zeros((), _BF)
    for j in range(nq):
        rows = slice(j * BLOCK, (j + 1) * BLOCK)
        bj = bias0 if j == 0 else bias
        bj = jnp.concatenate([bj, bj], axis=0)
        lse_all = jnp.zeros((BLOCK, B_HEADS), _F32)
        for c in range(B_W // LANES):
            cs = slice(c * LANES, (c + 1) * LANES)
            q2 = q_ref[0, 0, rows, cs]
            k2 = _kv_window(kp_ref, kc_ref, j, (0, 0), cs)
            v2 = _kv_window(vp_ref, vc_ref, j, (0, 0), cs)
            qs = jnp.concatenate([jnp.where(lo, q2, zero), jnp.where(lo, zero, q2)], axis=0)
            s = lax.dot_general(qs, k2, (((1,), (1,)), ((), ())), preferred_element_type=_F32) + bj
            m = jnp.max(s, axis=-1, keepdims=True)
            p = jnp.exp(s - m)
            l = jnp.sum(p, axis=-1, keepdims=True)
            on = jnp.dot(p.astype(_BF), v2, preferred_element_type=_F32) * (1.0 / l)
            o_ref[0, 0, rows, cs] = jnp.where(lo, on[:BLOCK], on[BLOCK:]).astype(o_ref.dtype)
            lse = m + jnp.log(l)
            lse_all = jnp.where(hcol == 2 * c, lse[:BLOCK], lse_all)
            lse_all = jnp.where(hcol == 2 * c + 1, lse[BLOCK:], lse_all)
        lse_ref[0, 0, rows, :] = lse_all


def _band_attn(q, k, v, dil):
    b, _, m, w = q.shape
    nq = min(Q_BLOCKS_B, m // BLOCK)
    tq = nq * BLOCK
    cur = pl.BlockSpec((1, 1, tq, w), lambda bb, r, i: (bb, r, i, 0))
    prev = pl.BlockSpec((1, 1, BLOCK, w), lambda bb, r, i: (bb, r, jnp.maximum(i * nq - 1, 0), 0))
    window, _ = [p for p in B_PATTERNS if p[1] == dil][0]
    return pl.pallas_call(
        functools.partial(_band_attn_kernel, max_dist=window // dil, nq=nq),
        grid=(b, dil, m // tq),
        in_specs=[cur, prev, cur, prev, cur],
        out_specs=[cur, pl.BlockSpec((1, 1, tq, B_HEADS), lambda bb, r, i: (bb, r, i, 0))],
        out_shape=[jax.ShapeDtypeStruct((b, dil, m, w), _BF),
                   jax.ShapeDtypeStruct((b, dil, m, B_HEADS), _F32)],
        compiler_params=_params(("parallel", "parallel", "arbitrary")),
        name=f"band_attn_d{dil}",
    )(q, k, k, v, v)


def _swa_kernel(sink_ref, q_ref, kp_ref, kc_ref, vp_ref, vc_ref, o_ref, *, nq):
    first_step = pl.program_id(1) == 0
    bias = _band_bias(A_WINDOW - 1, None)
    bias0 = _band_bias(A_WINDOW - 1, first_step)
    lane = lax.broadcasted_iota(jnp.int32, (1, LANES), 1)
    lo = lane < HEAD_DIM
    col0 = lax.broadcasted_iota(jnp.int32, (1, 2 * BLOCK), 1) == 0
    for j in range(nq):
        rows = slice(j * BLOCK, (j + 1) * BLOCK)
        bj = bias0 if j == 0 else bias
        bj = jnp.concatenate([bj, bj], axis=0)
        k2 = _kv_window(kp_ref, kc_ref, j, (0,), slice(None))
        v2 = _kv_window(vp_ref, vc_ref, j, (0,), slice(None))
        for c in range(A_QW // LANES):
            g = (2 * c) // A_GROUP
            keep = lo if g == 0 else jnp.logical_not(lo)
            qc = q_ref[0, rows, c * LANES:(c + 1) * LANES].astype(_F32)
            qr = pltpu.roll(qc, HEAD_DIM, 1)
            tiles = [jnp.where(keep, qc if half == g else qr, 0.0).astype(_BF) for half in range(2)]
            qs = jnp.concatenate(tiles, axis=0)
            s = lax.dot_general(qs, k2, (((1,), (1,)), ((), ())), preferred_element_type=_F32) + bj
            s = jnp.concatenate([jnp.where(col0, sink_ref[2 * c + half], s[half * BLOCK:(half + 1) * BLOCK])
                                 for half in range(2)], axis=0)
            m = jnp.max(s, axis=-1, keepdims=True)
            p = jnp.exp(s - m)
            l = jnp.sum(p, axis=-1, keepdims=True)
            p = jnp.where(col0, 0.0, p).astype(_BF)
            on = jnp.dot(p, v2, preferred_element_type=_F32) * (1.0 / l)
            halves = [on[half * BLOCK:(half + 1) * BLOCK] for half in range(2)]
            halves = [t if half == g else pltpu.roll(t, HEAD_DIM, 1) for half, t in enumerate(halves)]
            _put(o_ref.at[0, rows, c * LANES:(c + 1) * LANES], jnp.where(lo, halves[0], halves[1]))


def _swa_prompt(q, k, v, sinks):
    b, s, _ = q.shape
    nq = Q_BLOCKS_A
    tq = nq * BLOCK
    assert A_WINDOW <= BLOCK, "the sink logit borrows a key column that no query row can reach"
    qspec = pl.BlockSpec((1, tq, A_QW), lambda bb, i: (bb, i, 0))
    cur = pl.BlockSpec((1, tq, A_KVW), lambda bb, i: (bb, i, 0))
    prev = pl.BlockSpec((1, BLOCK, A_KVW), lambda bb, i: (bb, jnp.maximum(i * nq - 1, 0), 0))
    return pl.pallas_call(
        functools.partial(_swa_kernel, nq=nq),
        grid=(b, s // tq),
        in_specs=[pl.BlockSpec(memory_space=pltpu.SMEM), qspec, prev, cur, prev, cur],
        out_specs=qspec,
        out_shape=jax.ShapeDtypeStruct((b, s, A_QW), _BF),
        compiler_params=_params(("parallel", "arbitrary")),
        name="swa_prompt",
    )(sinks, q, k, k, v, v)


def _roll_in(x_ref, new8_ref, o_ref, t_new):
    rows, width = x_ref.shape
    new8 = new8_ref[0, 0]
    tail = jnp.concatenate([jnp.zeros((LANES - new8.shape[0], new8.shape[1]), _F32), new8], axis=0).T[:rows]
    lane = lax.broadcasted_iota(jnp.int32, (1, LANES), 1)
    keep = lane < LANES - t_new
    nxt = pltpu.roll(x_ref[:, 0:LANES], LANES - t_new, 1)
    for c in range(width // LANES):
        cur = nxt
        if c + 1 < width // LANES:
            nxt = pltpu.roll(x_ref[:, (c + 1) * LANES:(c + 2) * LANES], LANES - t_new, 1)
        else:
            nxt = tail
        o_ref[:, c * LANES:(c + 1) * LANES] = jnp.where(keep, cur, nxt)


def _sample_half(g, refs, *, t_new):
    (sink_ref, qa_ref, kan_ref, van_ref, cak_ref, cav_ref, ak8_ref, av8_ref,
     qb_ref, kb8_ref, vb8_ref, cbk_ref, cbv_ref,
     oa_ref, ob_ref, nak_ref, nav_ref, nbk_ref, nbv_ref) = refs
    wa, wb = cak_ref.shape[-1], cbk_ref.shape[-1]

    rows_a = t_new * A_GROUP
    ra = lax.broadcasted_iota(jnp.int32, (rows_a, 1), 0)
    qi_a = _div(ra, A_GROUP)
    rel_a = wa + qi_a - lax.broadcasted_iota(jnp.int32, (1, wa), 1)
    valid_a = (rel_a >= 0) & (rel_a < A_WINDOW)
    q = qa_ref[0, 0]
    s = jnp.dot(q.astype(_BF), cak_ref[0].astype(_BF), preferred_element_type=_F32)
    s = jnp.where(valid_a, s, NEG)
    s_new = [jnp.where(qi_a >= j, jnp.sum(q * kan_ref[0, 0, j:j + 1, :], axis=-1, keepdims=True), NEG)
             for j in range(t_new)]
    sink = jnp.zeros((rows_a, 1), _F32)
    for u in range(A_GROUP):
        sink = jnp.where((ra & (A_GROUP - 1)) == u, sink_ref[g * A_GROUP + u], sink)
    m = functools.reduce(jnp.maximum, [jnp.max(s, axis=-1, keepdims=True), sink] + s_new)
    p = jnp.exp(s - m)
    p_new = [jnp.exp(z - m) for z in s_new]
    l = functools.reduce(jnp.add, [jnp.sum(p, axis=-1, keepdims=True), jnp.exp(sink - m)] + p_new)
    o = lax.dot_general(p.astype(_BF), cav_ref[0].astype(_BF), (((1,), (1,)), ((), ())),
                        preferred_element_type=_F32)
    for j in range(t_new):
        o = o + p_new[j] * van_ref[0, 0, j:j + 1, :]
    oa_ref[0, 0] = o * (1.0 / l)
    _roll_in(cak_ref.at[0], ak8_ref, nak_ref.at[0], t_new)
    _roll_in(cav_ref.at[0], av8_ref, nav_ref.at[0], t_new)

    width_b = cbk_ref.shape[1]
    r0 = kb8_ref.shape[2] - t_new
    rows_b = t_new * SUBLANES
    sub = lax.broadcasted_iota(jnp.int32, (SUBLANES, width_b), 0)
    own = _div(lax.broadcasted_iota(jnp.int32, (SUBLANES, width_b), 1), HEAD_DIM) == sub
    qbd = jnp.concatenate([jnp.where(own, qb_ref[0, 0, i:i + 1, :], 0.0) for i in range(t_new)], axis=0)
    s_all = jnp.dot(qbd.astype(_BF), cbk_ref[0].astype(_BF), preferred_element_type=_F32)
    qi = _div(lax.broadcasted_iota(jnp.int32, (rows_b, 1), 0), SUBLANES)
    s_new = [jnp.sum(qbd * kb8_ref[0, 0, r0 + j:r0 + j + 1, :], axis=-1, keepdims=True) for j in range(t_new)]
    pats = []
    for window, dil in B_PATTERNS:
        lo_lane = max(wb - (-(-window // LANES) * LANES), 0)
        wl = lo_lane + lax.broadcasted_iota(jnp.int32, (1, wb - lo_lane), 1)
        rel = wb + qi - wl
        valid = (rel <= window) & _multiple(rel, dil)
        s = jnp.where(valid, s_all[:, lo_lane:], NEG)
        sn = [jnp.where((qi - j >= 0) & _multiple(qi - j, dil), s_new[j], NEG) for j in range(t_new)]
        m = functools.reduce(jnp.maximum, [jnp.max(s, axis=-1, keepdims=True)] + sn)
        p = jnp.exp(s - m)
        pn = [jnp.exp(z - m) for z in sn]
        l = functools.reduce(jnp.add, [jnp.sum(p, axis=-1, keepdims=True)] + pn)
        pats.append((lo_lane, p, pn, l, m + jnp.log(l)))
    mx = functools.reduce(jnp.maximum, [t[4] for t in pats])
    es = [jnp.exp(t[4] - mx) for t in pats]
    den = functools.reduce(jnp.add, es)
    coef = [e / (den * t[3]) for e, t in zip(es, pats)]
    starts = sorted({t[0] for t in pats} | {wb})
    blocks = []
    for a, b in zip(starts[:-1], starts[1:]):
        acc = None
        for c, (lo_lane, p, _, _, _) in zip(coef, pats):
            if lo_lane <= a:
                term = c * p[:, a - lo_lane:b - lo_lane]
                acc = term if acc is None else acc + term
        blocks.append(acc)
    lead = starts[0]
    p_comb = jnp.concatenate(blocks, axis=1) if len(blocks) > 1 else blocks[0]
    o_full = lax.dot_general(p_comb.astype(_BF), cbv_ref[0, :, lead:].astype(_BF), (((1,), (1,)), ((), ())),
                             preferred_element_type=_F32)
    for j in range(t_new):
        pj = functools.reduce(jnp.add, [c * t[2][j] for c, t in zip(coef, pats)])
        o_full = o_full + pj * vb8_ref[0, 0, r0 + j:r0 + j + 1, :]
    for i in range(t_new):
        blk = jnp.where(own, o_full[i * SUBLANES:(i + 1) * SUBLANES], 0.0)
        ob_ref[0, 0, i:i + 1, :] = jnp.sum(blk, axis=0, keepdims=True)
    _roll_in(cbk_ref.at[0], kb8_ref, nbk_ref, t_new)
    _roll_in(cbv_ref.at[0], vb8_ref, nbv_ref, t_new)


def _ffn_chunk(h, wg_ref, wu_ref, wd_ref, cs):
    gt = jnp.dot(h, wg_ref[:, cs], preferred_element_type=_F32)
    up = jnp.dot(h, wu_ref[:, cs], preferred_element_type=_F32)
    act = (gt * jax.nn.sigmoid(gt) * up).astype(_BF)
    return jnp.dot(act, wd_ref[cs, :], preferred_element_type=_F32)


def _attn_residual(x, oa, ob, gta, shf, scf, goa_ref, gob_ref, wo_ref, gf_ref):
    merged = jnp.concatenate([_rms(oa, goa_ref[...]), _rms(ob, gob_ref[...])], axis=1)
    x1 = x + gta * jnp.dot(merged.astype(_BF), wo_ref[...], preferred_element_type=_F32)
    return x1, (_rms(x1, gf_ref[...]) * (1.0 + scf) + shf).astype(_BF)


_TWO = lambda r: r[...].reshape(-1, r.shape[-1])
_FF_CUTS = (0, 384, 1152, 2048, 2816)
_PHASES = len(_FF_CUTS) - 1


def _ffn_and_sample_kernel(*refs, t_new):
    (x_ref, oa_ref, o1_ref, o4_ref, o16_ref, l1_ref, l2_ref, l3_ref, e_ref,
     gta_ref, shf_ref, scf_ref, gtf_ref, goa_ref, gob_ref, wo_ref, gf_ref, wg_ref, wu_ref, wd_ref, gfin_ref) = refs[:21]
    sample_in = refs[21:34]
    y_ref = refs[34]
    oa_s_ref, ob_s_ref, nak_ref, nav_ref, nbk_hbm, nbv_hbm = refs[35:41]
    s4_ref, s16_ref, x1_ref, h_ref, acc_ref, kbuf_ref, vbuf_ref, out_sem = refs[41:]
    step = pl.program_id(0)
    half = step % 2
    phase = step % _PHASES
    tm = x_ref.shape[1]

    slot = step % 2
    rows_b = kbuf_ref.shape[1]

    def out_copies(slot_, step_):
        dst = (step_ // 2, pl.ds(pl.multiple_of((step_ % 2) * rows_b, rows_b), rows_b), slice(None))
        return (pltpu.make_async_copy(kbuf_ref.at[slot_], nbk_hbm.at[dst], out_sem.at[0, slot_]),
                pltpu.make_async_copy(vbuf_ref.at[slot_], nbv_hbm.at[dst], out_sem.at[1, slot_]))

    @pl.when(step >= 2)
    def _():
        for cp in out_copies(slot, step - 2):
            cp.wait()

    sample_out = (oa_s_ref, ob_s_ref, nak_ref, nav_ref, kbuf_ref.at[slot], vbuf_ref.at[slot])

    def mix_and_project():
        for dil, src, stage in ((4, o4_ref, s4_ref), (16, o16_ref, s16_ref)):
            for r in range(dil):
                for c in range(B_W // LANES):
                    stage[c, pl.ds(r, tm // dil, stride=dil), :] = src[0, r, :, c * LANES:(c + 1) * LANES].astype(_F32)
        gather = lambda stage: jnp.concatenate([stage[c] for c in range(B_W // LANES)], axis=1)
        outs = [_TWO(o1_ref).astype(_F32), gather(s4_ref), gather(s16_ref)]
        lses = [_TWO(l1_ref), _TWO(l2_ref), _TWO(l3_ref)]
        mx = jnp.maximum(jnp.maximum(lses[0], lses[1]), lses[2])
        es = [jnp.exp(z - mx) for z in lses]
        den = es[0] + es[1] + es[2]
        ob = jnp.zeros((tm, B_W), _F32)
        for e, o in zip(es, outs):
            alpha = e / den
            hi = alpha.astype(_BF)
            lo = (alpha - hi.astype(_F32)).astype(_BF)
            wide = (jnp.dot(hi, e_ref[...], preferred_element_type=_F32)
                    + jnp.dot(lo, e_ref[...], preferred_element_type=_F32))
            ob = ob + wide * o
        x1, h = _attn_residual(_TWO(x_ref), _TWO(oa_ref).astype(_F32), ob, _TWO(gta_ref), _TWO(shf_ref), _TWO(scf_ref),
                               goa_ref, gob_ref, wo_ref, gf_ref)
        x1_ref[...] = x1
        h_ref[...] = h

    for k in range(_PHASES):
        @pl.when(phase == k)
        def _(k=k):
            _sample_half(half, tuple(sample_in) + tuple(sample_out), t_new=t_new)
            if k == 0:
                mix_and_project()
            part = _ffn_chunk(h_ref[...], wg_ref, wu_ref, wd_ref, slice(_FF_CUTS[k], _FF_CUTS[k + 1]))
            if k == 0:
                acc_ref[...] = part
            elif k < _PHASES - 1:
                acc_ref[...] += part
            else:
                x2 = x1_ref[...] + _TWO(gtf_ref) * (acc_ref[...] + part)
                y_ref[...] = _rms(x2, gfin_ref[...]).reshape(y_ref.shape)

    for cp in out_copies(slot, step):
        cp.start()

    @pl.when(step == pl.num_programs(0) - 1)
    def _():
        for cp in out_copies(1 - slot, step - 1) + out_copies(slot, step):
            cp.wait()


def _ffn_and_sample(x, oa, obs, lses, mods, goa, gob, wo, gf, wg, wu, wd, gfin, sinks, sample_ins, *, tm):
    b, s, d = x.shape
    n = sample_ins[0].shape[0]
    tiles = s // tm
    steps = 2 * n
    assert steps == b * tiles * _PHASES and _PHASES % 2 == 0 and wg.shape[1] == _FF_CUTS[-1]
    t_new = sample_ins[0].shape[2] // A_GROUP

    def tile(j):
        t = j // _PHASES
        return t // tiles, t % tiles

    row = lambda c: pl.BlockSpec((1, tm, c), lambda j: tile(j) + (0,))
    res = lambda dil: pl.BlockSpec((1, dil, tm // dil, B_W), lambda j: (tile(j)[0], 0, tile(j)[1], 0))
    mod_spec = pl.BlockSpec((1, 1, d), lambda j: (tile(j)[0], 0, 0))
    expand = (jnp.arange(B_W)[None, :] // HEAD_DIM == jnp.arange(B_HEADS)[:, None]).astype(_BF)
    ffn_ins = [x, oa] + list(obs) + list(lses) + [expand] + list(mods) + [goa, gob, wo, gf, wg, wu, wd, gfin]
    ffn_specs = ([row(d), row(A_QW)] + [res(o.shape[1]) for o in obs] + [row(B_HEADS)] * len(lses)
                 + [_resident(expand.shape)] + [mod_spec] * 4
                 + [_resident(a.shape) for a in (goa, gob, wo, gf, wg, wu, wd, gfin)])
    half_blk = lambda a: pl.BlockSpec((1, 1) + a.shape[2:], lambda j: (j // 2, j % 2) + (0,) * (a.ndim - 2))
    (qa_g, ka_n, va_n, ca_k, ca_v, ak8, av8, qb_h, kb8, vb8, cb_k, cb_v) = sample_ins
    sample_outs = [jax.ShapeDtypeStruct(a.shape, _F32) for a in (qa_g, qb_h, ca_k, ca_v, cb_k, cb_v)]
    cache_blk = lambda a: pl.BlockSpec((1, a.shape[1] // 2, a.shape[2]), lambda j: (j // 2, j % 2, 0))
    s_specs = [half_blk(qa_g), half_blk(ka_n), half_blk(va_n), cache_blk(ca_k), cache_blk(ca_v), half_blk(ak8),
               half_blk(av8), half_blk(qb_h), half_blk(kb8), half_blk(vb8), cache_blk(cb_k), cache_blk(cb_v)]
    in_hbm = pl.BlockSpec(memory_space=pl.ANY)
    o_specs = [half_blk(qa_g), half_blk(qb_h), cache_blk(ca_k), cache_blk(ca_v), in_hbm, in_hbm]
    stage = pltpu.VMEM((B_W // LANES, tm, LANES), _F32)
    out_buf = pltpu.VMEM((2, cb_k.shape[1] // 2, cb_k.shape[2]), _F32)
    outs = pl.pallas_call(
        functools.partial(_ffn_and_sample_kernel, t_new=t_new),
        grid=(steps,),
        in_specs=ffn_specs + [pl.BlockSpec(memory_space=pltpu.SMEM)] + s_specs,
        out_specs=[row(d)] + o_specs,
        out_shape=[jax.ShapeDtypeStruct((b, s, d), _F32)] + sample_outs,
        scratch_shapes=[stage, stage, pltpu.VMEM((tm, d), _F32), pltpu.VMEM((tm, d), _BF), pltpu.VMEM((tm, d), _F32),
                        out_buf, out_buf, pltpu.SemaphoreType.DMA((2, 2))],
        compiler_params=_params(("arbitrary",)),
        name="ffn_and_sample",
    )(*ffn_ins, sinks, *sample_ins)
    return outs


def _merge_ffn_sample_kernel(x_ref, oa_ref, ob_ref, gta_ref, shf_ref, scf_ref, gtf_ref,
                             goa_ref, gob_ref, wo_ref, gf_ref, wg_ref, wu_ref, wd_ref, gfin_ref, y_ref):
    x1, h = _attn_residual(_TWO(x_ref), _TWO(oa_ref), _TWO(ob_ref), gta_ref[...], shf_ref[...], scf_ref[...],
                           goa_ref, gob_ref, wo_ref, gf_ref)
    acc = jnp.zeros_like(x1)
    for k in range(_PHASES):
        acc = acc + _ffn_chunk(h, wg_ref, wu_ref, wd_ref, slice(_FF_CUTS[k], _FF_CUTS[k + 1]))
    y_ref[...] = _rms(x1 + gtf_ref[...] * acc, gfin_ref[...]).reshape(y_ref.shape)


def _merge_ffn_sample(x, oa, ob, mods, goa, gob, wo, gf, wg, wu, wd, gfin):
    t, n, d = x.shape
    row = lambda c: pl.BlockSpec((1, n, c), lambda j: (j, 0, 0))
    mod_spec = pl.BlockSpec((n, d), lambda j: (0, 0))
    weights = (goa, gob, wo, gf, wg, wu, wd, gfin)
    return pl.pallas_call(
        _merge_ffn_sample_kernel,
        grid=(t,),
        in_specs=[row(d), row(A_QW), row(B_W)] + [mod_spec] * 4 + [_resident(a.shape) for a in weights],
        out_specs=row(d),
        out_shape=jax.ShapeDtypeStruct((t, n, d), _F32),
        compiler_params=_params(("parallel",)),
        name="merge_ffn_sample",
    )(x, oa, ob, *mods, *weights)


def _rope_tables(pos):
    half = HEAD_DIM // 2
    inv = jnp.exp(-math.log(ROPE_THETA) * jnp.arange(half, dtype=_F32) * (2.0 / HEAD_DIM))
    ang = pos.astype(_F32)[:, None] * inv[None, :]
    cos, sin = jnp.cos(ang), jnp.sin(ang)
    return jnp.tile(cos, (1, LANES // half)), jnp.tile(jnp.concatenate([-sin, sin], axis=1), (1, LANES // HEAD_DIM))


def _window_on_lanes(cache):
    _, n, w, h, dh = cache.shape
    return cache.transpose(0, 1, 3, 4, 2).reshape(n, h * dh, w)


def _window_off_lanes(t, h):
    n, _, w = t.shape
    return t.reshape(1, n, h, HEAD_DIM, w).transpose(0, 1, 4, 2, 3)


def _halves_last_rows(new_t, pad_to):
    t, n, c = new_t.shape
    v = new_t.reshape(t, n, 2, c // 2).transpose(1, 2, 0, 3)
    return jnp.pad(v, ((0, 0), (0, 0), (SUBLANES - t, 0), (0, pad_to - c // 2)))


def kernel(x_prompt, x_sample, c_prompt, c_sample, cache_a_k, cache_a_v, cache_b_k, cache_b_v, w_ada, b_ada, g_attn, w_in, sinks, g_out_a, g_out_b, w_o, g_ffn, w_gate, w_up, w_down, g_final):
    nb, s, d = x_prompt.shape
    ns, t_new, _ = x_sample.shape
    assert w_ada.shape[0] == 1, "single trunk layer"
    l = 0
    bf = lambda w: w.astype(_BF)
    row = lambda g: g.reshape(1, -1)

    c_all = jnp.concatenate([c_prompt, c_sample], axis=0)
    c_all = jnp.pad(c_all, ((0, (-c_all.shape[0]) % SUBLANES), (0, 0)))
    mod = _adaln(c_all, w_ada[l], b_ada[l].reshape(1, -1))
    mod_p = [m.reshape(nb, 1, d) for m in jnp.split(mod[:nb], 6, axis=-1)]
    mod_s = jnp.split(mod[nb:nb + ns], 6, axis=-1)

    cos_p, sin_p = _rope_tables(jnp.arange(s, dtype=jnp.int32))
    cos_s, sin_s = _rope_tables(PAST_LEN + jnp.arange(t_new, dtype=jnp.int32))
    w_in_bf = bf(w_in[l])
    ffn_w = (row(g_out_a[l]), row(g_out_b[l]), bf(w_o[l]), row(g_ffn[l]), bf(w_gate[l]), bf(w_up[l]), bf(w_down[l]),
             row(g_final))

    wa_p, wb_p = min(A_WINDOW, s), min(B_WINDOW, s)
    (qa, ka, va, q1, k1, v1, q4, k4, v4, q16, k16, v16, ka_t, va_t, kb_t, vb_t) = _inproj_prompt(
        x_prompt, mod_p[0], mod_p[1], row(g_attn[l]), w_in_bf, cos_p, sin_p, tm=512, tail_a=wa_p, tail_b=wb_p)
    xs_t = x_sample.transpose(1, 0, 2)
    qa_s, ka_s, va_s, qb_s, kb_s, vb_s = _inproj_sample(
        xs_t, mod_s[0], mod_s[1], row(g_attn[l]), w_in_bf, cos_s.reshape(t_new, 1, LANES),
        sin_s.reshape(t_new, 1, LANES))

    oa = _swa_prompt(qa, ka, va, sinks[l])
    obs, lses = [], []
    for (q, k, v), (_, dil) in zip(((q1, k1, v1), (q4, k4, v4), (q16, k16, v16)), B_PATTERNS):
        o, lse = _band_attn(q, k, v, dil)
        obs.append(o)
        lses.append(lse.transpose(0, 2, 1, 3).reshape(nb, s, B_HEADS))

    qa_g = (qa_s.reshape(t_new, ns, A_KV_HEADS, A_GROUP, HEAD_DIM).transpose(1, 2, 0, 3, 4)
            .reshape(ns, A_KV_HEADS, t_new * A_GROUP, HEAD_DIM))
    kv_g = lambda t: t.reshape(t_new, ns, A_KV_HEADS, HEAD_DIM).transpose(1, 2, 0, 3)
    qb_h = qb_s.reshape(t_new, ns, 2, B_W // 2).transpose(1, 2, 0, 3)
    sample_ins = (qa_g, kv_g(ka_s), kv_g(va_s), _window_on_lanes(cache_a_k), _window_on_lanes(cache_a_v),
                  _halves_last_rows(ka_s, LANES), _halves_last_rows(va_s, LANES), qb_h,
                  _halves_last_rows(kb_s, B_W // 2), _halves_last_rows(vb_s, B_W // 2),
                  _window_on_lanes(cache_b_k), _window_on_lanes(cache_b_v))
    y_prompt, oa_g, ob_h, na_k, na_v, nb_k, nb_v = _ffn_and_sample(
        x_prompt, oa, obs, lses, mod_p[2:6], *ffn_w, sinks[l], sample_ins, tm=256)
    pa_k = ka_t.reshape(1, nb, wa_p, A_KV_HEADS, HEAD_DIM)
    pa_v = va_t.reshape(1, nb, wa_p, A_KV_HEADS, HEAD_DIM)
    pb_k = kb_t.reshape(1, nb, wb_p, B_HEADS, HEAD_DIM)
    pb_v = vb_t.reshape(1, nb, wb_p, B_HEADS, HEAD_DIM)

    oa_t = (oa_g.reshape(ns, A_KV_HEADS, t_new, A_GROUP, HEAD_DIM).transpose(2, 0, 1, 3, 4)
            .reshape(t_new, ns, A_QW))
    ob_t = ob_h.transpose(2, 0, 1, 3).reshape(t_new, ns, B_W)
    y_s = _merge_ffn_sample(xs_t, oa_t, ob_t, mod_s[2:6], *ffn_w)
    y_sample = y_s.transpose(1, 0, 2)
    sa_k, sa_v = _window_off_lanes(na_k, A_KV_HEADS), _window_off_lanes(na_v, A_KV_HEADS)
    sb_k, sb_v = _window_off_lanes(nb_k, B_HEADS), _window_off_lanes(nb_v, B_HEADS)

    return (y_prompt, y_sample, pa_k, pa_v, pb_k, pb_v, sa_k, sa_v, sb_k, sb_v)
```

```python
import functools
import math

import jax
import jax.numpy as jnp
from jax import lax
from jax.experimental import pallas as pl
from jax.experimental.pallas import tpu as pltpu

HEAD_DIM = 64
A_Q_HEADS = 8
A_KV_HEADS = 2
A_GROUP = A_Q_HEADS // A_KV_HEADS
B_HEADS = 8
A_WINDOW = 128
BLOCK = 128
B_PATTERNS = ((128, 1), (512, 4), (2048, 16))
B_WINDOW = 2048
PAST_LEN = 8192
ROPE_THETA = 10000.0
EPS = 1e-6
A_QW = A_Q_HEADS * HEAD_DIM
A_KVW = A_KV_HEADS * HEAD_DIM
B_W = B_HEADS * HEAD_DIM
SCALE = HEAD_DIM ** -0.5
LANES = 128
SUBLANES = 8
NEG = -1e30
VMEM_LIMIT = 56 * 1024 * 1024
Q_BLOCKS_B = 8
Q_BLOCKS_A = 4

_BF = jnp.bfloat16
_F32 = jnp.float32


def _params(sem, vmem=VMEM_LIMIT):
    return pltpu.CompilerParams(dimension_semantics=sem, vmem_limit_bytes=vmem)


def _resident(shape):
    nd = len(shape)
    return pl.BlockSpec(shape, lambda *_: (0,) * nd, pipeline_mode=pl.Buffered(1))


def _rms(x, g):
    return x * lax.rsqrt(jnp.mean(x * x, axis=-1, keepdims=True) + EPS) * g


def _log2(n):
    assert n > 0 and n & (n - 1) == 0, "power of two expected"
    return n.bit_length() - 1


def _div(x, n):
    return x >> _log2(n)


def _multiple(x, n):
    _log2(n)
    return (x & (n - 1)) == 0


def _adaln_kernel(c_ref, w_ref, b_ref, o_ref):
    c = c_ref[...]
    a = (c * jax.nn.sigmoid(c)).astype(_BF)
    o_ref[...] = jnp.dot(a, w_ref[...].astype(_BF), preferred_element_type=_F32) + b_ref[...]


def _adaln(c_all, w, b):
    m, d = c_all.shape
    n = w.shape[1]
    tn = 768
    return pl.pallas_call(
        _adaln_kernel,
        grid=(n // tn,),
        in_specs=[pl.BlockSpec((m, d), lambda j: (0, 0)),
                  pl.BlockSpec((d, tn), lambda j: (0, j)),
                  pl.BlockSpec((1, tn), lambda j: (0, j))],
        out_specs=pl.BlockSpec((m, tn), lambda j: (0, j)),
        out_shape=jax.ShapeDtypeStruct((m, n), _F32),
        compiler_params=_params(("arbitrary",)),
        name="adaln",
    )(c_all, w, b)


def _rope(x, cos, sin_signed, first_half):
    sw = jnp.where(first_half, pltpu.roll(x, 96, 1), pltpu.roll(x, 32, 1))
    return x * cos + sw * sin_signed


def _project(x_ref, sh_ref, sc_ref, g_ref, w_ref, cos_ref, sin_ref):
    d = x_ref.shape[-1]
    x = x_ref[...].reshape(-1, d)
    sh = sh_ref[...].reshape(-1, d)
    sc = sc_ref[...].reshape(-1, d)
    h = _rms(x, g_ref[...]) * (1.0 + sc) + sh
    proj = jnp.dot(h.astype(_BF), w_ref[...], preferred_element_type=_F32)
    cos = cos_ref[...].reshape(-1, LANES)
    sin = sin_ref[...].reshape(-1, LANES)
    lane = lax.broadcasted_iota(jnp.int32, (1, LANES), 1)
    first_half = (lane & (HEAD_DIM - 1)) < (HEAD_DIM // 2)

    def rot(col0, ncols):
        parts = [_rope(proj[:, col0 + j * LANES: col0 + (j + 1) * LANES], cos, sin, first_half)
                 for j in range(ncols // LANES)]
        return parts[0] if len(parts) == 1 else jnp.concatenate(parts, axis=1)

    o = 0
    qa = rot(o, A_QW) * SCALE; o += A_QW
    ka = rot(o, A_KVW); o += A_KVW
    va = proj[:, o:o + A_KVW]; o += A_KVW
    qb = rot(o, B_W) * SCALE; o += B_W
    kb = rot(o, B_W); o += B_W
    vb = proj[:, o:o + B_W]
    return qa, ka, va, qb, kb, vb


def _put(ref, val):
    ref[...] = val.astype(ref.dtype).reshape(ref.shape)


def _inproj_prompt_kernel(x_ref, sh_ref, sc_ref, g_ref, w_ref, cos_ref, sin_ref,
                          qa_ref, ka_ref, va_ref, q1_ref, k1_ref, v1_ref, q4_ref, k4_ref, v4_ref,
                          q16_ref, k16_ref, v16_ref, kat_ref, vat_ref, kbt_ref, vbt_ref,
                          sq_ref, sk_ref, sv_ref):
    qa, ka, va, qb, kb, vb = _project(x_ref, sh_ref, sc_ref, g_ref, w_ref, cos_ref, sin_ref)
    tm = qa.shape[0]
    _put(qa_ref, qa); _put(ka_ref, ka); _put(va_ref, va)
    _put(q1_ref, qb); _put(k1_ref, kb); _put(v1_ref, vb)
    ta = kat_ref.shape[1]
    _put(kat_ref, ka[tm - ta:]); _put(vat_ref, va[tm - ta:])
    _put(kbt_ref, kb); _put(vbt_ref, vb)
    for val, stage, r4, r16 in ((qb, sq_ref, q4_ref, q16_ref), (kb, sk_ref, k4_ref, k16_ref),
                                (vb, sv_ref, v4_ref, v16_ref)):
        for c in range(B_W // LANES):
            stage[c] = val[:, c * LANES:(c + 1) * LANES]
        for dil, out in ((4, r4), (16, r16)):
            n = tm // dil
            for r in range(dil):
                for c in range(B_W // LANES):
                    out[0, r, :, c * LANES:(c + 1) * LANES] = stage[c, pl.ds(r, n, stride=dil), :].astype(out.dtype)


def _inproj_prompt(x, sh, sc, g, w_bf, cos, sin, *, tm, tail_a, tail_b):
    b, s, d = x.shape
    nt = s // tm
    assert tail_b % tm == 0 and tail_a <= tm
    row = lambda c: pl.BlockSpec((1, tm, c), lambda bb, i: (bb, i, 0))
    res = lambda dil: pl.BlockSpec((1, dil, tm // dil, B_W), lambda bb, i: (bb, 0, i, 0))
    mod_spec = pl.BlockSpec((1, 1, d), lambda bb, i: (bb, 0, 0))
    tab_spec = pl.BlockSpec((tm, LANES), lambda bb, i: (i, 0))
    tail_a_spec = pl.BlockSpec((1, tail_a, A_KVW), lambda bb, i: (bb, 0, 0))
    tail_b_spec = pl.BlockSpec((1, tm, B_W), lambda bb, i: (bb, jnp.maximum(i - (nt - tail_b // tm), 0), 0))
    shp = lambda c, dt: jax.ShapeDtypeStruct((b, s, c), dt)
    rshp = lambda dil: jax.ShapeDtypeStruct((b, dil, s // dil, B_W), _BF)
    return pl.pallas_call(
        _inproj_prompt_kernel,
        grid=(b, nt),
        in_specs=[row(d), mod_spec, mod_spec, _resident((1, d)), _resident(w_bf.shape), tab_spec, tab_spec],
        out_specs=[row(A_QW), row(A_KVW), row(A_KVW)] + [res(1)] * 3 + [res(4)] * 3 + [res(16)] * 3
                  + [tail_a_spec] * 2 + [tail_b_spec] * 2,
        out_shape=[shp(A_QW, _BF), shp(A_KVW, _BF), shp(A_KVW, _BF)] + [rshp(1)] * 3 + [rshp(4)] * 3 + [rshp(16)] * 3
                  + [jax.ShapeDtypeStruct((b, tail_a, A_KVW), _F32)] * 2
                  + [jax.ShapeDtypeStruct((b, tail_b, B_W), _F32)] * 2,
        scratch_shapes=[pltpu.VMEM((B_W // LANES, tm, LANES), _F32)] * 3,
        compiler_params=_params(("parallel", "arbitrary")),
        name="in_proj_prompt",
    )(x, sh, sc, g, w_bf, cos, sin)


def _inproj_sample_kernel(x_ref, sh_ref, sc_ref, g_ref, w_ref, cos_ref, sin_ref,
                          qa_ref, ka_ref, va_ref, qb_ref, kb_ref, vb_ref):
    vals = _project(x_ref, sh_ref, sc_ref, g_ref, w_ref, cos_ref, sin_ref)
    for ref, val in zip((qa_ref, ka_ref, va_ref, qb_ref, kb_ref, vb_ref), vals):
        _put(ref, val)


def _inproj_sample(x, sh, sc, g, w_bf, cos, sin):
    t, n, d = x.shape
    row = lambda c: pl.BlockSpec((1, n, c), lambda j: (j, 0, 0))
    mod_spec = pl.BlockSpec((n, d), lambda j: (0, 0))
    tab_spec = pl.BlockSpec((1, 1, LANES), lambda j: (j, 0, 0))
    widths = (A_QW, A_KVW, A_KVW, B_W, B_W, B_W)
    return pl.pallas_call(
        _inproj_sample_kernel,
        grid=(t,),
        in_specs=[row(d), mod_spec, mod_spec, _resident((1, d)), _resident(w_bf.shape), tab_spec, tab_spec],
        out_specs=[row(c) for c in widths],
        out_shape=[jax.ShapeDtypeStruct((t, n, c), _F32) for c in widths],
        compiler_params=_params(("parallel",)),
        name="in_proj_sample",
    )(x, sh, sc, g, w_bf, cos, sin)


def _band_bias(max_dist, drop_prev):
    qi = lax.broadcasted_iota(jnp.int32, (BLOCK, 2 * BLOCK), 0)
    ki = lax.broadcasted_iota(jnp.int32, (BLOCK, 2 * BLOCK), 1)
    rel = qi + BLOCK - ki
    valid = (rel >= 0) & (rel <= max_dist)
    if drop_prev is not None:
        valid = valid & ((ki >= BLOCK) | jnp.logical_not(drop_prev))
    return jnp.where(valid, 0.0, NEG).astype(_F32)


def _kv_window(prev_ref, cur_ref, j, lead, cols):
    if j == 0:
        return jnp.concatenate([prev_ref[lead + (slice(None), cols)], cur_ref[lead + (slice(0, BLOCK), cols)]], axis=0)
    return cur_ref[lead + (slice((j - 1) * BLOCK, (j + 1) * BLOCK), cols)]


def _stat_columns(stats, cols, acc):
    lane = lax.broadcasted_iota(jnp.int32, (1, acc.shape[1]), 1)
    for col, stat in zip(cols, stats):
        acc = jnp.where(lane == col, stat, acc)
    return acc


def _band_attn_kernel(q_ref, kp_ref, kc_ref, vp_ref, vc_ref, o_ref, ml_ref, *, max_dist, nq):
    first_step = pl.program_id(2) == 0
    bias = _band_bias(max_dist, None)
    bias0 = _band_bias(max_dist, first_step)
    lane = lax.broadcasted_iota(jnp.int32, (1, LANES), 1)
    lo = lane < HEAD_DIM
    zero = jnp.zeros((), _BF)
    for j in range(nq):
        rows = slice(j * BLOCK, (j + 1) * BLOCK)
        bj = bias0 if j == 0 else bias
        bj = jnp.concatenate([bj, bj], axis=0)
        ml_all = jnp.zeros((BLOCK, 2 * B_HEADS), _F32)
        for c in range(B_W // LANES):
            cs = slice(c * LANES, (c + 1) * LANES)
            q2 = q_ref[0, 0, rows, cs]
            k2 = _kv_window(kp_ref, kc_ref, j, (0, 0), cs)
            v2 = _kv_window(vp_ref, vc_ref, j, (0, 0), cs)
            qs = jnp.concatenate([jnp.where(lo, q2, zero), jnp.where(lo, zero, q2)], axis=0)
            s = lax.dot_general(qs, k2, (((1,), (1,)), ((), ())), preferred_element_type=_F32) + bj
            m = jnp.max(s, axis=-1, keepdims=True)
            p = jnp.exp(s - m)
            l = jnp.sum(p, axis=-1, keepdims=True)
            pv = jnp.dot(p.astype(_BF), v2, preferred_element_type=_F32)
            o_ref[0, 0, rows, cs] = jnp.where(lo, pv[:BLOCK], pv[BLOCK:]).astype(o_ref.dtype)
            ml_all = _stat_columns((m[:BLOCK], m[BLOCK:], l[:BLOCK], l[BLOCK:]),
                                   (2 * c, 2 * c + 1, B_HEADS + 2 * c, B_HEADS + 2 * c + 1), ml_all)
        ml_ref[0, 0, rows, :] = ml_all


def _band_attn(q, k, v, dil):
    b, _, m, w = q.shape
    nq = min(Q_BLOCKS_B, m // BLOCK)
    tq = nq * BLOCK
    cur = pl.BlockSpec((1, 1, tq, w), lambda bb, r, i: (bb, r, i, 0))
    prev = pl.BlockSpec((1, 1, BLOCK, w), lambda bb, r, i: (bb, r, jnp.maximum(i * nq - 1, 0), 0))
    window, _ = [p for p in B_PATTERNS if p[1] == dil][0]
    return pl.pallas_call(
        functools.partial(_band_attn_kernel, max_dist=window // dil, nq=nq),
        grid=(b, dil, m // tq),
        in_specs=[cur, prev, cur, prev, cur],
        out_specs=[cur, pl.BlockSpec((1, 1, tq, 2 * B_HEADS), lambda bb, r, i: (bb, r, i, 0))],
        out_shape=[jax.ShapeDtypeStruct((b, dil, m, w), _BF),
                   jax.ShapeDtypeStruct((b, dil, m, 2 * B_HEADS), _F32)],
        compiler_params=_params(("parallel", "parallel", "arbitrary")),
        name=f"band_attn_d{dil}",
    )(q, k, k, v, v)


def _swa_kernel(sink_ref, q_ref, kp_ref, kc_ref, vp_ref, vc_ref, o_ref, l_ref, *, nq):
    first_step = pl.program_id(1) == 0
    bias = _band_bias(A_WINDOW - 1, None)
    bias0 = _band_bias(A_WINDOW - 1, first_step)
    lane = lax.broadcasted_iota(jnp.int32, (1, LANES), 1)
    lo = lane < HEAD_DIM
    col0 = lax.broadcasted_iota(jnp.int32, (1, 2 * BLOCK), 1) == 0
    for j in range(nq):
        rows = slice(j * BLOCK, (j + 1) * BLOCK)
        bj = bias0 if j == 0 else bias
        bj = jnp.concatenate([bj, bj], axis=0)
        k2 = _kv_window(kp_ref, kc_ref, j, (0,), slice(None))
        v2 = _kv_window(vp_ref, vc_ref, j, (0,), slice(None))
        l_all = jnp.zeros((BLOCK, A_Q_HEADS), _F32)
        for c in range(A_QW // LANES):
            g = (2 * c) // A_GROUP
            keep = lo if g == 0 else jnp.logical_not(lo)
            qc = q_ref[0, rows, c * LANES:(c + 1) * LANES].astype(_F32)
            qr = pltpu.roll(qc, HEAD_DIM, 1)
            tiles = [jnp.where(keep, qc if half == g else qr, 0.0).astype(_BF) for half in range(2)]
            qs = jnp.concatenate(tiles, axis=0)
            s = lax.dot_general(qs, k2, (((1,), (1,)), ((), ())), preferred_element_type=_F32) + bj
            s = jnp.concatenate([jnp.where(col0, sink_ref[2 * c + half], s[half * BLOCK:(half + 1) * BLOCK])
                                 for half in range(2)], axis=0)
            m = jnp.max(s, axis=-1, keepdims=True)
            p = jnp.exp(s - m)
            l = jnp.sum(p, axis=-1, keepdims=True)
            p = jnp.where(col0, 0.0, p).astype(_BF)
            pv = jnp.dot(p, v2, preferred_element_type=_F32)
            halves = [pv[half * BLOCK:(half + 1) * BLOCK] for half in range(2)]
            halves = [t if half == g else pltpu.roll(t, HEAD_DIM, 1) for half, t in enumerate(halves)]
            _put(o_ref.at[0, rows, c * LANES:(c + 1) * LANES], jnp.where(lo, halves[0], halves[1]))
            l_all = _stat_columns((l[:BLOCK], l[BLOCK:]), (2 * c, 2 * c + 1), l_all)
        l_ref[0, rows, :] = l_all


def _swa_prompt(q, k, v, sinks):
    b, s, _ = q.shape
    nq = Q_BLOCKS_A
    tq = nq * BLOCK
    assert A_WINDOW <= BLOCK, "the sink logit borrows a key column that no query row can reach"
    qspec = pl.BlockSpec((1, tq, A_QW), lambda bb, i: (bb, i, 0))
    cur = pl.BlockSpec((1, tq, A_KVW), lambda bb, i: (bb, i, 0))
    prev = pl.BlockSpec((1, BLOCK, A_KVW), lambda bb, i: (bb, jnp.maximum(i * nq - 1, 0), 0))
    return pl.pallas_call(
        functools.partial(_swa_kernel, nq=nq),
        grid=(b, s // tq),
        in_specs=[pl.BlockSpec(memory_space=pltpu.SMEM), qspec, prev, cur, prev, cur],
        out_specs=[qspec, pl.BlockSpec((1, tq, A_Q_HEADS), lambda bb, i: (bb, i, 0))],
        out_shape=[jax.ShapeDtypeStruct((b, s, A_QW), _BF), jax.ShapeDtypeStruct((b, s, A_Q_HEADS), _F32)],
        compiler_params=_params(("parallel", "arbitrary")),
        name="swa_prompt",
    )(sinks, q, k, k, v, v)


def _roll_in(x_ref, new8_ref, o_ref, t_new):
    rows, width = x_ref.shape
    new8 = new8_ref[0, 0]
    tail = jnp.concatenate([jnp.zeros((LANES - new8.shape[0], new8.shape[1]), _F32), new8], axis=0).T[:rows]
    lane = lax.broadcasted_iota(jnp.int32, (1, LANES), 1)
    keep = lane < LANES - t_new
    nxt = pltpu.roll(x_ref[:, 0:LANES], LANES - t_new, 1)
    for c in range(width // LANES):
        cur = nxt
        if c + 1 < width // LANES:
            nxt = pltpu.roll(x_ref[:, (c + 1) * LANES:(c + 2) * LANES], LANES - t_new, 1)
        else:
            nxt = tail
        o_ref[:, c * LANES:(c + 1) * LANES] = jnp.where(keep, cur, nxt)


def _sample_half(g, refs, *, t_new):
    (sink_ref, qa_ref, kan_ref, van_ref, cak_ref, cav_ref, ak8_ref, av8_ref,
     qb_ref, kb8_ref, vb8_ref, cbk_ref, cbv_ref,
     oa_ref, ob_ref, nak_ref, nav_ref, nbk_ref, nbv_ref) = refs
    wa, wb = cak_ref.shape[-1], cbk_ref.shape[-1]

    rows_a = t_new * A_GROUP
    ra = lax.broadcasted_iota(jnp.int32, (rows_a, 1), 0)
    qi_a = _div(ra, A_GROUP)
    rel_a = wa + qi_a - lax.broadcasted_iota(jnp.int32, (1, wa), 1)
    valid_a = (rel_a >= 0) & (rel_a < A_WINDOW)
    q = qa_ref[0, 0]
    s = jnp.dot(q.astype(_BF), cak_ref[0].astype(_BF), preferred_element_type=_F32)
    s = jnp.where(valid_a, s, NEG)
    s_new = [jnp.where(qi_a >= j, jnp.sum(q * kan_ref[0, 0, j:j + 1, :], axis=-1, keepdims=True), NEG)
             for j in range(t_new)]
    sink = jnp.zeros((rows_a, 1), _F32)
    for u in range(A_GROUP):
        sink = jnp.where((ra & (A_GROUP - 1)) == u, sink_ref[g * A_GROUP + u], sink)
    m = functools.reduce(jnp.maximum, [jnp.max(s, axis=-1, keepdims=True), sink] + s_new)
    p = jnp.exp(s - m)
    p_new = [jnp.exp(z - m) for z in s_new]
    l = functools.reduce(jnp.add, [jnp.sum(p, axis=-1, keepdims=True), jnp.exp(sink - m)] + p_new)
    o = lax.dot_general(p.astype(_BF), cav_ref[0].astype(_BF), (((1,), (1,)), ((), ())),
                        preferred_element_type=_F32)
    for j in range(t_new):
        o = o + p_new[j] * van_ref[0, 0, j:j + 1, :]
    oa_ref[0, 0] = o * (1.0 / l)
    _roll_in(cak_ref.at[0], ak8_ref, nak_ref.at[0], t_new)
    _roll_in(cav_ref.at[0], av8_ref, nav_ref.at[0], t_new)

    width_b = cbk_ref.shape[1]
    r0 = kb8_ref.shape[2] - t_new
    rows_b = t_new * SUBLANES
    sub = lax.broadcasted_iota(jnp.int32, (SUBLANES, width_b), 0)
    own = _div(lax.broadcasted_iota(jnp.int32, (SUBLANES, width_b), 1), HEAD_DIM) == sub
    qbd = jnp.concatenate([jnp.where(own, qb_ref[0, 0, i:i + 1, :], 0.0) for i in range(t_new)], axis=0)
    s_all = jnp.dot(qbd.astype(_BF), cbk_ref[0].astype(_BF), preferred_element_type=_F32)
    qi = _div(lax.broadcasted_iota(jnp.int32, (rows_b, 1), 0), SUBLANES)
    s_new = [jnp.sum(qbd * kb8_ref[0, 0, r0 + j:r0 + j + 1, :], axis=-1, keepdims=True) for j in range(t_new)]
    pats = []
    for window, dil in B_PATTERNS:
        lo_lane = max(wb - (-(-window // LANES) * LANES), 0)
        wl = lo_lane + lax.broadcasted_iota(jnp.int32, (1, wb - lo_lane), 1)
        rel = wb + qi - wl
        valid = (rel <= window) & _multiple(rel, dil)
        s = jnp.where(valid, s_all[:, lo_lane:], NEG)
        sn = [jnp.where((qi - j >= 0) & _multiple(qi - j, dil), s_new[j], NEG) for j in range(t_new)]
        m = functools.reduce(jnp.maximum, [jnp.max(s, axis=-1, keepdims=True)] + sn)
        p = jnp.exp(s - m)
        pn = [jnp.exp(z - m) for z in sn]
        l = functools.reduce(jnp.add, [jnp.sum(p, axis=-1, keepdims=True)] + pn)
        pats.append((lo_lane, p, pn, l, m + jnp.log(l)))
    mx = functools.reduce(jnp.maximum, [t[4] for t in pats])
    es = [jnp.exp(t[4] - mx) for t in pats]
    den = functools.reduce(jnp.add, es)
    coef = [e / (den * t[3]) for e, t in zip(es, pats)]
    starts = sorted({t[0] for t in pats} | {wb})
    blocks = []
    for a, b in zip(starts[:-1], starts[1:]):
        acc = None
        for c, (lo_lane, p, _, _, _) in zip(coef, pats):
            if lo_lane <= a:
                term = c * p[:, a - lo_lane:b - lo_lane]
                acc = term if acc is None else acc + term
        blocks.append(acc)
    lead = starts[0]
    p_comb = jnp.concatenate(blocks, axis=1) if len(blocks) > 1 else blocks[0]
    o_full = lax.dot_general(p_comb.astype(_BF), cbv_ref[0, :, lead:].astype(_BF), (((1,), (1,)), ((), ())),
                             preferred_element_type=_F32)
    for j in range(t_new):
        pj = functools.reduce(jnp.add, [c * t[2][j] for c, t in zip(coef, pats)])
        o_full = o_full + pj * vb8_ref[0, 0, r0 + j:r0 + j + 1, :]
    for i in range(t_new):
        blk = jnp.where(own, o_full[i * SUBLANES:(i + 1) * SUBLANES], 0.0)
        ob_ref[0, 0, i:i + 1, :] = jnp.sum(blk, axis=0, keepdims=True)
    _roll_in(cbk_ref.at[0], kb8_ref, nbk_ref, t_new)
    _roll_in(cbv_ref.at[0], vb8_ref, nbv_ref, t_new)


def _ffn_chunk(h, wg_ref, wu_ref, wd_ref, cs):
    gt = jnp.dot(h, wg_ref[:, cs], preferred_element_type=_F32)
    up = jnp.dot(h, wu_ref[:, cs], preferred_element_type=_F32)
    act = (gt * jax.nn.sigmoid(gt) * up).astype(_BF)
    return jnp.dot(act, wd_ref[cs, :], preferred_element_type=_F32)


def _attn_residual(x, oa, ob, gta, shf, scf, goa_ref, gob_ref, wo_ref, gf_ref):
    merged = jnp.concatenate([_rms(oa, goa_ref[...]), _rms(ob, gob_ref[...])], axis=1)
    x1 = x + gta * jnp.dot(merged.astype(_BF), wo_ref[...], preferred_element_type=_F32)
    return x1, (_rms(x1, gf_ref[...]) * (1.0 + scf) + shf).astype(_BF)


_TWO = lambda r: r[...].reshape(-1, r.shape[-1])
_FF_CUTS = (0, 384, 1152, 2048, 2816)
_PHASES = len(_FF_CUTS) - 1


def _ffn_and_sample_kernel(*refs, t_new):
    (x_ref, oa_ref, la_ref, o1_ref, o4_ref, o16_ref, ml1_ref, ml2_ref, ml3_ref, e_ref,
     gta_ref, shf_ref, scf_ref, gtf_ref, goa_ref, gob_ref, wo_ref, gf_ref, wg_ref, wu_ref, wd_ref, gfin_ref) = refs[:22]
    sample_in = refs[22:35]
    y_ref = refs[35]
    oa_s_ref, ob_s_ref, nak_ref, nav_ref, nbk_hbm, nbv_hbm = refs[36:42]
    s4_ref, s16_ref, x1_ref, h_ref, acc_ref, kbuf_ref, vbuf_ref, out_sem = refs[42:]
    step = pl.program_id(0)
    half = step % 2
    phase = step % _PHASES
    tm = x_ref.shape[1]

    slot = step % 2
    rows_b = kbuf_ref.shape[1]

    def out_copies(slot_, step_):
        dst = (step_ // 2, pl.ds(pl.multiple_of((step_ % 2) * rows_b, rows_b), rows_b), slice(None))
        return (pltpu.make_async_copy(kbuf_ref.at[slot_], nbk_hbm.at[dst], out_sem.at[0, slot_]),
                pltpu.make_async_copy(vbuf_ref.at[slot_], nbv_hbm.at[dst], out_sem.at[1, slot_]))

    @pl.when(step >= 2)
    def _():
        for cp in out_copies(slot, step - 2):
            cp.wait()

    sample_out = (oa_s_ref, ob_s_ref, nak_ref, nav_ref, kbuf_ref.at[slot], vbuf_ref.at[slot])

    def mix_and_project():
        for dil, src, stage in ((4, o4_ref, s4_ref), (16, o16_ref, s16_ref)):
            for r in range(dil):
                for c in range(B_W // LANES):
                    stage[c, pl.ds(r, tm // dil, stride=dil), :] = src[0, r, :, c * LANES:(c + 1) * LANES].astype(_F32)
        gather = lambda stage: jnp.concatenate([stage[c] for c in range(B_W // LANES)], axis=1)
        outs = [_TWO(o1_ref).astype(_F32), gather(s4_ref), gather(s16_ref)]

        def widen(per_head):
            hi = per_head.astype(_BF)
            lo = (per_head - hi.astype(_F32)).astype(_BF)
            return (jnp.dot(hi, e_ref[...], preferred_element_type=_F32)
                    + jnp.dot(lo, e_ref[...], preferred_element_type=_F32))

        ms = [r[0, :, 0:B_HEADS] for r in (ml1_ref, ml2_ref, ml3_ref)]
        ls = [r[0, :, B_HEADS:2 * B_HEADS] for r in (ml1_ref, ml2_ref, ml3_ref)]
        lses = [m + jnp.log(l) for m, l in zip(ms, ls)]
        mx = jnp.maximum(jnp.maximum(lses[0], lses[1]), lses[2])
        es = [jnp.exp(z - mx) for z in lses]
        den = es[0] + es[1] + es[2]
        ob = jnp.zeros((tm, B_W), _F32)
        for e, l, o in zip(es, ls, outs):
            ob = ob + widen(e / (den * l)) * o
        oa = widen(1.0 / _TWO(la_ref)) * _TWO(oa_ref).astype(_F32)
        x1, h = _attn_residual(_TWO(x_ref), oa, ob, _TWO(gta_ref), _TWO(shf_ref), _TWO(scf_ref),
                               goa_ref, gob_ref, wo_ref, gf_ref)
        x1_ref[...] = x1
        h_ref[...] = h

    for k in range(_PHASES):
        @pl.when(phase == k)
        def _(k=k):
            _sample_half(half, tuple(sample_in) + tuple(sample_out), t_new=t_new)
            if k == 0:
                mix_and_project()
            part = _ffn_chunk(h_ref[...], wg_ref, wu_ref, wd_ref, slice(_FF_CUTS[k], _FF_CUTS[k + 1]))
            if k == 0:
                acc_ref[...] = part
            elif k < _PHASES - 1:
                acc_ref[...] += part
            else:
                x2 = x1_ref[...] + _TWO(gtf_ref) * (acc_ref[...] + part)
                y_ref[...] = _rms(x2, gfin_ref[...]).reshape(y_ref.shape)

    for cp in out_copies(slot, step):
        cp.start()

    @pl.when(step == pl.num_programs(0) - 1)
    def _():
        for cp in out_copies(1 - slot, step - 1) + out_copies(slot, step):
            cp.wait()


def _ffn_and_sample(x, oa, la, obs, mls, mods, goa, gob, wo, gf, wg, wu, wd, gfin, sinks, sample_ins, *, tm):
    b, s, d = x.shape
    n = sample_ins[0].shape[0]
    tiles = s // tm
    steps = 2 * n
    assert steps == b * tiles * _PHASES and _PHASES % 2 == 0 and wg.shape[1] == _FF_CUTS[-1]
    t_new = sample_ins[0].shape[2] // A_GROUP

    def tile(j):
        t = j // _PHASES
        return t // tiles, t % tiles

    row = lambda c: pl.BlockSpec((1, tm, c), lambda j: tile(j) + (0,))
    res = lambda dil: pl.BlockSpec((1, dil, tm // dil, B_W), lambda j: (tile(j)[0], 0, tile(j)[1], 0))
    mod_spec = pl.BlockSpec((1, 1, d), lambda j: (tile(j)[0], 0, 0))
    expand = (jnp.arange(B_W)[None, :] // HEAD_DIM == jnp.arange(B_HEADS)[:, None]).astype(_BF)
    ffn_ins = [x, oa, la] + list(obs) + list(mls) + [expand] + list(mods) + [goa, gob, wo, gf, wg, wu, wd, gfin]
    ffn_specs = ([row(d), row(A_QW), row(A_Q_HEADS)] + [res(o.shape[1]) for o in obs]
                 + [row(2 * B_HEADS)] * len(mls)
                 + [_resident(expand.shape)] + [mod_spec] * 4
                 + [_resident(a.shape) for a in (goa, gob, wo, gf, wg, wu, wd, gfin)])
    half_blk = lambda a: pl.BlockSpec((1, 1) + a.shape[2:], lambda j: (j // 2, j % 2) + (0,) * (a.ndim - 2))
    (qa_g, ka_n, va_n, ca_k, ca_v, ak8, av8, qb_h, kb8, vb8, cb_k, cb_v) = sample_ins
    sample_outs = [jax.ShapeDtypeStruct(a.shape, _F32) for a in (qa_g, qb_h, ca_k, ca_v, cb_k, cb_v)]
    cache_blk = lambda a: pl.BlockSpec((1, a.shape[1] // 2, a.shape[2]), lambda j: (j // 2, j % 2, 0))
    s_specs = [half_blk(qa_g), half_blk(ka_n), half_blk(va_n), cache_blk(ca_k), cache_blk(ca_v), half_blk(ak8),
               half_blk(av8), half_blk(qb_h), half_blk(kb8), half_blk(vb8), cache_blk(cb_k), cache_blk(cb_v)]
    in_hbm = pl.BlockSpec(memory_space=pl.ANY)
    o_specs = [half_blk(qa_g), half_blk(qb_h), cache_blk(ca_k), cache_blk(ca_v), in_hbm, in_hbm]
    stage = pltpu.VMEM((B_W // LANES, tm, LANES), _F32)
    out_buf = pltpu.VMEM((2, cb_k.shape[1] // 2, cb_k.shape[2]), _F32)
    outs = pl.pallas_call(
        functools.partial(_ffn_and_sample_kernel, t_new=t_new),
        grid=(steps,),
        in_specs=ffn_specs + [pl.BlockSpec(memory_space=pltpu.SMEM)] + s_specs,
        out_specs=[row(d)] + o_specs,
        out_shape=[jax.ShapeDtypeStruct((b, s, d), _F32)] + sample_outs,
        scratch_shapes=[stage, stage, pltpu.VMEM((tm, d), _F32), pltpu.VMEM((tm, d), _BF), pltpu.VMEM((tm, d), _F32),
                        out_buf, out_buf, pltpu.SemaphoreType.DMA((2, 2))],
        compiler_params=_params(("arbitrary",)),
        name="ffn_and_sample",
    )(*ffn_ins, sinks, *sample_ins)
    return outs


def _merge_ffn_sample_kernel(x_ref, oa_ref, ob_ref, gta_ref, shf_ref, scf_ref, gtf_ref,
                             goa_ref, gob_ref, wo_ref, gf_ref, wg_ref, wu_ref, wd_ref, gfin_ref, y_ref):
    x1, h = _attn_residual(_TWO(x_ref), _TWO(oa_ref), _TWO(ob_ref), gta_ref[...], shf_ref[...], scf_ref[...],
                           goa_ref, gob_ref, wo_ref, gf_ref)
    acc = jnp.zeros_like(x1)
    for k in range(_PHASES):
        acc = acc + _ffn_chunk(h, wg_ref, wu_ref, wd_ref, slice(_FF_CUTS[k], _FF_CUTS[k + 1]))
    y_ref[...] = _rms(x1 + gtf_ref[...] * acc, gfin_ref[...]).reshape(y_ref.shape)


def _merge_ffn_sample(x, oa, ob, mods, goa, gob, wo, gf, wg, wu, wd, gfin):
    t, n, d = x.shape
    row = lambda c: pl.BlockSpec((1, n, c), lambda j: (j, 0, 0))
    mod_spec = pl.BlockSpec((n, d), lambda j: (0, 0))
    weights = (goa, gob, wo, gf, wg, wu, wd, gfin)
    return pl.pallas_call(
        _merge_ffn_sample_kernel,
        grid=(t,),
        in_specs=[row(d), row(A_QW), row(B_W)] + [mod_spec] * 4 + [_resident(a.shape) for a in weights],
        out_specs=row(d),
        out_shape=jax.ShapeDtypeStruct((t, n, d), _F32),
        compiler_params=_params(("parallel",)),
        name="merge_ffn_sample",
    )(x, oa, ob, *mods, *weights)


def _rope_tables(pos):
    half = HEAD_DIM // 2
    inv = jnp.exp(-math.log(ROPE_THETA) * jnp.arange(half, dtype=_F32) * (2.0 / HEAD_DIM))
    ang = pos.astype(_F32)[:, None] * inv[None, :]
    cos, sin = jnp.cos(ang), jnp.sin(ang)
    return jnp.tile(cos, (1, LANES // half)), jnp.tile(jnp.concatenate([-sin, sin], axis=1), (1, LANES // HEAD_DIM))


def _window_on_lanes(cache):
    _, n, w, h, dh = cache.shape
    return cache.transpose(0, 1, 3, 4, 2).reshape(n, h * dh, w)


def _window_off_lanes(t, h):
    n, _, w = t.shape
    return t.reshape(1, n, h, HEAD_DIM, w).transpose(0, 1, 4, 2, 3)


def _halves_last_rows(new_t, pad_to):
    t, n, c = new_t.shape
    v = new_t.reshape(t, n, 2, c // 2).transpose(1, 2, 0, 3)
    return jnp.pad(v, ((0, 0), (0, 0), (SUBLANES - t, 0), (0, pad_to - c // 2)))


def kernel(x_prompt, x_sample, c_prompt, c_sample, cache_a_k, cache_a_v, cache_b_k, cache_b_v, w_ada, b_ada, g_attn, w_in, sinks, g_out_a, g_out_b, w_o, g_ffn, w_gate, w_up, w_down, g_final):
    nb, s, d = x_prompt.shape
    ns, t_new, _ = x_sample.shape
    assert w_ada.shape[0] == 1, "single trunk layer"
    l = 0
    bf = lambda w: w.astype(_BF)
    row = lambda g: g.reshape(1, -1)

    c_all = jnp.concatenate([c_prompt, c_sample], axis=0)
    c_all = jnp.pad(c_all, ((0, (-c_all.shape[0]) % SUBLANES), (0, 0)))
    mod = _adaln(c_all, w_ada[l], b_ada[l].reshape(1, -1))
    mod_p = [m.reshape(nb, 1, d) for m in jnp.split(mod[:nb], 6, axis=-1)]
    mod_s = jnp.split(mod[nb:nb + ns], 6, axis=-1)

    cos_p, sin_p = _rope_tables(jnp.arange(s, dtype=jnp.int32))
    cos_s, sin_s = _rope_tables(PAST_LEN + jnp.arange(t_new, dtype=jnp.int32))
    w_in_bf = bf(w_in[l])
    ffn_w = (row(g_out_a[l]), row(g_out_b[l]), bf(w_o[l]), row(g_ffn[l]), bf(w_gate[l]), bf(w_up[l]), bf(w_down[l]),
             row(g_final))

    wa_p, wb_p = min(A_WINDOW, s), min(B_WINDOW, s)
    (qa, ka, va, q1, k1, v1, q4, k4, v4, q16, k16, v16, ka_t, va_t, kb_t, vb_t) = _inproj_prompt(
        x_prompt, mod_p[0], mod_p[1], row(g_attn[l]), w_in_bf, cos_p, sin_p, tm=512, tail_a=wa_p, tail_b=wb_p)
    xs_t = x_sample.transpose(1, 0, 2)
    qa_s, ka_s, va_s, qb_s, kb_s, vb_s = _inproj_sample(
        xs_t, mod_s[0], mod_s[1], row(g_attn[l]), w_in_bf, cos_s.reshape(t_new, 1, LANES),
        sin_s.reshape(t_new, 1, LANES))

    oa, la = _swa_prompt(qa, ka, va, sinks[l])
    obs, mls = [], []
    for (q, k, v), (_, dil) in zip(((q1, k1, v1), (q4, k4, v4), (q16, k16, v16)), B_PATTERNS):
        o, ml = _band_attn(q, k, v, dil)
        obs.append(o)
        mls.append(ml.transpose(0, 2, 1, 3).reshape(nb, s, 2 * B_HEADS))

    qa_g = (qa_s.reshape(t_new, ns, A_KV_HEADS, A_GROUP, HEAD_DIM).transpose(1, 2, 0, 3, 4)
            .reshape(ns, A_KV_HEADS, t_new * A_GROUP, HEAD_DIM))
    kv_g = lambda t: t.reshape(t_new, ns, A_KV_HEADS, HEAD_DIM).transpose(1, 2, 0, 3)
    qb_h = qb_s.reshape(t_new, ns, 2, B_W // 2).transpose(1, 2, 0, 3)
    sample_ins = (qa_g, kv_g(ka_s), kv_g(va_s), _window_on_lanes(cache_a_k), _window_on_lanes(cache_a_v),
                  _halves_last_rows(ka_s, LANES), _halves_last_rows(va_s, LANES), qb_h,
                  _halves_last_rows(kb_s, B_W // 2), _halves_last_rows(vb_s, B_W // 2),
                  _window_on_lanes(cache_b_k), _window_on_lanes(cache_b_v))
    y_prompt, oa_g, ob_h, na_k, na_v, nb_k, nb_v = _ffn_and_sample(
        x_prompt, oa, la, obs, mls, mod_p[2:6], *ffn_w, sinks[l], sample_ins, tm=256)
    pa_k = ka_t.reshape(1, nb, wa_p, A_KV_HEADS, HEAD_DIM)
    pa_v = va_t.reshape(1, nb, wa_p, A_KV_HEADS, HEAD_DIM)
    pb_k = kb_t.reshape(1, nb, wb_p, B_HEADS, HEAD_DIM)
    pb_v = vb_t.reshape(1, nb, wb_p, B_HEADS, HEAD_DIM)

    oa_t = (oa_g.reshape(ns, A_KV_HEADS, t_new, A_GROUP, HEAD_DIM).transpose(2, 0, 1, 3, 4)
            .reshape(t_new, ns, A_QW))
    ob_t = ob_h.transpose(2, 0, 1, 3).reshape(t_new, ns, B_W)
    y_s = _merge_ffn_sample(xs_t, oa_t, ob_t, mod_s[2:6], *ffn_w)
    y_sample = y_s.transpose(1, 0, 2)
    sa_k, sa_v = _window_off_lanes(na_k, A_KV_HEADS), _window_off_lanes(na_v, A_KV_HEADS)
    sb_k, sb_v = _window_off_lanes(nb_k, B_HEADS), _window_off_lanes(nb_v, B_HEADS)

    return (y_prompt, y_sample, pa_k, pa_v, pb_k, pb_v, sa_k, sa_v, sb_k, sb_v)
```

```python
import functools
import math

import jax
import jax.numpy as jnp
from jax import lax
from jax.experimental import pallas as pl
from jax.experimental.pallas import tpu as pltpu

HEAD_DIM = 64
A_Q_HEADS = 8
A_KV_HEADS = 2
A_GROUP = A_Q_HEADS // A_KV_HEADS
B_HEADS = 8
A_WINDOW = 128
BLOCK = 128
B_PATTERNS = ((128, 1), (512, 4), (2048, 16))
B_WINDOW = 2048
PAST_LEN = 8192
ROPE_THETA = 10000.0
EPS = 1e-6
A_QW = A_Q_HEADS * HEAD_DIM
A_KVW = A_KV_HEADS * HEAD_DIM
B_W = B_HEADS * HEAD_DIM
SCALE = HEAD_DIM ** -0.5
LANES = 128
SUBLANES = 8
NEG = -1e30
VMEM_LIMIT = 56 * 1024 * 1024
Q_BLOCKS = 16

_BF = jnp.bfloat16
_F32 = jnp.float32


def _params(sem, vmem=VMEM_LIMIT):
    return pltpu.CompilerParams(dimension_semantics=sem, vmem_limit_bytes=vmem)


def _resident(shape):
    nd = len(shape)
    return pl.BlockSpec(shape, lambda *_: (0,) * nd, pipeline_mode=pl.Buffered(1))


def _rms(x, g):
    return x * lax.rsqrt(jnp.mean(x * x, axis=-1, keepdims=True) + EPS) * g


def _log2(n):
    assert n > 0 and n & (n - 1) == 0, "power of two expected"
    return n.bit_length() - 1


def _div(x, n):
    return x >> _log2(n)


def _multiple(x, n):
    _log2(n)
    return (x & (n - 1)) == 0


def _adaln_kernel(c_ref, w_ref, b_ref, o_ref):
    c = c_ref[...]
    a = (c * jax.nn.sigmoid(c)).astype(_BF)
    o_ref[...] = jnp.dot(a, w_ref[...].astype(_BF), preferred_element_type=_F32) + b_ref[...]


def _adaln(c_all, w, b):
    m, d = c_all.shape
    n = w.shape[1]
    tn = 768
    return pl.pallas_call(
        _adaln_kernel,
        grid=(n // tn,),
        in_specs=[pl.BlockSpec((m, d), lambda j: (0, 0)),
                  pl.BlockSpec((d, tn), lambda j: (0, j)),
                  pl.BlockSpec((1, tn), lambda j: (0, j))],
        out_specs=pl.BlockSpec((m, tn), lambda j: (0, j)),
        out_shape=jax.ShapeDtypeStruct((m, n), _F32),
        compiler_params=_params(("arbitrary",)),
        name="adaln",
    )(c_all, w, b)


def _rope(x, cos, sin_signed, first_half):
    sw = jnp.where(first_half, pltpu.roll(x, 96, 1), pltpu.roll(x, 32, 1))
    return x * cos + sw * sin_signed


def _project(x_ref, sh_ref, sc_ref, g_ref, w_ref, cos_ref, sin_ref):
    d = x_ref.shape[-1]
    x = x_ref[...].reshape(-1, d)
    sh = sh_ref[...].reshape(-1, d)
    sc = sc_ref[...].reshape(-1, d)
    h = _rms(x, g_ref[...]) * (1.0 + sc) + sh
    proj = jnp.dot(h.astype(_BF), w_ref[...], preferred_element_type=_F32)
    cos = cos_ref[...].reshape(-1, LANES)
    sin = sin_ref[...].reshape(-1, LANES)
    lane = lax.broadcasted_iota(jnp.int32, (1, LANES), 1)
    first_half = (lane & (HEAD_DIM - 1)) < (HEAD_DIM // 2)

    def rot(col0, ncols):
        parts = [_rope(proj[:, col0 + j * LANES: col0 + (j + 1) * LANES], cos, sin, first_half)
                 for j in range(ncols // LANES)]
        return parts[0] if len(parts) == 1 else jnp.concatenate(parts, axis=1)

    o = 0
    qa = rot(o, A_QW) * SCALE; o += A_QW
    ka = rot(o, A_KVW); o += A_KVW
    va = proj[:, o:o + A_KVW]; o += A_KVW
    qb = rot(o, B_W) * SCALE; o += B_W
    kb = rot(o, B_W); o += B_W
    vb = proj[:, o:o + B_W]
    return qa, ka, va, qb, kb, vb


def _put(ref, val):
    ref[...] = val.astype(ref.dtype).reshape(ref.shape)


def _inproj_prompt_kernel(x_ref, sh_ref, sc_ref, g_ref, w_ref, cos_ref, sin_ref,
                          qa_ref, ka_ref, va_ref, q1_ref, k1_ref, v1_ref, q4_ref, k4_ref, v4_ref,
                          q16_ref, k16_ref, v16_ref, kat_ref, vat_ref, kbt_ref, vbt_ref,
                          sq_ref, sk_ref, sv_ref):
    qa, ka, va, qb, kb, vb = _project(x_ref, sh_ref, sc_ref, g_ref, w_ref, cos_ref, sin_ref)
    tm = qa.shape[0]
    _put(qa_ref, qa)
    lo = lax.broadcasted_iota(jnp.int32, (1, LANES), 1) < HEAD_DIM
    for ref, val in ((ka_ref, ka), (va_ref, va)):
        swapped = pltpu.roll(val, HEAD_DIM, 1)
        _put(ref, jnp.concatenate([jnp.where(lo, val, swapped), jnp.where(lo, swapped, val)], axis=1))
    _put(q1_ref, qb); _put(k1_ref, kb); _put(v1_ref, vb)
    ta = kat_ref.shape[1]
    _put(kat_ref, ka[tm - ta:]); _put(vat_ref, va[tm - ta:])
    _put(kbt_ref, kb); _put(vbt_ref, vb)
    for val, stage, r4, r16 in ((qb, sq_ref, q4_ref, q16_ref), (kb, sk_ref, k4_ref, k16_ref),
                                (vb, sv_ref, v4_ref, v16_ref)):
        for c in range(B_W // LANES):
            stage[c] = val[:, c * LANES:(c + 1) * LANES]
        for dil, out in ((4, r4), (16, r16)):
            n = tm // dil
            for r in range(dil):
                for c in range(B_W // LANES):
                    out[0, r, :, c * LANES:(c + 1) * LANES] = stage[c, pl.ds(r, n, stride=dil), :].astype(out.dtype)


def _inproj_prompt(x, sh, sc, g, w_bf, cos, sin, *, tm, tail_a, tail_b):
    b, s, d = x.shape
    nt = s // tm
    assert tail_b % tm == 0 and tail_a <= tm
    row = lambda c: pl.BlockSpec((1, tm, c), lambda bb, i: (bb, i, 0))
    res = lambda dil: pl.BlockSpec((1, dil, tm // dil, B_W), lambda bb, i: (bb, 0, i, 0))
    mod_spec = pl.BlockSpec((1, 1, d), lambda bb, i: (bb, 0, 0))
    tab_spec = pl.BlockSpec((tm, LANES), lambda bb, i: (i, 0))
    tail_a_spec = pl.BlockSpec((1, tail_a, A_KVW), lambda bb, i: (bb, 0, 0))
    tail_b_spec = pl.BlockSpec((1, tm, B_W), lambda bb, i: (bb, jnp.maximum(i - (nt - tail_b // tm), 0), 0))
    shp = lambda c, dt: jax.ShapeDtypeStruct((b, s, c), dt)
    rshp = lambda dil: jax.ShapeDtypeStruct((b, dil, s // dil, B_W), _BF)
    return pl.pallas_call(
        _inproj_prompt_kernel,
        grid=(b, nt),
        in_specs=[row(d), mod_spec, mod_spec, _resident((1, d)), _resident(w_bf.shape), tab_spec, tab_spec],
        out_specs=[row(A_QW), row(2 * A_KVW), row(2 * A_KVW)] + [res(1)] * 3 + [res(4)] * 3 + [res(16)] * 3
                  + [tail_a_spec] * 2 + [tail_b_spec] * 2,
        out_shape=[shp(A_QW, _BF), shp(2 * A_KVW, _BF), shp(2 * A_KVW, _BF)] + [rshp(1)] * 3 + [rshp(4)] * 3 + [rshp(16)] * 3
                  + [jax.ShapeDtypeStruct((b, tail_a, A_KVW), _F32)] * 2
                  + [jax.ShapeDtypeStruct((b, tail_b, B_W), _F32)] * 2,
        scratch_shapes=[pltpu.VMEM((B_W // LANES, tm, LANES), _F32)] * 3,
        compiler_params=_params(("parallel", "arbitrary")),
        name="in_proj_prompt",
    )(x, sh, sc, g, w_bf, cos, sin)


def _inproj_sample_kernel(x_ref, sh_ref, sc_ref, g_ref, w_ref, cos_ref, sin_ref,
                          qa_ref, ka_ref, va_ref, qb_ref, kb_ref, vb_ref):
    vals = _project(x_ref, sh_ref, sc_ref, g_ref, w_ref, cos_ref, sin_ref)
    for ref, val in zip((qa_ref, ka_ref, va_ref, qb_ref, kb_ref, vb_ref), vals):
        _put(ref, val)


def _inproj_sample(x, sh, sc, g, w_bf, cos, sin):
    t, n, d = x.shape
    row = lambda c: pl.BlockSpec((1, n, c), lambda j: (j, 0, 0))
    mod_spec = pl.BlockSpec((n, d), lambda j: (0, 0))
    tab_spec = pl.BlockSpec((1, 1, LANES), lambda j: (j, 0, 0))
    widths = (A_QW, A_KVW, A_KVW, B_W, B_W, B_W)
    return pl.pallas_call(
        _inproj_sample_kernel,
        grid=(t,),
        in_specs=[row(d), mod_spec, mod_spec, _resident((1, d)), _resident(w_bf.shape), tab_spec, tab_spec],
        out_specs=[row(c) for c in widths],
        out_shape=[jax.ShapeDtypeStruct((t, n, c), _F32) for c in widths],
        compiler_params=_params(("parallel",)),
        name="in_proj_sample",
    )(x, sh, sc, g, w_bf, cos, sin)


def _band_bias(max_dist, drop_prev):
    qi = lax.broadcasted_iota(jnp.int32, (BLOCK, 2 * BLOCK), 0)
    ki = lax.broadcasted_iota(jnp.int32, (BLOCK, 2 * BLOCK), 1)
    rel = qi + BLOCK - ki
    valid = (rel >= 0) & (rel <= max_dist)
    if drop_prev is not None:
        valid = valid & ((ki >= BLOCK) | jnp.logical_not(drop_prev))
    return jnp.where(valid, 0.0, NEG).astype(_F32)


def _kv_window(prev_ref, cur_ref, j, lead, cols):
    if j == 0:
        return jnp.concatenate([prev_ref[lead + (slice(None), cols)], cur_ref[lead + (slice(0, BLOCK), cols)]], axis=0)
    return cur_ref[lead + (slice((j - 1) * BLOCK, (j + 1) * BLOCK), cols)]


def _stat_columns(stats, cols, acc):
    lane = lax.broadcasted_iota(jnp.int32, (1, acc.shape[1]), 1)
    for col, stat in zip(cols, stats):
        acc = jnp.where(lane == col, stat, acc)
    return acc


def _band_attn_kernel(q_ref, kp_ref, kc_ref, vp_ref, vc_ref, o_ref, ml_ref, *, max_dist, nq):
    first_step = pl.program_id(2) == 0
    bias = _band_bias(max_dist, None)
    bias0 = _band_bias(max_dist, first_step)
    lane = lax.broadcasted_iota(jnp.int32, (1, LANES), 1)
    lo = lane < HEAD_DIM
    zero = jnp.zeros((), _BF)
    for j in range(nq):
        rows = slice(j * BLOCK, (j + 1) * BLOCK)
        bj = bias0 if j == 0 else bias
        bj = jnp.concatenate([bj, bj], axis=0)
        ml_all = jnp.zeros((BLOCK, LANES), _F32)
        for c in range(B_W // LANES):
            cs = slice(c * LANES, (c + 1) * LANES)
            q2 = q_ref[0, 0, rows, cs]
            k2 = _kv_window(kp_ref, kc_ref, j, (0, 0), cs)
            v2 = _kv_window(vp_ref, vc_ref, j, (0, 0), cs)
            qs = jnp.concatenate([jnp.where(lo, q2, zero), jnp.where(lo, zero, q2)], axis=0)
            s = lax.dot_general(qs, k2, (((1,), (1,)), ((), ())), preferred_element_type=_F32) + bj
            m = jnp.max(s, axis=-1, keepdims=True)
            p = jnp.exp(s - m)
            l = jnp.sum(p, axis=-1, keepdims=True)
            pv = jnp.dot(p.astype(_BF), v2, preferred_element_type=_F32)
            o_ref[0, 0, rows, cs] = jnp.where(lo, pv[:BLOCK], pv[BLOCK:]).astype(o_ref.dtype)
            ml_all = _stat_columns((m[:BLOCK], m[BLOCK:], l[:BLOCK], l[BLOCK:]),
                                   (2 * c, 2 * c + 1, B_HEADS + 2 * c, B_HEADS + 2 * c + 1), ml_all)
        ml_ref[0, 0, rows, :] = ml_all


def _band_attn(q, k, v, dil):
    b, _, m, w = q.shape
    nq = min(Q_BLOCKS, m // BLOCK)
    tq = nq * BLOCK
    cur = pl.BlockSpec((1, 1, tq, w), lambda bb, r, i: (bb, r, i, 0))
    prev = pl.BlockSpec((1, 1, BLOCK, w), lambda bb, r, i: (bb, r, jnp.maximum(i * nq - 1, 0), 0))
    window, _ = [p for p in B_PATTERNS if p[1] == dil][0]
    return pl.pallas_call(
        functools.partial(_band_attn_kernel, max_dist=window // dil, nq=nq),
        grid=(b, dil, m // tq),
        in_specs=[cur, prev, cur, prev, cur],
        out_specs=[cur, pl.BlockSpec((1, 1, tq, LANES), lambda bb, r, i: (bb, r, i, 0))],
        out_shape=[jax.ShapeDtypeStruct((b, dil, m, w), _BF),
                   jax.ShapeDtypeStruct((b, dil, m, LANES), _F32)],
        compiler_params=_params(("parallel", "parallel", "arbitrary")),
        name=f"band_attn_d{dil}",
    )(q, k, k, v, v)


def _swa_kernel(sink_ref, q_ref, kp_ref, kc_ref, vp_ref, vc_ref, o_ref, l_ref, *, nq):
    first_step = pl.program_id(1) == 0
    bias = _band_bias(A_WINDOW - 1, None)
    bias0 = _band_bias(A_WINDOW - 1, first_step)
    lane = lax.broadcasted_iota(jnp.int32, (1, LANES), 1)
    lo = lane < HEAD_DIM
    zero = jnp.zeros((), _BF)
    col0 = lax.broadcasted_iota(jnp.int32, (1, 2 * BLOCK), 1) == 0
    for j in range(nq):
        rows = slice(j * BLOCK, (j + 1) * BLOCK)
        bj = bias0 if j == 0 else bias
        bj = jnp.concatenate([bj, bj], axis=0)
        l_all = jnp.zeros((BLOCK, A_Q_HEADS), _F32)
        for c in range(A_QW // LANES):
            g = (2 * c) // A_GROUP
            gs = slice(g * LANES, (g + 1) * LANES)
            q2 = q_ref[0, rows, c * LANES:(c + 1) * LANES]
            k2 = _kv_window(kp_ref, kc_ref, j, (0,), gs)
            v2 = _kv_window(vp_ref, vc_ref, j, (0,), gs)
            qs = jnp.concatenate([jnp.where(lo, q2, zero), jnp.where(lo, zero, q2)], axis=0)
            s = lax.dot_general(qs, k2, (((1,), (1,)), ((), ())), preferred_element_type=_F32) + bj
            s = jnp.concatenate([jnp.where(col0, sink_ref[2 * c + half], s[half * BLOCK:(half + 1) * BLOCK])
                                 for half in range(2)], axis=0)
            m = jnp.max(s, axis=-1, keepdims=True)
            p = jnp.exp(s - m)
            l = jnp.sum(p, axis=-1, keepdims=True)
            p = jnp.where(col0, 0.0, p).astype(_BF)
            pv = jnp.dot(p, v2, preferred_element_type=_F32)
            _put(o_ref.at[0, rows, c * LANES:(c + 1) * LANES], jnp.where(lo, pv[:BLOCK], pv[BLOCK:]))
            l_all = _stat_columns((l[:BLOCK], l[BLOCK:]), (2 * c, 2 * c + 1), l_all)
        l_ref[0, rows, :] = l_all


def _swa_prompt(q, k, v, sinks):
    b, s, _ = q.shape
    nq = Q_BLOCKS
    tq = nq * BLOCK
    assert A_WINDOW <= BLOCK, "the sink logit borrows a key column that no query row can reach"
    qspec = pl.BlockSpec((1, tq, A_QW), lambda bb, i: (bb, i, 0))
    cur = pl.BlockSpec((1, tq, 2 * A_KVW), lambda bb, i: (bb, i, 0))
    prev = pl.BlockSpec((1, BLOCK, 2 * A_KVW), lambda bb, i: (bb, jnp.maximum(i * nq - 1, 0), 0))
    return pl.pallas_call(
        functools.partial(_swa_kernel, nq=nq),
        grid=(b, s // tq),
        in_specs=[pl.BlockSpec(memory_space=pltpu.SMEM), qspec, prev, cur, prev, cur],
        out_specs=[qspec, pl.BlockSpec((1, tq, A_Q_HEADS), lambda bb, i: (bb, i, 0))],
        out_shape=[jax.ShapeDtypeStruct((b, s, A_QW), _BF), jax.ShapeDtypeStruct((b, s, A_Q_HEADS), _F32)],
        compiler_params=_params(("parallel", "arbitrary")),
        name="swa_prompt",
    )(sinks, q, k, k, v, v)


def _roll_in(x_ref, new8_ref, o_ref, t_new):
    rows, width = x_ref.shape
    new8 = new8_ref[0, 0]
    tail = jnp.concatenate([jnp.zeros((LANES - new8.shape[0], new8.shape[1]), _F32), new8], axis=0).T[:rows]
    lane = lax.broadcasted_iota(jnp.int32, (1, LANES), 1)
    keep = lane < LANES - t_new
    nxt = pltpu.roll(x_ref[:, 0:LANES], LANES - t_new, 1)
    for c in range(width // LANES):
        cur = nxt
        if c + 1 < width // LANES:
            nxt = pltpu.roll(x_ref[:, (c + 1) * LANES:(c + 2) * LANES], LANES - t_new, 1)
        else:
            nxt = tail
        o_ref[:, c * LANES:(c + 1) * LANES] = jnp.where(keep, cur, nxt)


def _sample_half(g, refs, *, t_new):
    (sink_ref, qa_ref, kan_ref, van_ref, cak_ref, cav_ref, ak8_ref, av8_ref,
     qb_ref, kb8_ref, vb8_ref, cbk_ref, cbv_ref,
     oa_ref, ob_ref, nak_ref, nav_ref, nbk_ref, nbv_ref) = refs
    wa, wb = cak_ref.shape[-1], cbk_ref.shape[-1]

    rows_a = t_new * A_GROUP
    ra = lax.broadcasted_iota(jnp.int32, (rows_a, 1), 0)
    qi_a = _div(ra, A_GROUP)
    rel_a = wa + qi_a - lax.broadcasted_iota(jnp.int32, (1, wa), 1)
    valid_a = (rel_a >= 0) & (rel_a < A_WINDOW)
    q = qa_ref[0, 0]
    s = jnp.dot(q.astype(_BF), cak_ref[0].astype(_BF), preferred_element_type=_F32)
    s = jnp.where(valid_a, s, NEG)
    s_new = [jnp.where(qi_a >= j, jnp.sum(q * kan_ref[0, 0, j:j + 1, :], axis=-1, keepdims=True), NEG)
             for j in range(t_new)]
    sink = jnp.zeros((rows_a, 1), _F32)
    for u in range(A_GROUP):
        sink = jnp.where((ra & (A_GROUP - 1)) == u, sink_ref[g * A_GROUP + u], sink)
    m = functools.reduce(jnp.maximum, [jnp.max(s, axis=-1, keepdims=True), sink] + s_new)
    p = jnp.exp(s - m)
    p_new = [jnp.exp(z - m) for z in s_new]
    l = functools.reduce(jnp.add, [jnp.sum(p, axis=-1, keepdims=True), jnp.exp(sink - m)] + p_new)
    o = lax.dot_general(p.astype(_BF), cav_ref[0].astype(_BF), (((1,), (1,)), ((), ())),
                        preferred_element_type=_F32)
    for j in range(t_new):
        o = o + p_new[j] * van_ref[0, 0, j:j + 1, :]
    oa_ref[0, 0] = o * (1.0 / l)
    _roll_in(cak_ref.at[0], ak8_ref, nak_ref.at[0], t_new)
    _roll_in(cav_ref.at[0], av8_ref, nav_ref.at[0], t_new)

    width_b = cbk_ref.shape[1]
    r0 = kb8_ref.shape[2] - t_new
    rows_b = t_new * SUBLANES
    sub = lax.broadcasted_iota(jnp.int32, (SUBLANES, width_b), 0)
    own = _div(lax.broadcasted_iota(jnp.int32, (SUBLANES, width_b), 1), HEAD_DIM) == sub
    qbd = jnp.concatenate([jnp.where(own, qb_ref[0, 0, i:i + 1, :], 0.0) for i in range(t_new)], axis=0)
    s_all = jnp.dot(qbd.astype(_BF), cbk_ref[0].astype(_BF), preferred_element_type=_F32)
    qi = _div(lax.broadcasted_iota(jnp.int32, (rows_b, 1), 0), SUBLANES)
    s_new = [jnp.sum(qbd * kb8_ref[0, 0, r0 + j:r0 + j + 1, :], axis=-1, keepdims=True) for j in range(t_new)]
    pats = []
    for window, dil in B_PATTERNS:
        lo_lane = max(wb - (-(-window // LANES) * LANES), 0)
        wl = lo_lane + lax.broadcasted_iota(jnp.int32, (1, wb - lo_lane), 1)
        rel = wb + qi - wl
        valid = (rel <= window) & _multiple(rel, dil)
        s = jnp.where(valid, s_all[:, lo_lane:], NEG)
        sn = [jnp.where((qi - j >= 0) & _multiple(qi - j, dil), s_new[j], NEG) for j in range(t_new)]
        m = functools.reduce(jnp.maximum, [jnp.max(s, axis=-1, keepdims=True)] + sn)
        p = jnp.exp(s - m)
        pn = [jnp.exp(z - m) for z in sn]
        l = functools.reduce(jnp.add, [jnp.sum(p, axis=-1, keepdims=True)] + pn)
        pats.append((lo_lane, p, pn, l, m + jnp.log(l)))
    mx = functools.reduce(jnp.maximum, [t[4] for t in pats])
    es = [jnp.exp(t[4] - mx) for t in pats]
    den = functools.reduce(jnp.add, es)
    coef = [e / (den * t[3]) for e, t in zip(es, pats)]
    starts = sorted({t[0] for t in pats} | {wb})
    blocks = []
    for a, b in zip(starts[:-1], starts[1:]):
        acc = None
        for c, (lo_lane, p, _, _, _) in zip(coef, pats):
            if lo_lane <= a:
                term = c * p[:, a - lo_lane:b - lo_lane]
                acc = term if acc is None else acc + term
        blocks.append(acc)
    lead = starts[0]
    p_comb = jnp.concatenate(blocks, axis=1) if len(blocks) > 1 else blocks[0]
    o_full = lax.dot_general(p_comb.astype(_BF), cbv_ref[0, :, lead:].astype(_BF), (((1,), (1,)), ((), ())),
                             preferred_element_type=_F32)
    for j in range(t_new):
        pj = functools.reduce(jnp.add, [c * t[2][j] for c, t in zip(coef, pats)])
        o_full = o_full + pj * vb8_ref[0, 0, r0 + j:r0 + j + 1, :]
    for i in range(t_new):
        blk = jnp.where(own, o_full[i * SUBLANES:(i + 1) * SUBLANES], 0.0)
        ob_ref[0, 0, i:i + 1, :] = jnp.sum(blk, axis=0, keepdims=True)
    _roll_in(cbk_ref.at[0], kb8_ref, nbk_ref, t_new)
    _roll_in(cbv_ref.at[0], vb8_ref, nbv_ref, t_new)


def _ffn_chunk(h, wg_ref, wu_ref, wd_ref, cs):
    gt = jnp.dot(h, wg_ref[:, cs], preferred_element_type=_F32)
    up = jnp.dot(h, wu_ref[:, cs], preferred_element_type=_F32)
    act = (gt * jax.nn.sigmoid(gt) * up).astype(_BF)
    return jnp.dot(act, wd_ref[cs, :], preferred_element_type=_F32)


def _attn_residual(x, oa, ob, gta, shf, scf, goa_ref, gob_ref, wo_ref, gf_ref):
    merged = jnp.concatenate([_rms(oa, goa_ref[...]), _rms(ob, gob_ref[...])], axis=1)
    x1 = x + gta * jnp.dot(merged.astype(_BF), wo_ref[...], preferred_element_type=_F32)
    return x1, (_rms(x1, gf_ref[...]) * (1.0 + scf) + shf).astype(_BF)


_TWO = lambda r: r[...].reshape(-1, r.shape[-1])
_FF_CUTS = (0, 384, 1152, 2048, 2816)
_PHASES = len(_FF_CUTS) - 1


def _ffn_and_sample_kernel(*refs, t_new):
    (x_ref, oa_ref, la_ref, o1_ref, o4_ref, o16_ref, ml1_ref, ml2_ref, ml3_ref, e_ref,
     gta_ref, shf_ref, scf_ref, gtf_ref, goa_ref, gob_ref, wo_ref, gf_ref, wg_ref, wu_ref, wd_ref, gfin_ref) = refs[:22]
    sample_in = refs[22:35]
    y_ref = refs[35]
    oa_s_ref, ob_s_ref, nak_ref, nav_ref, nbk_hbm, nbv_hbm = refs[36:42]
    s4_ref, s16_ref, x1_ref, h_ref, acc_ref, kbuf_ref, vbuf_ref, out_sem, t4_ref, t16_ref = refs[42:]
    step = pl.program_id(0)
    half = step % 2
    phase = step % _PHASES
    tm = x_ref.shape[1]

    slot = step % 2
    rows_b = kbuf_ref.shape[1]

    def out_copies(slot_, step_):
        dst = (step_ // 2, pl.ds(pl.multiple_of((step_ % 2) * rows_b, rows_b), rows_b), slice(None))
        return (pltpu.make_async_copy(kbuf_ref.at[slot_], nbk_hbm.at[dst], out_sem.at[0, slot_]),
                pltpu.make_async_copy(vbuf_ref.at[slot_], nbv_hbm.at[dst], out_sem.at[1, slot_]))

    @pl.when(step >= 2)
    def _():
        for cp in out_copies(slot, step - 2):
            cp.wait()

    sample_out = (oa_s_ref, ob_s_ref, nak_ref, nav_ref, kbuf_ref.at[slot], vbuf_ref.at[slot])

    def mix_and_project():
        for dil, src, stage, stat_src, stat_stage in ((4, o4_ref, s4_ref, ml2_ref, t4_ref),
                                                      (16, o16_ref, s16_ref, ml3_ref, t16_ref)):
            for r in range(dil):
                rows_r = pl.ds(r, tm // dil, stride=dil)
                stat_stage[rows_r, :] = stat_src[0, r]
                for c in range(B_W // LANES):
                    stage[c, rows_r, :] = src[0, r, :, c * LANES:(c + 1) * LANES].astype(_F32)
        gather = lambda stage: jnp.concatenate([stage[c] for c in range(B_W // LANES)], axis=1)
        outs = [_TWO(o1_ref).astype(_F32), gather(s4_ref), gather(s16_ref)]

        def widen(per_head):
            hi = per_head.astype(_BF)
            lo = (per_head - hi.astype(_F32)).astype(_BF)
            return (jnp.dot(hi, e_ref[...], preferred_element_type=_F32)
                    + jnp.dot(lo, e_ref[...], preferred_element_type=_F32))

        stats = (ml1_ref.at[0, 0], t4_ref, t16_ref)
        ms = [r[:, 0:B_HEADS] for r in stats]
        ls = [r[:, B_HEADS:2 * B_HEADS] for r in stats]
        lses = [m + jnp.log(l) for m, l in zip(ms, ls)]
        mx = jnp.maximum(jnp.maximum(lses[0], lses[1]), lses[2])
        es = [jnp.exp(z - mx) for z in lses]
        den = es[0] + es[1] + es[2]
        ob = jnp.zeros((tm, B_W), _F32)
        for e, l, o in zip(es, ls, outs):
            ob = ob + widen(e / (den * l)) * o
        oa = widen(1.0 / _TWO(la_ref)) * _TWO(oa_ref).astype(_F32)
        x1, h = _attn_residual(_TWO(x_ref), oa, ob, _TWO(gta_ref), _TWO(shf_ref), _TWO(scf_ref),
                               goa_ref, gob_ref, wo_ref, gf_ref)
        x1_ref[...] = x1
        h_ref[...] = h

    for k in range(_PHASES):
        @pl.when(phase == k)
        def _(k=k):
            _sample_half(half, tuple(sample_in) + tuple(sample_out), t_new=t_new)
            if k == 0:
                mix_and_project()
            part = _ffn_chunk(h_ref[...], wg_ref, wu_ref, wd_ref, slice(_FF_CUTS[k], _FF_CUTS[k + 1]))
            if k == 0:
                acc_ref[...] = part
            elif k < _PHASES - 1:
                acc_ref[...] += part
            else:
                x2 = x1_ref[...] + _TWO(gtf_ref) * (acc_ref[...] + part)
                y_ref[...] = _rms(x2, gfin_ref[...]).reshape(y_ref.shape)

    for cp in out_copies(slot, step):
        cp.start()

    @pl.when(step == pl.num_programs(0) - 1)
    def _():
        for cp in out_copies(1 - slot, step - 1) + out_copies(slot, step):
            cp.wait()


def _ffn_and_sample(x, oa, la, obs, mls, mods, goa, gob, wo, gf, wg, wu, wd, gfin, sinks, sample_ins, *, tm):
    b, s, d = x.shape
    n = sample_ins[0].shape[0]
    tiles = s // tm
    steps = 2 * n
    assert steps == b * tiles * _PHASES and _PHASES % 2 == 0 and wg.shape[1] == _FF_CUTS[-1]
    t_new = sample_ins[0].shape[2] // A_GROUP

    def tile(j):
        t = j // _PHASES
        return t // tiles, t % tiles

    row = lambda c: pl.BlockSpec((1, tm, c), lambda j: tile(j) + (0,))
    res = lambda dil: pl.BlockSpec((1, dil, tm // dil, B_W), lambda j: (tile(j)[0], 0, tile(j)[1], 0))
    mod_spec = pl.BlockSpec((1, 1, d), lambda j: (tile(j)[0], 0, 0))
    expand = (jnp.arange(B_W)[None, :] // HEAD_DIM == jnp.arange(B_HEADS)[:, None]).astype(_BF)
    ffn_ins = [x, oa, la] + list(obs) + list(mls) + [expand] + list(mods) + [goa, gob, wo, gf, wg, wu, wd, gfin]
    ffn_specs = ([row(d), row(A_QW), row(A_Q_HEADS)] + [res(o.shape[1]) for o in obs]
                 + [pl.BlockSpec((1, o.shape[1], tm // o.shape[1], LANES), lambda j: (tile(j)[0], 0, tile(j)[1], 0))
                    for o in obs]
                 + [_resident(expand.shape)] + [mod_spec] * 4
                 + [_resident(a.shape) for a in (goa, gob, wo, gf, wg, wu, wd, gfin)])
    half_blk = lambda a: pl.BlockSpec((1, 1) + a.shape[2:], lambda j: (j // 2, j % 2) + (0,) * (a.ndim - 2))
    (qa_g, ka_n, va_n, ca_k, ca_v, ak8, av8, qb_h, kb8, vb8, cb_k, cb_v) = sample_ins
    sample_outs = [jax.ShapeDtypeStruct(a.shape, _F32) for a in (qa_g, qb_h, ca_k, ca_v, cb_k, cb_v)]
    cache_blk = lambda a: pl.BlockSpec((1, a.shape[1] // 2, a.shape[2]), lambda j: (j // 2, j % 2, 0))
    s_specs = [half_blk(qa_g), half_blk(ka_n), half_blk(va_n), cache_blk(ca_k), cache_blk(ca_v), half_blk(ak8),
               half_blk(av8), half_blk(qb_h), half_blk(kb8), half_blk(vb8), cache_blk(cb_k), cache_blk(cb_v)]
    in_hbm = pl.BlockSpec(memory_space=pl.ANY)
    o_specs = [half_blk(qa_g), half_blk(qb_h), cache_blk(ca_k), cache_blk(ca_v), in_hbm, in_hbm]
    stage = pltpu.VMEM((B_W // LANES, tm, LANES), _F32)
    out_buf = pltpu.VMEM((2, cb_k.shape[1] // 2, cb_k.shape[2]), _F32)
    outs = pl.pallas_call(
        functools.partial(_ffn_and_sample_kernel, t_new=t_new),
        grid=(steps,),
        in_specs=ffn_specs + [pl.BlockSpec(memory_space=pltpu.SMEM)] + s_specs,
        out_specs=[row(d)] + o_specs,
        out_shape=[jax.ShapeDtypeStruct((b, s, d), _F32)] + sample_outs,
        scratch_shapes=[stage, stage, pltpu.VMEM((tm, d), _F32), pltpu.VMEM((tm, d), _BF), pltpu.VMEM((tm, d), _F32),
                        out_buf, out_buf, pltpu.SemaphoreType.DMA((2, 2)),
                        pltpu.VMEM((tm, LANES), _F32), pltpu.VMEM((tm, LANES), _F32)],
        compiler_params=_params(("arbitrary",)),
        name="ffn_and_sample",
    )(*ffn_ins, sinks, *sample_ins)
    return outs


def _merge_ffn_sample_kernel(x_ref, oa_ref, ob_ref, gta_ref, shf_ref, scf_ref, gtf_ref,
                             goa_ref, gob_ref, wo_ref, gf_ref, wg_ref, wu_ref, wd_ref, gfin_ref, y_ref):
    x1, h = _attn_residual(_TWO(x_ref), _TWO(oa_ref), _TWO(ob_ref), gta_ref[...], shf_ref[...], scf_ref[...],
                           goa_ref, gob_ref, wo_ref, gf_ref)
    acc = jnp.zeros_like(x1)
    for k in range(_PHASES):
        acc = acc + _ffn_chunk(h, wg_ref, wu_ref, wd_ref, slice(_FF_CUTS[k], _FF_CUTS[k + 1]))
    y_ref[...] = _rms(x1 + gtf_ref[...] * acc, gfin_ref[...]).reshape(y_ref.shape)


def _merge_ffn_sample(x, oa, ob, mods, goa, gob, wo, gf, wg, wu, wd, gfin):
    t, n, d = x.shape
    row = lambda c: pl.BlockSpec((1, n, c), lambda j: (j, 0, 0))
    mod_spec = pl.BlockSpec((n, d), lambda j: (0, 0))
    weights = (goa, gob, wo, gf, wg, wu, wd, gfin)
    return pl.pallas_call(
        _merge_ffn_sample_kernel,
        grid=(t,),
        in_specs=[row(d), row(A_QW), row(B_W)] + [mod_spec] * 4 + [_resident(a.shape) for a in weights],
        out_specs=row(d),
        out_shape=jax.ShapeDtypeStruct((t, n, d), _F32),
        compiler_params=_params(("parallel",)),
        name="merge_ffn_sample",
    )(x, oa, ob, *mods, *weights)


def _rope_tables(pos):
    half = HEAD_DIM // 2
    inv = jnp.exp(-math.log(ROPE_THETA) * jnp.arange(half, dtype=_F32) * (2.0 / HEAD_DIM))
    ang = pos.astype(_F32)[:, None] * inv[None, :]
    cos, sin = jnp.cos(ang), jnp.sin(ang)
    return jnp.tile(cos, (1, LANES // half)), jnp.tile(jnp.concatenate([-sin, sin], axis=1), (1, LANES // HEAD_DIM))


def _window_on_lanes(cache):
    _, n, w, h, dh = cache.shape
    return cache.transpose(0, 1, 3, 4, 2).reshape(n, h * dh, w)


def _window_off_lanes(t, h):
    n, _, w = t.shape
    return t.reshape(1, n, h, HEAD_DIM, w).transpose(0, 1, 4, 2, 3)


def _halves_last_rows(new_t, pad_to):
    t, n, c = new_t.shape
    v = new_t.reshape(t, n, 2, c // 2).transpose(1, 2, 0, 3)
    return jnp.pad(v, ((0, 0), (0, 0), (SUBLANES - t, 0), (0, pad_to - c // 2)))


def kernel(x_prompt, x_sample, c_prompt, c_sample, cache_a_k, cache_a_v, cache_b_k, cache_b_v, w_ada, b_ada, g_attn, w_in, sinks, g_out_a, g_out_b, w_o, g_ffn, w_gate, w_up, w_down, g_final):
    nb, s, d = x_prompt.shape
    ns, t_new, _ = x_sample.shape
    assert w_ada.shape[0] == 1, "single trunk layer"
    l = 0
    bf = lambda w: w.astype(_BF)
    row = lambda g: g.reshape(1, -1)

    c_all = jnp.concatenate([c_prompt, c_sample], axis=0)
    c_all = jnp.pad(c_all, ((0, (-c_all.shape[0]) % SUBLANES), (0, 0)))
    mod = _adaln(c_all, w_ada[l], b_ada[l].reshape(1, -1))
    mod_p = [m.reshape(nb, 1, d) for m in jnp.split(mod[:nb], 6, axis=-1)]
    mod_s = jnp.split(mod[nb:nb + ns], 6, axis=-1)

    cos_p, sin_p = _rope_tables(jnp.arange(s, dtype=jnp.int32))
    cos_s, sin_s = _rope_tables(PAST_LEN + jnp.arange(t_new, dtype=jnp.int32))
    w_in_bf = bf(w_in[l])
    ffn_w = (row(g_out_a[l]), row(g_out_b[l]), bf(w_o[l]), row(g_ffn[l]), bf(w_gate[l]), bf(w_up[l]), bf(w_down[l]),
             row(g_final))

    wa_p, wb_p = min(A_WINDOW, s), min(B_WINDOW, s)
    (qa, ka, va, q1, k1, v1, q4, k4, v4, q16, k16, v16, ka_t, va_t, kb_t, vb_t) = _inproj_prompt(
        x_prompt, mod_p[0], mod_p[1], row(g_attn[l]), w_in_bf, cos_p, sin_p, tm=512, tail_a=wa_p, tail_b=wb_p)
    xs_t = x_sample.transpose(1, 0, 2)
    qa_s, ka_s, va_s, qb_s, kb_s, vb_s = _inproj_sample(
        xs_t, mod_s[0], mod_s[1], row(g_attn[l]), w_in_bf, cos_s.reshape(t_new, 1, LANES),
        sin_s.reshape(t_new, 1, LANES))

    oa, la = _swa_prompt(qa, ka, va, sinks[l])
    obs, mls = [], []
    for (q, k, v), (_, dil) in zip(((q1, k1, v1), (q4, k4, v4), (q16, k16, v16)), B_PATTERNS):
        o, ml = _band_attn(q, k, v, dil)
        obs.append(o)
        mls.append(ml)

    qa_g = (qa_s.reshape(t_new, ns, A_KV_HEADS, A_GROUP, HEAD_DIM).transpose(1, 2, 0, 3, 4)
            .reshape(ns, A_KV_HEADS, t_new * A_GROUP, HEAD_DIM))
    kv_g = lambda t: t.reshape(t_new, ns, A_KV_HEADS, HEAD_DIM).transpose(1, 2, 0, 3)
    qb_h = qb_s.reshape(t_new, ns, 2, B_W // 2).transpose(1, 2, 0, 3)
    sample_ins = (qa_g, kv_g(ka_s), kv_g(va_s), _window_on_lanes(cache_a_k), _window_on_lanes(cache_a_v),
                  _halves_last_rows(ka_s, LANES), _halves_last_rows(va_s, LANES), qb_h,
                  _halves_last_rows(kb_s, B_W // 2), _halves_last_rows(vb_s, B_W // 2),
                  _window_on_lanes(cache_b_k), _window_on_lanes(cache_b_v))
    y_prompt, oa_g, ob_h, na_k, na_v, nb_k, nb_v = _ffn_and_sample(
        x_prompt, oa, la, obs, mls, mod_p[2:6], *ffn_w, sinks[l], sample_ins, tm=256)
    pa_k = ka_t.reshape(1, nb, wa_p, A_KV_HEADS, HEAD_DIM)
    pa_v = va_t.reshape(1, nb, wa_p, A_KV_HEADS, HEAD_DIM)
    pb_k = kb_t.reshape(1, nb, wb_p, B_HEADS, HEAD_DIM)
    pb_v = vb_t.reshape(1, nb, wb_p, B_HEADS, HEAD_DIM)

    oa_t = (oa_g.reshape(ns, A_KV_HEADS, t_new, A_GROUP, HEAD_DIM).transpose(2, 0, 1, 3, 4)
            .reshape(t_new, ns, A_QW))
    ob_t = ob_h.transpose(2, 0, 1, 3).reshape(t_new, ns, B_W)
    y_s = _merge_ffn_sample(xs_t, oa_t, ob_t, mod_s[2:6], *ffn_w)
    y_sample = y_s.transpose(1, 0, 2)
    sa_k, sa_v = _window_off_lanes(na_k, A_KV_HEADS), _window_off_lanes(na_v, A_KV_HEADS)
    sb_k, sb_v = _window_off_lanes(nb_k, B_HEADS), _window_off_lanes(nb_v, B_HEADS)

    return (y_prompt, y_sample, pa_k, pa_v, pb_k, pb_v, sa_k, sa_v, sb_k, sb_v)
```

```python
import functools
import math

import jax
import jax.numpy as jnp
from jax import lax
from jax.experimental import pallas as pl
from jax.experimental.pallas import tpu as pltpu

HEAD_DIM = 64
A_Q_HEADS = 8
A_KV_HEADS = 2
A_GROUP = A_Q_HEADS // A_KV_HEADS
B_HEADS = 8
A_WINDOW = 128
BLOCK = 128
B_PATTERNS = ((128, 1), (512, 4), (2048, 16))
B_WINDOW = 2048
PAST_LEN = 8192
ROPE_THETA = 10000.0
EPS = 1e-6
A_QW = A_Q_HEADS * HEAD_DIM
A_KVW = A_KV_HEADS * HEAD_DIM
B_W = B_HEADS * HEAD_DIM
SCALE = HEAD_DIM ** -0.5
LANES = 128
SUBLANES = 8
NEG = -1e30
VMEM_LIMIT = 56 * 1024 * 1024
Q_BLOCKS = 16

_BF = jnp.bfloat16
_F32 = jnp.float32


def _params(sem, vmem=VMEM_LIMIT):
    return pltpu.CompilerParams(dimension_semantics=sem, vmem_limit_bytes=vmem)


def _resident(shape):
    nd = len(shape)
    return pl.BlockSpec(shape, lambda *_: (0,) * nd, pipeline_mode=pl.Buffered(1))


def _rms(x, g):
    return x * lax.rsqrt(jnp.mean(x * x, axis=-1, keepdims=True) + EPS) * g


def _log2(n):
    assert n > 0 and n & (n - 1) == 0, "power of two expected"
    return n.bit_length() - 1


def _div(x, n):
    return x >> _log2(n)


def _multiple(x, n):
    _log2(n)
    return (x & (n - 1)) == 0


def _adaln_kernel(c_ref, w_ref, b_ref, o_ref):
    c = c_ref[...]
    a = (c * jax.nn.sigmoid(c)).astype(_BF)
    o_ref[...] = jnp.dot(a, w_ref[...].astype(_BF), preferred_element_type=_F32) + b_ref[...]


def _adaln(c_all, w, b):
    m, d = c_all.shape
    n = w.shape[1]
    tn = 768
    return pl.pallas_call(
        _adaln_kernel,
        grid=(n // tn,),
        in_specs=[pl.BlockSpec((m, d), lambda j: (0, 0)),
                  pl.BlockSpec((d, tn), lambda j: (0, j)),
                  pl.BlockSpec((1, tn), lambda j: (0, j))],
        out_specs=pl.BlockSpec((m, tn), lambda j: (0, j)),
        out_shape=jax.ShapeDtypeStruct((m, n), _F32),
        compiler_params=_params(("arbitrary",)),
        name="adaln",
    )(c_all, w, b)


def _rope(x, cos, sin_signed, first_half):
    sw = jnp.where(first_half, pltpu.roll(x, 96, 1), pltpu.roll(x, 32, 1))
    return x * cos + sw * sin_signed


def _project(x_ref, sh_ref, sc_ref, g_ref, w_ref, cos_ref, sin_ref):
    d = x_ref.shape[-1]
    x = x_ref[...].reshape(-1, d)
    sh = sh_ref[...].reshape(-1, d)
    sc = sc_ref[...].reshape(-1, d)
    h = _rms(x, g_ref[...]) * (1.0 + sc) + sh
    proj = jnp.dot(h.astype(_BF), w_ref[...], preferred_element_type=_F32)
    cos = cos_ref[...].reshape(-1, LANES)
    sin = sin_ref[...].reshape(-1, LANES)
    lane = lax.broadcasted_iota(jnp.int32, (1, LANES), 1)
    first_half = (lane & (HEAD_DIM - 1)) < (HEAD_DIM // 2)

    def rot(col0, ncols):
        parts = [_rope(proj[:, col0 + j * LANES: col0 + (j + 1) * LANES], cos, sin, first_half)
                 for j in range(ncols // LANES)]
        return parts[0] if len(parts) == 1 else jnp.concatenate(parts, axis=1)

    o = 0
    qa = rot(o, A_QW) * SCALE; o += A_QW
    ka = rot(o, A_KVW); o += A_KVW
    va = proj[:, o:o + A_KVW]; o += A_KVW
    qb = rot(o, B_W) * SCALE; o += B_W
    kb = rot(o, B_W); o += B_W
    vb = proj[:, o:o + B_W]
    return qa, ka, va, qb, kb, vb


def _put(ref, val):
    ref[...] = val.astype(ref.dtype).reshape(ref.shape)


def _inproj_prompt_kernel(x_ref, sh_ref, sc_ref, g_ref, w_ref, cos_ref, sin_ref,
                          qa_ref, ka_ref, va_ref, q1_ref, k1_ref, v1_ref, q4_ref, k4_ref, v4_ref,
                          q16_ref, k16_ref, v16_ref, kat_ref, vat_ref, kbt_ref, vbt_ref,
                          sq_ref, sk_ref, sv_ref):
    qa, ka, va, qb, kb, vb = _project(x_ref, sh_ref, sc_ref, g_ref, w_ref, cos_ref, sin_ref)
    tm = qa.shape[0]
    _put(qa_ref, qa)
    lo = lax.broadcasted_iota(jnp.int32, (1, LANES), 1) < HEAD_DIM
    for ref, val in ((ka_ref, ka), (va_ref, va)):
        swapped = pltpu.roll(val, HEAD_DIM, 1)
        _put(ref, jnp.concatenate([jnp.where(lo, val, swapped), jnp.where(lo, swapped, val)], axis=1))
    _put(q1_ref, qb); _put(k1_ref, kb); _put(v1_ref, vb)
    ta = kat_ref.shape[1]
    _put(kat_ref, ka[tm - ta:]); _put(vat_ref, va[tm - ta:])
    _put(kbt_ref, kb); _put(vbt_ref, vb)
    for val, stage, r4, r16 in ((qb, sq_ref, q4_ref, q16_ref), (kb, sk_ref, k4_ref, k16_ref),
                                (vb, sv_ref, v4_ref, v16_ref)):
        for c in range(B_W // LANES):
            stage[c] = val[:, c * LANES:(c + 1) * LANES]
        for dil, out in ((4, r4), (16, r16)):
            n = tm // dil
            for r in range(dil):
                for c in range(B_W // LANES):
                    out[0, r, :, c * LANES:(c + 1) * LANES] = stage[c, pl.ds(r, n, stride=dil), :].astype(out.dtype)


def _inproj_prompt(x, sh, sc, g, w_bf, cos, sin, *, tm, tail_a, tail_b):
    b, s, d = x.shape
    nt = s // tm
    assert tail_b % tm == 0 and tail_a <= tm
    row = lambda c: pl.BlockSpec((1, tm, c), lambda bb, i: (bb, i, 0))
    res = lambda dil: pl.BlockSpec((1, dil, tm // dil, B_W), lambda bb, i: (bb, 0, i, 0))
    mod_spec = pl.BlockSpec((1, 1, d), lambda bb, i: (bb, 0, 0))
    tab_spec = pl.BlockSpec((tm, LANES), lambda bb, i: (i, 0))
    tail_a_spec = pl.BlockSpec((1, tail_a, A_KVW), lambda bb, i: (bb, 0, 0))
    tail_b_spec = pl.BlockSpec((1, tm, B_W), lambda bb, i: (bb, jnp.maximum(i - (nt - tail_b // tm), 0), 0))
    shp = lambda c, dt: jax.ShapeDtypeStruct((b, s, c), dt)
    rshp = lambda dil: jax.ShapeDtypeStruct((b, dil, s // dil, B_W), _BF)
    return pl.pallas_call(
        _inproj_prompt_kernel,
        grid=(b, nt),
        in_specs=[row(d), mod_spec, mod_spec, _resident((1, d)), _resident(w_bf.shape), tab_spec, tab_spec],
        out_specs=[row(A_QW), row(2 * A_KVW), row(2 * A_KVW)] + [res(1)] * 3 + [res(4)] * 3 + [res(16)] * 3
                  + [tail_a_spec] * 2 + [tail_b_spec] * 2,
        out_shape=[shp(A_QW, _BF), shp(2 * A_KVW, _BF), shp(2 * A_KVW, _BF)] + [rshp(1)] * 3 + [rshp(4)] * 3 + [rshp(16)] * 3
                  + [jax.ShapeDtypeStruct((b, tail_a, A_KVW), _F32)] * 2
                  + [jax.ShapeDtypeStruct((b, tail_b, B_W), _F32)] * 2,
        scratch_shapes=[pltpu.VMEM((B_W // LANES, tm, LANES), _F32)] * 3,
        compiler_params=_params(("parallel", "arbitrary")),
        name="in_proj_prompt",
    )(x, sh, sc, g, w_bf, cos, sin)


def _inproj_sample_kernel(x_ref, sh_ref, sc_ref, g_ref, w_ref, cos_ref, sin_ref,
                          qa_ref, ka_ref, va_ref, qb_ref, kb_ref, vb_ref):
    vals = _project(x_ref, sh_ref, sc_ref, g_ref, w_ref, cos_ref, sin_ref)
    for ref, val in zip((qa_ref, ka_ref, va_ref, qb_ref, kb_ref, vb_ref), vals):
        _put(ref, val)


def _inproj_sample(x, sh, sc, g, w_bf, cos, sin):
    t, n, d = x.shape
    row = lambda c: pl.BlockSpec((1, n, c), lambda j: (j, 0, 0))
    mod_spec = pl.BlockSpec((n, d), lambda j: (0, 0))
    tab_spec = pl.BlockSpec((1, 1, LANES), lambda j: (j, 0, 0))
    widths = (A_QW, A_KVW, A_KVW, B_W, B_W, B_W)
    return pl.pallas_call(
        _inproj_sample_kernel,
        grid=(t,),
        in_specs=[row(d), mod_spec, mod_spec, _resident((1, d)), _resident(w_bf.shape), tab_spec, tab_spec],
        out_specs=[row(c) for c in widths],
        out_shape=[jax.ShapeDtypeStruct((t, n, c), _F32) for c in widths],
        compiler_params=_params(("parallel",)),
        name="in_proj_sample",
    )(x, sh, sc, g, w_bf, cos, sin)


def _band_bias(max_dist, drop_prev):
    qi = lax.broadcasted_iota(jnp.int32, (BLOCK, 2 * BLOCK), 0)
    ki = lax.broadcasted_iota(jnp.int32, (BLOCK, 2 * BLOCK), 1)
    rel = qi + BLOCK - ki
    valid = (rel >= 0) & (rel <= max_dist)
    if drop_prev is not None:
        valid = valid & ((ki >= BLOCK) | jnp.logical_not(drop_prev))
    return jnp.where(valid, 0.0, NEG).astype(_F32)


def _kv_window(prev_ref, cur_ref, j, lead, cols):
    if j == 0:
        return jnp.concatenate([prev_ref[lead + (slice(None), cols)], cur_ref[lead + (slice(0, BLOCK), cols)]], axis=0)
    return cur_ref[lead + (slice((j - 1) * BLOCK, (j + 1) * BLOCK), cols)]


def _stat_columns(stats, cols, acc):
    lane = lax.broadcasted_iota(jnp.int32, (1, acc.shape[1]), 1)
    for col, stat in zip(cols, stats):
        acc = jnp.where(lane == col, stat, acc)
    return acc


def _band_attn_kernel(q_ref, kp_ref, kc_ref, vp_ref, vc_ref, o_ref, ml_ref, *, max_dist, nq):
    first_step = pl.program_id(2) == 0
    bias = _band_bias(max_dist, None)
    bias0 = _band_bias(max_dist, first_step)
    lane = lax.broadcasted_iota(jnp.int32, (1, LANES), 1)
    lo = lane < HEAD_DIM
    zero = jnp.zeros((), _BF)
    for rr, j in [(rr, j) for rr in range(q_ref.shape[1]) for j in range(nq)]:
        rows = slice(j * BLOCK, (j + 1) * BLOCK)
        bj = bias0 if j == 0 else bias
        bj = jnp.concatenate([bj, bj], axis=0)
        ml_all = jnp.zeros((BLOCK, LANES), _F32)
        for c in range(B_W // LANES):
            cs = slice(c * LANES, (c + 1) * LANES)
            q2 = q_ref[0, rr, rows, cs]
            k2 = _kv_window(kp_ref, kc_ref, j, (0, rr), cs)
            v2 = _kv_window(vp_ref, vc_ref, j, (0, rr), cs)
            qs = jnp.concatenate([jnp.where(lo, q2, zero), jnp.where(lo, zero, q2)], axis=0)
            s = lax.dot_general(qs, k2, (((1,), (1,)), ((), ())), preferred_element_type=_F32) + bj
            m = jnp.max(s, axis=-1, keepdims=True)
            p = jnp.exp(s - m)
            l = jnp.sum(p, axis=-1, keepdims=True)
            pv = jnp.dot(p.astype(_BF), v2, preferred_element_type=_F32)
            o_ref[0, rr, rows, cs] = jnp.where(lo, pv[:BLOCK], pv[BLOCK:]).astype(o_ref.dtype)
            ml_all = _stat_columns((m[:BLOCK], m[BLOCK:], l[:BLOCK], l[BLOCK:]),
                                   (2 * c, 2 * c + 1, B_HEADS + 2 * c, B_HEADS + 2 * c + 1), ml_all)
        ml_ref[0, rr, rows, :] = ml_all


def _band_attn(q, k, v, dil):
    b, _, m, w = q.shape
    nq = min(Q_BLOCKS, m // BLOCK)
    tq = nq * BLOCK
    rb = min(dil, Q_BLOCKS // nq)
    cur = pl.BlockSpec((1, rb, tq, w), lambda bb, r, i: (bb, r, i, 0))
    prev = pl.BlockSpec((1, rb, BLOCK, w), lambda bb, r, i: (bb, r, jnp.maximum(i * nq - 1, 0), 0))
    window, _ = [p for p in B_PATTERNS if p[1] == dil][0]
    return pl.pallas_call(
        functools.partial(_band_attn_kernel, max_dist=window // dil, nq=nq),
        grid=(b, dil // rb, m // tq),
        in_specs=[cur, prev, cur, prev, cur],
        out_specs=[cur, pl.BlockSpec((1, rb, tq, LANES), lambda bb, r, i: (bb, r, i, 0))],
        out_shape=[jax.ShapeDtypeStruct((b, dil, m, w), _BF),
                   jax.ShapeDtypeStruct((b, dil, m, LANES), _F32)],
        compiler_params=_params(("parallel", "parallel", "arbitrary")),
        name=f"band_attn_d{dil}",
    )(q, k, k, v, v)


def _swa_kernel(sink_ref, q_ref, kp_ref, kc_ref, vp_ref, vc_ref, o_ref, l_ref, *, nq):
    first_step = pl.program_id(1) == 0
    bias = _band_bias(A_WINDOW - 1, None)
    bias0 = _band_bias(A_WINDOW - 1, first_step)
    lane = lax.broadcasted_iota(jnp.int32, (1, LANES), 1)
    lo = lane < HEAD_DIM
    zero = jnp.zeros((), _BF)
    col0 = lax.broadcasted_iota(jnp.int32, (1, 2 * BLOCK), 1) == 0
    for j in range(nq):
        rows = slice(j * BLOCK, (j + 1) * BLOCK)
        bj = bias0 if j == 0 else bias
        bj = jnp.concatenate([bj, bj], axis=0)
        l_all = jnp.zeros((BLOCK, A_Q_HEADS), _F32)
        for c in range(A_QW // LANES):
            g = (2 * c) // A_GROUP
            gs = slice(g * LANES, (g + 1) * LANES)
            q2 = q_ref[0, rows, c * LANES:(c + 1) * LANES]
            k2 = _kv_window(kp_ref, kc_ref, j, (0,), gs)
            v2 = _kv_window(vp_ref, vc_ref, j, (0,), gs)
            qs = jnp.concatenate([jnp.where(lo, q2, zero), jnp.where(lo, zero, q2)], axis=0)
            s = lax.dot_general(qs, k2, (((1,), (1,)), ((), ())), preferred_element_type=_F32) + bj
            s = jnp.concatenate([jnp.where(col0, sink_ref[2 * c + half], s[half * BLOCK:(half + 1) * BLOCK])
                                 for half in range(2)], axis=0)
            m = jnp.max(s, axis=-1, keepdims=True)
            p = jnp.exp(s - m)
            l = jnp.sum(p, axis=-1, keepdims=True)
            p = jnp.where(col0, 0.0, p).astype(_BF)
            pv = jnp.dot(p, v2, preferred_element_type=_F32)
            _put(o_ref.at[0, rows, c * LANES:(c + 1) * LANES], jnp.where(lo, pv[:BLOCK], pv[BLOCK:]))
            l_all = _stat_columns((l[:BLOCK], l[BLOCK:]), (2 * c, 2 * c + 1), l_all)
        l_ref[0, rows, :] = l_all


def _swa_prompt(q, k, v, sinks):
    b, s, _ = q.shape
    nq = Q_BLOCKS
    tq = nq * BLOCK
    assert A_WINDOW <= BLOCK, "the sink logit borrows a key column that no query row can reach"
    qspec = pl.BlockSpec((1, tq, A_QW), lambda bb, i: (bb, i, 0))
    cur = pl.BlockSpec((1, tq, 2 * A_KVW), lambda bb, i: (bb, i, 0))
    prev = pl.BlockSpec((1, BLOCK, 2 * A_KVW), lambda bb, i: (bb, jnp.maximum(i * nq - 1, 0), 0))
    return pl.pallas_call(
        functools.partial(_swa_kernel, nq=nq),
        grid=(b, s // tq),
        in_specs=[pl.BlockSpec(memory_space=pltpu.SMEM), qspec, prev, cur, prev, cur],
        out_specs=[qspec, pl.BlockSpec((1, tq, A_Q_HEADS), lambda bb, i: (bb, i, 0))],
        out_shape=[jax.ShapeDtypeStruct((b, s, A_QW), _BF), jax.ShapeDtypeStruct((b, s, A_Q_HEADS), _F32)],
        compiler_params=_params(("parallel", "arbitrary")),
        name="swa_prompt",
    )(sinks, q, k, k, v, v)


def _roll_in(x_ref, new8_ref, o_ref, t_new):
    rows, width = x_ref.shape
    new8 = new8_ref[0, 0]
    tail = jnp.concatenate([jnp.zeros((LANES - new8.shape[0], new8.shape[1]), _F32), new8], axis=0).T[:rows]
    lane = lax.broadcasted_iota(jnp.int32, (1, LANES), 1)
    keep = lane < LANES - t_new
    nxt = pltpu.roll(x_ref[:, 0:LANES], LANES - t_new, 1)
    for c in range(width // LANES):
        cur = nxt
        if c + 1 < width // LANES:
            nxt = pltpu.roll(x_ref[:, (c + 1) * LANES:(c + 2) * LANES], LANES - t_new, 1)
        else:
            nxt = tail
        o_ref[:, c * LANES:(c + 1) * LANES] = jnp.where(keep, cur, nxt)


def _sample_half(g, refs, *, t_new):
    (sink_ref, qa_ref, kan_ref, van_ref, cak_ref, cav_ref, ak8_ref, av8_ref,
     qb_ref, kb8_ref, vb8_ref, cbk_ref, cbv_ref,
     oa_ref, ob_ref, nak_ref, nav_ref, nbk_ref, nbv_ref) = refs
    wa, wb = cak_ref.shape[-1], cbk_ref.shape[-1]

    rows_a = t_new * A_GROUP
    ra = lax.broadcasted_iota(jnp.int32, (rows_a, 1), 0)
    qi_a = _div(ra, A_GROUP)
    rel_a = wa + qi_a - lax.broadcasted_iota(jnp.int32, (1, wa), 1)
    valid_a = (rel_a >= 0) & (rel_a < A_WINDOW)
    q = qa_ref[0, 0]
    s = jnp.dot(q.astype(_BF), cak_ref[0].astype(_BF), preferred_element_type=_F32)
    s = jnp.where(valid_a, s, NEG)
    s_new = [jnp.where(qi_a >= j, jnp.sum(q * kan_ref[0, 0, j:j + 1, :], axis=-1, keepdims=True), NEG)
             for j in range(t_new)]
    sink = jnp.zeros((rows_a, 1), _F32)
    for u in range(A_GROUP):
        sink = jnp.where((ra & (A_GROUP - 1)) == u, sink_ref[g * A_GROUP + u], sink)
    m = functools.reduce(jnp.maximum, [jnp.max(s, axis=-1, keepdims=True), sink] + s_new)
    p = jnp.exp(s - m)
    p_new = [jnp.exp(z - m) for z in s_new]
    l = functools.reduce(jnp.add, [jnp.sum(p, axis=-1, keepdims=True), jnp.exp(sink - m)] + p_new)
    o = lax.dot_general(p.astype(_BF), cav_ref[0].astype(_BF), (((1,), (1,)), ((), ())),
                        preferred_element_type=_F32)
    for j in range(t_new):
        o = o + p_new[j] * van_ref[0, 0, j:j + 1, :]
    oa_ref[0, 0] = o * (1.0 / l)
    _roll_in(cak_ref.at[0], ak8_ref, nak_ref.at[0], t_new)
    _roll_in(cav_ref.at[0], av8_ref, nav_ref.at[0], t_new)

    width_b = cbk_ref.shape[1]
    r0 = kb8_ref.shape[2] - t_new
    rows_b = t_new * SUBLANES
    sub = lax.broadcasted_iota(jnp.int32, (SUBLANES, width_b), 0)
    own = _div(lax.broadcasted_iota(jnp.int32, (SUBLANES, width_b), 1), HEAD_DIM) == sub
    qbd = jnp.concatenate([jnp.where(own, qb_ref[0, 0, i:i + 1, :], 0.0) for i in range(t_new)], axis=0)
    s_all = jnp.dot(qbd.astype(_BF), cbk_ref[0].astype(_BF), preferred_element_type=_F32)
    qi = _div(lax.broadcasted_iota(jnp.int32, (rows_b, 1), 0), SUBLANES)
    s_new = [jnp.sum(qbd * kb8_ref[0, 0, r0 + j:r0 + j + 1, :], axis=-1, keepdims=True) for j in range(t_new)]
    pats = []
    for window, dil in B_PATTERNS:
        lo_lane = max(wb - (-(-window // LANES) * LANES), 0)
        wl = lo_lane + lax.broadcasted_iota(jnp.int32, (1, wb - lo_lane), 1)
        rel = wb + qi - wl
        valid = (rel <= window) & _multiple(rel, dil)
        s = jnp.where(valid, s_all[:, lo_lane:], NEG)
        sn = [jnp.where((qi - j >= 0) & _multiple(qi - j, dil), s_new[j], NEG) for j in range(t_new)]
        m = functools.reduce(jnp.maximum, [jnp.max(s, axis=-1, keepdims=True)] + sn)
        p = jnp.exp(s - m)
        pn = [jnp.exp(z - m) for z in sn]
        l = functools.reduce(jnp.add, [jnp.sum(p, axis=-1, keepdims=True)] + pn)
        pats.append((lo_lane, p, pn, l, m + jnp.log(l)))
    mx = functools.reduce(jnp.maximum, [t[4] for t in pats])
    es = [jnp.exp(t[4] - mx) for t in pats]
    den = functools.reduce(jnp.add, es)
    coef = [e / (den * t[3]) for e, t in zip(es, pats)]
    starts = sorted({t[0] for t in pats} | {wb})
    blocks = []
    for a, b in zip(starts[:-1], starts[1:]):
        acc = None
        for c, (lo_lane, p, _, _, _) in zip(coef, pats):
            if lo_lane <= a:
                term = c * p[:, a - lo_lane:b - lo_lane]
                acc = term if acc is None else acc + term
        blocks.append(acc)
    lead = starts[0]
    p_comb = jnp.concatenate(blocks, axis=1) if len(blocks) > 1 else blocks[0]
    o_full = lax.dot_general(p_comb.astype(_BF), cbv_ref[0, :, lead:].astype(_BF), (((1,), (1,)), ((), ())),
                             preferred_element_type=_F32)
    for j in range(t_new):
        pj = functools.reduce(jnp.add, [c * t[2][j] for c, t in zip(coef, pats)])
        o_full = o_full + pj * vb8_ref[0, 0, r0 + j:r0 + j + 1, :]
    for i in range(t_new):
        blk = jnp.where(own, o_full[i * SUBLANES:(i + 1) * SUBLANES], 0.0)
        ob_ref[0, 0, i:i + 1, :] = jnp.sum(blk, axis=0, keepdims=True)
    _roll_in(cbk_ref.at[0], kb8_ref, nbk_ref, t_new)
    _roll_in(cbv_ref.at[0], vb8_ref, nbv_ref, t_new)


def _ffn_chunk(h, wg_ref, wu_ref, wd_ref, cs):
    gt = jnp.dot(h, wg_ref[:, cs], preferred_element_type=_F32)
    up = jnp.dot(h, wu_ref[:, cs], preferred_element_type=_F32)
    act = (gt * jax.nn.sigmoid(gt) * up).astype(_BF)
    return jnp.dot(act, wd_ref[cs, :], preferred_element_type=_F32)


def _attn_residual(x, oa, ob, gta, shf, scf, goa_ref, gob_ref, wo_ref, gf_ref):
    merged = jnp.concatenate([_rms(oa, goa_ref[...]), _rms(ob, gob_ref[...])], axis=1)
    x1 = x + gta * jnp.dot(merged.astype(_BF), wo_ref[...], preferred_element_type=_F32)
    return x1, (_rms(x1, gf_ref[...]) * (1.0 + scf) + shf).astype(_BF)


_TWO = lambda r: r[...].reshape(-1, r.shape[-1])
_FF_CUTS = (0, 384, 1152, 2048, 2816)
_PHASES = len(_FF_CUTS) - 1


def _ffn_and_sample_kernel(*refs, t_new):
    (x_ref, oa_ref, la_ref, o1_ref, o4_ref, o16_ref, ml1_ref, ml2_ref, ml3_ref, e_ref,
     gta_ref, shf_ref, scf_ref, gtf_ref, goa_ref, gob_ref, wo_ref, gf_ref, wg_ref, wu_ref, wd_ref, gfin_ref) = refs[:22]
    sample_in = refs[22:35]
    y_ref = refs[35]
    oa_s_ref, ob_s_ref, nak_ref, nav_ref, nbk_hbm, nbv_hbm = refs[36:42]
    s4_ref, s16_ref, x1_ref, h_ref, acc_ref, kbuf_ref, vbuf_ref, out_sem, t4_ref, t16_ref = refs[42:]
    step = pl.program_id(0)
    half = step % 2
    phase = step % _PHASES
    tm = x_ref.shape[1]

    slot = step % 2
    rows_b = kbuf_ref.shape[1]

    def out_copies(slot_, step_):
        dst = (step_ // 2, pl.ds(pl.multiple_of((step_ % 2) * rows_b, rows_b), rows_b), slice(None))
        return (pltpu.make_async_copy(kbuf_ref.at[slot_], nbk_hbm.at[dst], out_sem.at[0, slot_]),
                pltpu.make_async_copy(vbuf_ref.at[slot_], nbv_hbm.at[dst], out_sem.at[1, slot_]))

    @pl.when(step >= 2)
    def _():
        for cp in out_copies(slot, step - 2):
            cp.wait()

    sample_out = (oa_s_ref, ob_s_ref, nak_ref, nav_ref, kbuf_ref.at[slot], vbuf_ref.at[slot])

    def mix_and_project():
        for dil, src, stage, stat_src, stat_stage in ((4, o4_ref, s4_ref, ml2_ref, t4_ref),
                                                      (16, o16_ref, s16_ref, ml3_ref, t16_ref)):
            for r in range(dil):
                rows_r = pl.ds(r, tm // dil, stride=dil)
                stat_stage[rows_r, :] = stat_src[0, r]
                for c in range(B_W // LANES):
                    stage[c, rows_r, :] = src[0, r, :, c * LANES:(c + 1) * LANES].astype(_F32)
        gather = lambda stage: jnp.concatenate([stage[c] for c in range(B_W // LANES)], axis=1)
        outs = [_TWO(o1_ref).astype(_F32), gather(s4_ref), gather(s16_ref)]

        def widen(per_head):
            hi = per_head.astype(_BF)
            lo = (per_head - hi.astype(_F32)).astype(_BF)
            return (jnp.dot(hi, e_ref[...], preferred_element_type=_F32)
                    + jnp.dot(lo, e_ref[...], preferred_element_type=_F32))

        stats = (ml1_ref.at[0, 0], t4_ref, t16_ref)
        ms = [r[:, 0:B_HEADS] for r in stats]
        ls = [r[:, B_HEADS:2 * B_HEADS] for r in stats]
        lses = [m + jnp.log(l) for m, l in zip(ms, ls)]
        mx = jnp.maximum(jnp.maximum(lses[0], lses[1]), lses[2])
        es = [jnp.exp(z - mx) for z in lses]
        den = es[0] + es[1] + es[2]
        ob = jnp.zeros((tm, B_W), _F32)
        for e, l, o in zip(es, ls, outs):
            ob = ob + widen(e / (den * l)) * o
        oa = widen(1.0 / _TWO(la_ref)) * _TWO(oa_ref).astype(_F32)
        x1, h = _attn_residual(_TWO(x_ref), oa, ob, _TWO(gta_ref), _TWO(shf_ref), _TWO(scf_ref),
                               goa_ref, gob_ref, wo_ref, gf_ref)
        x1_ref[...] = x1
        h_ref[...] = h

    for k in range(_PHASES):
        @pl.when(phase == k)
        def _(k=k):
            _sample_half(half, tuple(sample_in) + tuple(sample_out), t_new=t_new)
            if k == 0:
                mix_and_project()
            part = _ffn_chunk(h_ref[...], wg_ref, wu_ref, wd_ref, slice(_FF_CUTS[k], _FF_CUTS[k + 1]))
            if k == 0:
                acc_ref[...] = part
            elif k < _PHASES - 1:
                acc_ref[...] += part
            else:
                x2 = x1_ref[...] + _TWO(gtf_ref) * (acc_ref[...] + part)
                y_ref[...] = _rms(x2, gfin_ref[...]).reshape(y_ref.shape)

    for cp in out_copies(slot, step):
        cp.start()

    @pl.when(step == pl.num_programs(0) - 1)
    def _():
        for cp in out_copies(1 - slot, step - 1) + out_copies(slot, step):
            cp.wait()


def _ffn_and_sample(x, oa, la, obs, mls, mods, goa, gob, wo, gf, wg, wu, wd, gfin, sinks, sample_ins, *, tm):
    b, s, d = x.shape
    n = sample_ins[0].shape[0]
    tiles = s // tm
    steps = 2 * n
    assert steps == b * tiles * _PHASES and _PHASES % 2 == 0 and wg.shape[1] == _FF_CUTS[-1]
    t_new = sample_ins[0].shape[2] // A_GROUP

    def tile(j):
        t = j // _PHASES
        return t // tiles, t % tiles

    row = lambda c: pl.BlockSpec((1, tm, c), lambda j: tile(j) + (0,))
    res = lambda dil: pl.BlockSpec((1, dil, tm // dil, B_W), lambda j: (tile(j)[0], 0, tile(j)[1], 0))
    mod_spec = pl.BlockSpec((1, 1, d), lambda j: (tile(j)[0], 0, 0))
    expand = (jnp.arange(B_W)[None, :] // HEAD_DIM == jnp.arange(B_HEADS)[:, None]).astype(_BF)
    ffn_ins = [x, oa, la] + list(obs) + list(mls) + [expand] + list(mods) + [goa, gob, wo, gf, wg, wu, wd, gfin]
    ffn_specs = ([row(d), row(A_QW), row(A_Q_HEADS)] + [res(o.shape[1]) for o in obs]
                 + [pl.BlockSpec((1, o.shape[1], tm // o.shape[1], LANES), lambda j: (tile(j)[0], 0, tile(j)[1], 0))
                    for o in obs]
                 + [_resident(expand.shape)] + [mod_spec] * 4
                 + [_resident(a.shape) for a in (goa, gob, wo, gf, wg, wu, wd, gfin)])
    half_blk = lambda a: pl.BlockSpec((1, 1) + a.shape[2:], lambda j: (j // 2, j % 2) + (0,) * (a.ndim - 2))
    (qa_g, ka_n, va_n, ca_k, ca_v, ak8, av8, qb_h, kb8, vb8, cb_k, cb_v) = sample_ins
    sample_outs = [jax.ShapeDtypeStruct(a.shape, _F32) for a in (qa_g, qb_h, ca_k, ca_v, cb_k, cb_v)]
    cache_blk = lambda a: pl.BlockSpec((1, a.shape[1] // 2, a.shape[2]), lambda j: (j // 2, j % 2, 0))
    s_specs = [half_blk(qa_g), half_blk(ka_n), half_blk(va_n), cache_blk(ca_k), cache_blk(ca_v), half_blk(ak8),
               half_blk(av8), half_blk(qb_h), half_blk(kb8), half_blk(vb8), cache_blk(cb_k), cache_blk(cb_v)]
    in_hbm = pl.BlockSpec(memory_space=pl.ANY)
    o_specs = [half_blk(qa_g), half_blk(qb_h), cache_blk(ca_k), cache_blk(ca_v), in_hbm, in_hbm]
    stage = pltpu.VMEM((B_W // LANES, tm, LANES), _F32)
    out_buf = pltpu.VMEM((2, cb_k.shape[1] // 2, cb_k.shape[2]), _F32)
    outs = pl.pallas_call(
        functools.partial(_ffn_and_sample_kernel, t_new=t_new),
        grid=(steps,),
        in_specs=ffn_specs + [pl.BlockSpec(memory_space=pltpu.SMEM)] + s_specs,
        out_specs=[row(d)] + o_specs,
        out_shape=[jax.ShapeDtypeStruct((b, s, d), _F32)] + sample_outs,
        scratch_shapes=[stage, stage, pltpu.VMEM((tm, d), _F32), pltpu.VMEM((tm, d), _BF), pltpu.VMEM((tm, d), _F32),
                        out_buf, out_buf, pltpu.SemaphoreType.DMA((2, 2)),
                        pltpu.VMEM((tm, LANES), _F32), pltpu.VMEM((tm, LANES), _F32)],
        compiler_params=_params(("arbitrary",)),
        name="ffn_and_sample",
    )(*ffn_ins, sinks, *sample_ins)
    return outs


def _merge_ffn_sample_kernel(x_ref, oa_ref, ob_ref, gta_ref, shf_ref, scf_ref, gtf_ref,
                             goa_ref, gob_ref, wo_ref, gf_ref, wg_ref, wu_ref, wd_ref, gfin_ref, y_ref):
    per_row = lambda r: jnp.concatenate([r[...]] * x_ref.shape[0], axis=0)
    x1, h = _attn_residual(_TWO(x_ref), _TWO(oa_ref), _TWO(ob_ref), per_row(gta_ref), per_row(shf_ref),
                           per_row(scf_ref), goa_ref, gob_ref, wo_ref, gf_ref)
    acc = jnp.zeros_like(x1)
    for k in range(_PHASES):
        acc = acc + _ffn_chunk(h, wg_ref, wu_ref, wd_ref, slice(_FF_CUTS[k], _FF_CUTS[k + 1]))
    y_ref[...] = _rms(x1 + per_row(gtf_ref) * acc, gfin_ref[...]).reshape(y_ref.shape)


def _merge_ffn_sample(x, oa, ob, mods, goa, gob, wo, gf, wg, wu, wd, gfin):
    t, n, d = x.shape
    whole = lambda a: pl.BlockSpec(a.shape, lambda j: (0,) * a.ndim)
    weights = (goa, gob, wo, gf, wg, wu, wd, gfin)
    ins = (x, oa, ob, *mods, *weights)
    return pl.pallas_call(
        _merge_ffn_sample_kernel,
        grid=(1,),
        in_specs=[whole(a) for a in ins[:7]] + [_resident(a.shape) for a in weights],
        out_specs=whole(x),
        out_shape=jax.ShapeDtypeStruct((t, n, d), _F32),
        compiler_params=_params(("arbitrary",)),
        name="merge_ffn_sample",
    )(*ins)


def _rope_tables(pos):
    half = HEAD_DIM // 2
    inv = jnp.exp(-math.log(ROPE_THETA) * jnp.arange(half, dtype=_F32) * (2.0 / HEAD_DIM))
    ang = pos.astype(_F32)[:, None] * inv[None, :]
    cos, sin = jnp.cos(ang), jnp.sin(ang)
    return jnp.tile(cos, (1, LANES // half)), jnp.tile(jnp.concatenate([-sin, sin], axis=1), (1, LANES // HEAD_DIM))


def _window_on_lanes(cache):
    _, n, w, h, dh = cache.shape
    return cache.transpose(0, 1, 3, 4, 2).reshape(n, h * dh, w)


def _window_off_lanes(t, h):
    n, _, w = t.shape
    return t.reshape(1, n, h, HEAD_DIM, w).transpose(0, 1, 4, 2, 3)


def _halves_last_rows(new_t, pad_to):
    t, n, c = new_t.shape
    v = new_t.reshape(t, n, 2, c // 2).transpose(1, 2, 0, 3)
    return jnp.pad(v, ((0, 0), (0, 0), (SUBLANES - t, 0), (0, pad_to - c // 2)))


def kernel(x_prompt, x_sample, c_prompt, c_sample, cache_a_k, cache_a_v, cache_b_k, cache_b_v, w_ada, b_ada, g_attn, w_in, sinks, g_out_a, g_out_b, w_o, g_ffn, w_gate, w_up, w_down, g_final):
    nb, s, d = x_prompt.shape
    ns, t_new, _ = x_sample.shape
    assert w_ada.shape[0] == 1, "single trunk layer"
    l = 0
    bf = lambda w: w.astype(_BF)
    row = lambda g: g.reshape(1, -1)

    c_all = jnp.concatenate([c_prompt, c_sample], axis=0)
    c_all = jnp.pad(c_all, ((0, (-c_all.shape[0]) % SUBLANES), (0, 0)))
    mod = _adaln(c_all, w_ada[l], b_ada[l].reshape(1, -1))
    mod_p = [m.reshape(nb, 1, d) for m in jnp.split(mod[:nb], 6, axis=-1)]
    mod_s = jnp.split(mod[nb:nb + ns], 6, axis=-1)

    cos_p, sin_p = _rope_tables(jnp.arange(s, dtype=jnp.int32))
    cos_s, sin_s = _rope_tables(PAST_LEN + jnp.arange(t_new, dtype=jnp.int32))
    w_in_bf = bf(w_in[l])
    ffn_w = (row(g_out_a[l]), row(g_out_b[l]), bf(w_o[l]), row(g_ffn[l]), bf(w_gate[l]), bf(w_up[l]), bf(w_down[l]),
             row(g_final))

    wa_p, wb_p = min(A_WINDOW, s), min(B_WINDOW, s)
    (qa, ka, va, q1, k1, v1, q4, k4, v4, q16, k16, v16, ka_t, va_t, kb_t, vb_t) = _inproj_prompt(
        x_prompt, mod_p[0], mod_p[1], row(g_attn[l]), w_in_bf, cos_p, sin_p, tm=512, tail_a=wa_p, tail_b=wb_p)
    xs_t = x_sample.transpose(1, 0, 2)
    qa_s, ka_s, va_s, qb_s, kb_s, vb_s = _inproj_sample(
        xs_t, mod_s[0], mod_s[1], row(g_attn[l]), w_in_bf, cos_s.reshape(t_new, 1, LANES),
        sin_s.reshape(t_new, 1, LANES))

    oa, la = _swa_prompt(qa, ka, va, sinks[l])
    obs, mls = [], []
    for (q, k, v), (_, dil) in zip(((q1, k1, v1), (q4, k4, v4), (q16, k16, v16)), B_PATTERNS):
        o, ml = _band_attn(q, k, v, dil)
        obs.append(o)
        mls.append(ml)

    qa_g = (qa_s.reshape(t_new, ns, A_KV_HEADS, A_GROUP, HEAD_DIM).transpose(1, 2, 0, 3, 4)
            .reshape(ns, A_KV_HEADS, t_new * A_GROUP, HEAD_DIM))
    kv_g = lambda t: t.reshape(t_new, ns, A_KV_HEADS, HEAD_DIM).transpose(1, 2, 0, 3)
    qb_h = qb_s.reshape(t_new, ns, 2, B_W // 2).transpose(1, 2, 0, 3)
    sample_ins = (qa_g, kv_g(ka_s), kv_g(va_s), _window_on_lanes(cache_a_k), _window_on_lanes(cache_a_v),
                  _halves_last_rows(ka_s, LANES), _halves_last_rows(va_s, LANES), qb_h,
                  _halves_last_rows(kb_s, B_W // 2), _halves_last_rows(vb_s, B_W // 2),
                  _window_on_lanes(cache_b_k), _window_on_lanes(cache_b_v))
    y_prompt, oa_g, ob_h, na_k, na_v, nb_k, nb_v = _ffn_and_sample(
        x_prompt, oa, la, obs, mls, mod_p[2:6], *ffn_w, sinks[l], sample_ins, tm=256)
    pa_k = ka_t.reshape(1, nb, wa_p, A_KV_HEADS, HEAD_DIM)
    pa_v = va_t.reshape(1, nb, wa_p, A_KV_HEADS, HEAD_DIM)
    pb_k = kb_t.reshape(1, nb, wb_p, B_HEADS, HEAD_DIM)
    pb_v = vb_t.reshape(1, nb, wb_p, B_HEADS, HEAD_DIM)

    oa_t = (oa_g.reshape(ns, A_KV_HEADS, t_new, A_GROUP, HEAD_DIM).transpose(2, 0, 1, 3, 4)
            .reshape(t_new, ns, A_QW))
    ob_t = ob_h.transpose(2, 0, 1, 3).reshape(t_new, ns, B_W)
    y_s = _merge_ffn_sample(xs_t, oa_t, ob_t, mod_s[2:6], *ffn_w)
    y_sample = y_s.transpose(1, 0, 2)
    sa_k, sa_v = _window_off_lanes(na_k, A_KV_HEADS), _window_off_lanes(na_v, A_KV_HEADS)
    sb_k, sb_v = _window_off_lanes(nb_k, B_HEADS), _window_off_lanes(nb_v, B_HEADS)

    return (y_prompt, y_sample, pa_k, pa_v, pb_k, pb_v, sa_k, sa_v, sb_k, sb_v)
```

```python
import functools
import math

import jax
import jax.numpy as jnp
from jax import lax
from jax.experimental import pallas as pl
from jax.experimental.pallas import tpu as pltpu

HEAD_DIM = 64
A_Q_HEADS = 8
A_KV_HEADS = 2
A_GROUP = A_Q_HEADS // A_KV_HEADS
B_HEADS = 8
A_WINDOW = 128
BLOCK = 128
B_PATTERNS = ((128, 1), (512, 4), (2048, 16))
B_WINDOW = 2048
PAST_LEN = 8192
ROPE_THETA = 10000.0
EPS = 1e-6
A_QW = A_Q_HEADS * HEAD_DIM
A_KVW = A_KV_HEADS * HEAD_DIM
B_W = B_HEADS * HEAD_DIM
SCALE = HEAD_DIM ** -0.5
LANES = 128
SUBLANES = 8
NEG = -1e30
VMEM_LIMIT = 56 * 1024 * 1024
Q_BLOCKS = 16

_BF = jnp.bfloat16
_F32 = jnp.float32


def _params(sem, vmem=VMEM_LIMIT):
    return pltpu.CompilerParams(dimension_semantics=sem, vmem_limit_bytes=vmem)


def _resident(shape):
    nd = len(shape)
    return pl.BlockSpec(shape, lambda *_: (0,) * nd, pipeline_mode=pl.Buffered(1))


def _rms(x, g):
    return x * lax.rsqrt(jnp.mean(x * x, axis=-1, keepdims=True) + EPS) * g


def _log2(n):
    assert n > 0 and n & (n - 1) == 0, "power of two expected"
    return n.bit_length() - 1


def _div(x, n):
    return x >> _log2(n)


def _multiple(x, n):
    _log2(n)
    return (x & (n - 1)) == 0


def _adaln_kernel(c_ref, w_ref, b_ref, o_ref):
    c = c_ref[...]
    a = (c * jax.nn.sigmoid(c)).astype(_BF)
    o_ref[...] = jnp.dot(a, w_ref[...].astype(_BF), preferred_element_type=_F32) + b_ref[...]


def _adaln(c_all, w, b):
    m, d = c_all.shape
    n = w.shape[1]
    tn = 768
    return pl.pallas_call(
        _adaln_kernel,
        grid=(n // tn,),
        in_specs=[pl.BlockSpec((m, d), lambda j: (0, 0)),
                  pl.BlockSpec((d, tn), lambda j: (0, j)),
                  pl.BlockSpec((1, tn), lambda j: (0, j))],
        out_specs=pl.BlockSpec((m, tn), lambda j: (0, j)),
        out_shape=jax.ShapeDtypeStruct((m, n), _F32),
        compiler_params=_params(("arbitrary",)),
        name="adaln",
    )(c_all, w, b)


def _rope(x, cos, sin_signed, first_half):
    sw = jnp.where(first_half, pltpu.roll(x, 96, 1), pltpu.roll(x, 32, 1))
    return x * cos + sw * sin_signed


def _project(x_ref, sh_ref, sc_ref, g_ref, w_ref, cos_ref, sin_ref):
    d = x_ref.shape[-1]
    x = x_ref[...].reshape(-1, d)
    sh = sh_ref[...].reshape(-1, d)
    sc = sc_ref[...].reshape(-1, d)
    h = _rms(x, g_ref[...]) * (1.0 + sc) + sh
    proj = jnp.dot(h.astype(_BF), w_ref[...], preferred_element_type=_F32)
    cos = cos_ref[...].reshape(-1, LANES)
    sin = sin_ref[...].reshape(-1, LANES)
    lane = lax.broadcasted_iota(jnp.int32, (1, LANES), 1)
    first_half = (lane & (HEAD_DIM - 1)) < (HEAD_DIM // 2)

    def rot(col0, ncols):
        parts = [_rope(proj[:, col0 + j * LANES: col0 + (j + 1) * LANES], cos, sin, first_half)
                 for j in range(ncols // LANES)]
        return parts[0] if len(parts) == 1 else jnp.concatenate(parts, axis=1)

    o = 0
    qa = rot(o, A_QW) * SCALE; o += A_QW
    ka = rot(o, A_KVW); o += A_KVW
    va = proj[:, o:o + A_KVW]; o += A_KVW
    qb = rot(o, B_W) * SCALE; o += B_W
    kb = rot(o, B_W); o += B_W
    vb = proj[:, o:o + B_W]
    return qa, ka, va, qb, kb, vb


def _put(ref, val):
    ref[...] = val.astype(ref.dtype).reshape(ref.shape)


def _inproj_prompt_kernel(x_ref, sh_ref, sc_ref, g_ref, w_ref, cos_ref, sin_ref,
                          qa_ref, ka_ref, va_ref, q1_ref, k1_ref, v1_ref, q4_ref, k4_ref, v4_ref,
                          q16_ref, k16_ref, v16_ref, kat_ref, vat_ref, kbt_ref, vbt_ref,
                          sq_ref, sk_ref, sv_ref):
    qa, ka, va, qb, kb, vb = _project(x_ref, sh_ref, sc_ref, g_ref, w_ref, cos_ref, sin_ref)
    tm = qa.shape[0]
    _put(qa_ref, qa)
    lo = lax.broadcasted_iota(jnp.int32, (1, LANES), 1) < HEAD_DIM
    for ref, val in ((ka_ref, ka), (va_ref, va)):
        swapped = pltpu.roll(val, HEAD_DIM, 1)
        _put(ref, jnp.concatenate([jnp.where(lo, val, swapped), jnp.where(lo, swapped, val)], axis=1))
    _put(q1_ref, qb); _put(k1_ref, kb); _put(v1_ref, vb)
    ta = kat_ref.shape[1]
    _put(kat_ref, ka[tm - ta:]); _put(vat_ref, va[tm - ta:])
    _put(kbt_ref, kb); _put(vbt_ref, vb)
    for val, stage, r4, r16 in ((qb, sq_ref, q4_ref, q16_ref), (kb, sk_ref, k4_ref, k16_ref),
                                (vb, sv_ref, v4_ref, v16_ref)):
        for c in range(B_W // LANES):
            stage[c] = val[:, c * LANES:(c + 1) * LANES]
        for dil, out in ((4, r4), (16, r16)):
            n = tm // dil
            for r in range(dil):
                for c in range(B_W // LANES):
                    out[0, r, :, c * LANES:(c + 1) * LANES] = stage[c, pl.ds(r, n, stride=dil), :].astype(out.dtype)


def _inproj_prompt(x, sh, sc, g, w_bf, cos, sin, *, tm, tail_a, tail_b):
    b, s, d = x.shape
    nt = s // tm
    assert tail_b % tm == 0 and tail_a <= tm
    row = lambda c: pl.BlockSpec((1, tm, c), lambda bb, i: (bb, i, 0))
    res = lambda dil: pl.BlockSpec((1, dil, tm // dil, B_W), lambda bb, i: (bb, 0, i, 0))
    mod_spec = pl.BlockSpec((1, 1, d), lambda bb, i: (bb, 0, 0))
    tab_spec = pl.BlockSpec((tm, LANES), lambda bb, i: (i, 0))
    tail_a_spec = pl.BlockSpec((1, tail_a, A_KVW), lambda bb, i: (bb, 0, 0))
    tail_b_spec = pl.BlockSpec((1, tm, B_W), lambda bb, i: (bb, jnp.maximum(i - (nt - tail_b // tm), 0), 0))
    shp = lambda c, dt: jax.ShapeDtypeStruct((b, s, c), dt)
    rshp = lambda dil: jax.ShapeDtypeStruct((b, dil, s // dil, B_W), _BF)
    return pl.pallas_call(
        _inproj_prompt_kernel,
        grid=(b, nt),
        in_specs=[row(d), mod_spec, mod_spec, _resident((1, d)), _resident(w_bf.shape), tab_spec, tab_spec],
        out_specs=[row(A_QW), row(2 * A_KVW), row(2 * A_KVW)] + [res(1)] * 3 + [res(4)] * 3 + [res(16)] * 3
                  + [tail_a_spec] * 2 + [tail_b_spec] * 2,
        out_shape=[shp(A_QW, _BF), shp(2 * A_KVW, _BF), shp(2 * A_KVW, _BF)] + [rshp(1)] * 3 + [rshp(4)] * 3 + [rshp(16)] * 3
                  + [jax.ShapeDtypeStruct((b, tail_a, A_KVW), _F32)] * 2
                  + [jax.ShapeDtypeStruct((b, tail_b, B_W), _F32)] * 2,
        scratch_shapes=[pltpu.VMEM((B_W // LANES, tm, LANES), _F32)] * 3,
        compiler_params=_params(("parallel", "arbitrary")),
        name="in_proj_prompt",
    )(x, sh, sc, g, w_bf, cos, sin)


def _inproj_sample_kernel(x_ref, sh_ref, sc_ref, g_ref, w_ref, cos_ref, sin_ref,
                          qa_ref, ka_ref, va_ref, qb_ref, kb_ref, vb_ref):
    vals = _project(x_ref, sh_ref, sc_ref, g_ref, w_ref, cos_ref, sin_ref)
    for ref, val in zip((qa_ref, ka_ref, va_ref, qb_ref, kb_ref, vb_ref), vals):
        _put(ref, val)


def _inproj_sample(x, sh, sc, g, w_bf, cos, sin):
    t, n, d = x.shape
    row = lambda c: pl.BlockSpec((1, n, c), lambda j: (j, 0, 0))
    mod_spec = pl.BlockSpec((n, d), lambda j: (0, 0))
    tab_spec = pl.BlockSpec((1, 1, LANES), lambda j: (j, 0, 0))
    widths = (A_QW, A_KVW, A_KVW, B_W, B_W, B_W)
    return pl.pallas_call(
        _inproj_sample_kernel,
        grid=(t,),
        in_specs=[row(d), mod_spec, mod_spec, _resident((1, d)), _resident(w_bf.shape), tab_spec, tab_spec],
        out_specs=[row(c) for c in widths],
        out_shape=[jax.ShapeDtypeStruct((t, n, c), _F32) for c in widths],
        compiler_params=_params(("parallel",)),
        name="in_proj_sample",
    )(x, sh, sc, g, w_bf, cos, sin)


def _band_bias(max_dist, drop_prev):
    qi = lax.broadcasted_iota(jnp.int32, (BLOCK, 2 * BLOCK), 0)
    ki = lax.broadcasted_iota(jnp.int32, (BLOCK, 2 * BLOCK), 1)
    rel = qi + BLOCK - ki
    valid = (rel >= 0) & (rel <= max_dist)
    if drop_prev is not None:
        valid = valid & ((ki >= BLOCK) | jnp.logical_not(drop_prev))
    return jnp.where(valid, 0.0, NEG).astype(_F32)


def _kv_window(prev_ref, cur_ref, j, lead, cols):
    if j == 0:
        return jnp.concatenate([prev_ref[lead + (slice(None), cols)], cur_ref[lead + (slice(0, BLOCK), cols)]], axis=0)
    return cur_ref[lead + (slice((j - 1) * BLOCK, (j + 1) * BLOCK), cols)]


def _stat_columns(stats, cols, acc):
    lane = lax.broadcasted_iota(jnp.int32, (1, acc.shape[1]), 1)
    for col, stat in zip(cols, stats):
        acc = jnp.where(lane == col, stat, acc)
    return acc


def _band_attn_kernel(q_ref, kp_ref, kc_ref, vp_ref, vc_ref, o_ref, ml_ref, *, max_dist, nq):
    first_step = pl.program_id(2) == 0
    bias = _band_bias(max_dist, None)
    bias0 = _band_bias(max_dist, first_step)
    lane = lax.broadcasted_iota(jnp.int32, (1, LANES), 1)
    lo = lane < HEAD_DIM
    zero = jnp.zeros((), _BF)
    for rr, j in [(rr, j) for rr in range(q_ref.shape[1]) for j in range(nq)]:
        rows = slice(j * BLOCK, (j + 1) * BLOCK)
        bj = bias0 if j == 0 else bias
        bj = jnp.concatenate([bj, bj], axis=0)
        ml_all = jnp.zeros((BLOCK, LANES), _F32)
        for c in range(B_W // LANES):
            cs = slice(c * LANES, (c + 1) * LANES)
            q2 = q_ref[0, rr, rows, cs]
            k2 = _kv_window(kp_ref, kc_ref, j, (0, rr), cs)
            v2 = _kv_window(vp_ref, vc_ref, j, (0, rr), cs)
            qs = jnp.concatenate([jnp.where(lo, q2, zero), jnp.where(lo, zero, q2)], axis=0)
            s = lax.dot_general(qs, k2, (((1,), (1,)), ((), ())), preferred_element_type=_F32) + bj
            m = jnp.max(s, axis=-1, keepdims=True)
            p = jnp.exp(s - m)
            l = jnp.sum(p, axis=-1, keepdims=True)
            pv = jnp.dot(p.astype(_BF), v2, preferred_element_type=_F32)
            o_ref[0, rr, rows, cs] = jnp.where(lo, pv[:BLOCK], pv[BLOCK:]).astype(o_ref.dtype)
            ml_all = _stat_columns((m[:BLOCK], m[BLOCK:], l[:BLOCK], l[BLOCK:]),
                                   (2 * c, 2 * c + 1, B_HEADS + 2 * c, B_HEADS + 2 * c + 1), ml_all)
        ml_ref[0, rr, rows, :] = ml_all


def _band_attn(q, k, v, dil):
    b, _, m, w = q.shape
    nq = min(Q_BLOCKS, m // BLOCK)
    tq = nq * BLOCK
    rb = min(dil, Q_BLOCKS // nq)
    cur = pl.BlockSpec((1, rb, tq, w), lambda bb, r, i: (bb, r, i, 0))
    prev = pl.BlockSpec((1, rb, BLOCK, w), lambda bb, r, i: (bb, r, jnp.maximum(i * nq - 1, 0), 0))
    window, _ = [p for p in B_PATTERNS if p[1] == dil][0]
    return pl.pallas_call(
        functools.partial(_band_attn_kernel, max_dist=window // dil, nq=nq),
        grid=(b, dil // rb, m // tq),
        in_specs=[cur, prev, cur, prev, cur],
        out_specs=[cur, pl.BlockSpec((1, rb, tq, LANES), lambda bb, r, i: (bb, r, i, 0))],
        out_shape=[jax.ShapeDtypeStruct((b, dil, m, w), _BF),
                   jax.ShapeDtypeStruct((b, dil, m, LANES), _F32)],
        compiler_params=_params(("parallel", "parallel", "arbitrary")),
        name=f"band_attn_d{dil}",
    )(q, k, k, v, v)


def _swa_kernel(sink_ref, q_ref, kp_ref, kc_ref, vp_ref, vc_ref, o_ref, l_ref, *, nq):
    first_step = pl.program_id(1) == 0
    bias = _band_bias(A_WINDOW - 1, None)
    bias0 = _band_bias(A_WINDOW - 1, first_step)
    lane = lax.broadcasted_iota(jnp.int32, (1, LANES), 1)
    lo = lane < HEAD_DIM
    zero = jnp.zeros((), _BF)
    col0 = lax.broadcasted_iota(jnp.int32, (1, 2 * BLOCK), 1) == 0
    for j in range(nq):
        rows = slice(j * BLOCK, (j + 1) * BLOCK)
        bj = bias0 if j == 0 else bias
        bj = jnp.concatenate([bj, bj], axis=0)
        l_all = jnp.zeros((BLOCK, A_Q_HEADS), _F32)
        for c in range(A_QW // LANES):
            g = (2 * c) // A_GROUP
            gs = slice(g * LANES, (g + 1) * LANES)
            q2 = q_ref[0, rows, c * LANES:(c + 1) * LANES]
            k2 = _kv_window(kp_ref, kc_ref, j, (0,), gs)
            v2 = _kv_window(vp_ref, vc_ref, j, (0,), gs)
            qs = jnp.concatenate([jnp.where(lo, q2, zero), jnp.where(lo, zero, q2)], axis=0)
            s = lax.dot_general(qs, k2, (((1,), (1,)), ((), ())), preferred_element_type=_F32) + bj
            s = jnp.concatenate([jnp.where(col0, sink_ref[2 * c + half], s[half * BLOCK:(half + 1) * BLOCK])
                                 for half in range(2)], axis=0)
            m = jnp.max(s, axis=-1, keepdims=True)
            p = jnp.exp(s - m)
            l = jnp.sum(p, axis=-1, keepdims=True)
            p = jnp.where(col0, 0.0, p).astype(_BF)
            pv = jnp.dot(p, v2, preferred_element_type=_F32)
            _put(o_ref.at[0, rows, c * LANES:(c + 1) * LANES], jnp.where(lo, pv[:BLOCK], pv[BLOCK:]))
            l_all = _stat_columns((l[:BLOCK], l[BLOCK:]), (2 * c, 2 * c + 1), l_all)
        l_ref[0, rows, :] = l_all


def _swa_prompt(q, k, v, sinks):
    b, s, _ = q.shape
    nq = Q_BLOCKS
    tq = nq * BLOCK
    assert A_WINDOW <= BLOCK, "the sink logit borrows a key column that no query row can reach"
    qspec = pl.BlockSpec((1, tq, A_QW), lambda bb, i: (bb, i, 0))
    cur = pl.BlockSpec((1, tq, 2 * A_KVW), lambda bb, i: (bb, i, 0))
    prev = pl.BlockSpec((1, BLOCK, 2 * A_KVW), lambda bb, i: (bb, jnp.maximum(i * nq - 1, 0), 0))
    return pl.pallas_call(
        functools.partial(_swa_kernel, nq=nq),
        grid=(b, s // tq),
        in_specs=[pl.BlockSpec(memory_space=pltpu.SMEM), qspec, prev, cur, prev, cur],
        out_specs=[qspec, pl.BlockSpec((1, tq, A_Q_HEADS), lambda bb, i: (bb, i, 0))],
        out_shape=[jax.ShapeDtypeStruct((b, s, A_QW), _BF), jax.ShapeDtypeStruct((b, s, A_Q_HEADS), _F32)],
        compiler_params=_params(("parallel", "arbitrary")),
        name="swa_prompt",
    )(sinks, q, k, k, v, v)


def _roll_in(x_ref, new8_ref, o_ref, t_new):
    rows, width = x_ref.shape
    new8 = new8_ref[0, 0]
    tail = jnp.concatenate([jnp.zeros((LANES - new8.shape[0], new8.shape[1]), _F32), new8], axis=0).T[:rows]
    lane = lax.broadcasted_iota(jnp.int32, (1, LANES), 1)
    keep = lane < LANES - t_new
    nxt = pltpu.roll(x_ref[:, 0:LANES], LANES - t_new, 1)
    for c in range(width // LANES):
        cur = nxt
        if c + 1 < width // LANES:
            nxt = pltpu.roll(x_ref[:, (c + 1) * LANES:(c + 2) * LANES], LANES - t_new, 1)
        else:
            nxt = tail
        o_ref[:, c * LANES:(c + 1) * LANES] = jnp.where(keep, cur, nxt)


def _sample_half(g, refs, *, t_new):
    (sink_ref, qa_ref, kan_ref, van_ref, cak_ref, cav_ref, ak8_ref, av8_ref,
     qb_ref, kb8_ref, vb8_ref, cbk_ref, cbv_ref,
     oa_ref, ob_ref, nak_ref, nav_ref, nbk_ref, nbv_ref) = refs
    wa, wb = cak_ref.shape[-1], cbk_ref.shape[-1]

    rows_a = t_new * A_GROUP
    ra = lax.broadcasted_iota(jnp.int32, (rows_a, 1), 0)
    qi_a = _div(ra, A_GROUP)
    rel_a = wa + qi_a - lax.broadcasted_iota(jnp.int32, (1, wa), 1)
    valid_a = (rel_a >= 0) & (rel_a < A_WINDOW)
    hd = slice(0, HEAD_DIM)
    q = qa_ref[0, 0, 0:rows_a, hd]
    s = jnp.dot(q.astype(_BF), cak_ref[0].astype(_BF), preferred_element_type=_F32)
    s = jnp.where(valid_a, s, NEG)
    s_new = [jnp.where(qi_a >= j, jnp.sum(q * kan_ref[0, 0, j:j + 1, hd], axis=-1, keepdims=True), NEG)
             for j in range(t_new)]
    sink = jnp.zeros((rows_a, 1), _F32)
    for u in range(A_GROUP):
        sink = jnp.where((ra & (A_GROUP - 1)) == u, sink_ref[g * A_GROUP + u], sink)
    m = functools.reduce(jnp.maximum, [jnp.max(s, axis=-1, keepdims=True), sink] + s_new)
    p = jnp.exp(s - m)
    p_new = [jnp.exp(z - m) for z in s_new]
    l = functools.reduce(jnp.add, [jnp.sum(p, axis=-1, keepdims=True), jnp.exp(sink - m)] + p_new)
    o = lax.dot_general(p.astype(_BF), cav_ref[0].astype(_BF), (((1,), (1,)), ((), ())),
                        preferred_element_type=_F32)
    for j in range(t_new):
        o = o + p_new[j] * van_ref[0, 0, j:j + 1, hd]
    oa_ref[0, 0, 0:rows_a, hd] = o * (1.0 / l)
    _roll_in(cak_ref.at[0], ak8_ref, nak_ref.at[0], t_new)
    _roll_in(cav_ref.at[0], av8_ref, nav_ref.at[0], t_new)

    width_b = cbk_ref.shape[1]
    r0 = kb8_ref.shape[2] - t_new
    rows_b = t_new * SUBLANES
    sub = lax.broadcasted_iota(jnp.int32, (SUBLANES, width_b), 0)
    own = _div(lax.broadcasted_iota(jnp.int32, (SUBLANES, width_b), 1), HEAD_DIM) == sub
    qbd = jnp.concatenate([jnp.where(own, qb_ref[0, 0, i:i + 1, :], 0.0) for i in range(t_new)], axis=0)
    s_all = jnp.dot(qbd.astype(_BF), cbk_ref[0].astype(_BF), preferred_element_type=_F32)
    qi = _div(lax.broadcasted_iota(jnp.int32, (rows_b, 1), 0), SUBLANES)
    s_new = [jnp.sum(qbd * kb8_ref[0, 0, r0 + j:r0 + j + 1, :], axis=-1, keepdims=True) for j in range(t_new)]
    pats = []
    for window, dil in B_PATTERNS:
        lo_lane = max(wb - (-(-window // LANES) * LANES), 0)
        wl = lo_lane + lax.broadcasted_iota(jnp.int32, (1, wb - lo_lane), 1)
        rel = wb + qi - wl
        valid = (rel <= window) & _multiple(rel, dil)
        s = jnp.where(valid, s_all[:, lo_lane:], NEG)
        sn = [jnp.where((qi - j >= 0) & _multiple(qi - j, dil), s_new[j], NEG) for j in range(t_new)]
        m = functools.reduce(jnp.maximum, [jnp.max(s, axis=-1, keepdims=True)] + sn)
        p = jnp.exp(s - m)
        pn = [jnp.exp(z - m) for z in sn]
        l = functools.reduce(jnp.add, [jnp.sum(p, axis=-1, keepdims=True)] + pn)
        pats.append((lo_lane, p, pn, l, m + jnp.log(l)))
    mx = functools.reduce(jnp.maximum, [t[4] for t in pats])
    es = [jnp.exp(t[4] - mx) for t in pats]
    den = functools.reduce(jnp.add, es)
    coef = [e / (den * t[3]) for e, t in zip(es, pats)]
    starts = sorted({t[0] for t in pats} | {wb})
    blocks = []
    for a, b in zip(starts[:-1], starts[1:]):
        acc = None
        for c, (lo_lane, p, _, _, _) in zip(coef, pats):
            if lo_lane <= a:
                term = c * p[:, a - lo_lane:b - lo_lane]
                acc = term if acc is None else acc + term
        blocks.append(acc)
    lead = starts[0]
    p_comb = jnp.concatenate(blocks, axis=1) if len(blocks) > 1 else blocks[0]
    o_full = lax.dot_general(p_comb.astype(_BF), cbv_ref[0, :, lead:].astype(_BF), (((1,), (1,)), ((), ())),
                             preferred_element_type=_F32)
    for j in range(t_new):
        pj = functools.reduce(jnp.add, [c * t[2][j] for c, t in zip(coef, pats)])
        o_full = o_full + pj * vb8_ref[0, 0, r0 + j:r0 + j + 1, :]
    for i in range(t_new):
        blk = jnp.where(own, o_full[i * SUBLANES:(i + 1) * SUBLANES], 0.0)
        ob_ref[0, 0, i:i + 1, :] = jnp.sum(blk, axis=0, keepdims=True)
    _roll_in(cbk_ref.at[0], kb8_ref, nbk_ref, t_new)
    _roll_in(cbv_ref.at[0], vb8_ref, nbv_ref, t_new)


def _ffn_chunk(h, wg_ref, wu_ref, wd_ref, cs):
    gt = jnp.dot(h, wg_ref[:, cs], preferred_element_type=_F32)
    up = jnp.dot(h, wu_ref[:, cs], preferred_element_type=_F32)
    act = (gt * jax.nn.sigmoid(gt) * up).astype(_BF)
    return jnp.dot(act, wd_ref[cs, :], preferred_element_type=_F32)


def _attn_residual(x, oa, ob, gta, shf, scf, goa_ref, gob_ref, wo_ref, gf_ref):
    merged = jnp.concatenate([_rms(oa, goa_ref[...]), _rms(ob, gob_ref[...])], axis=1)
    x1 = x + gta * jnp.dot(merged.astype(_BF), wo_ref[...], preferred_element_type=_F32)
    return x1, (_rms(x1, gf_ref[...]) * (1.0 + scf) + shf).astype(_BF)


_TWO = lambda r: r[...].reshape(-1, r.shape[-1])
_FF_CUTS = (0, 384, 1152, 2048, 2816)
_PHASES = len(_FF_CUTS) - 1


def _ffn_and_sample_kernel(*refs, t_new):
    (x_ref, oa_ref, la_ref, o1_ref, o4_ref, o16_ref, ml1_ref, ml2_ref, ml3_ref, e_ref,
     gta_ref, shf_ref, scf_ref, gtf_ref, goa_ref, gob_ref, wo_ref, gf_ref, wg_ref, wu_ref, wd_ref, gfin_ref) = refs[:22]
    sink_ref, small_ref, cak_ref, cav_ref, cbk_ref, cbv_ref = refs[22:28]
    y_ref, small_out_ref, nak_ref, nav_ref, nbk_ref, nbv_ref = refs[28:34]
    s4_ref, s16_ref, x1_ref, h_ref, acc_ref, t4_ref, t16_ref = refs[34:]
    step = pl.program_id(0)
    half = step % 2
    phase = step % _PHASES
    tm = x_ref.shape[1]

    def piece(ref, rows):
        r0, r1, lanes = rows
        return ref.at[:, :, r0:r0 + -(-(r1 - r0) // SUBLANES) * SUBLANES, 0:-(-lanes // LANES) * LANES]

    lay_in, lay_out = _small_layout(t_new)[0], _small_out_layout(t_new)[0]
    sample_refs = ((sink_ref,) + tuple(piece(small_ref, lay_in[k]) for k in ("qa", "ka", "va"))
                   + (cak_ref, cav_ref) + tuple(piece(small_ref, lay_in[k]) for k in ("ak8", "av8", "qb", "kb8", "vb8"))
                   + (cbk_ref, cbv_ref, piece(small_out_ref, lay_out["oa"]), piece(small_out_ref, lay_out["ob"]),
                      nak_ref, nav_ref, nbk_ref.at[0], nbv_ref.at[0]))

    def mix_and_project():
        for dil, src, stage, stat_src, stat_stage in ((4, o4_ref, s4_ref, ml2_ref, t4_ref),
                                                      (16, o16_ref, s16_ref, ml3_ref, t16_ref)):
            for r in range(dil):
                rows_r = pl.ds(r, tm // dil, stride=dil)
                stat_stage[rows_r, :] = stat_src[0, r]
                for c in range(B_W // LANES):
                    stage[c, rows_r, :] = src[0, r, :, c * LANES:(c + 1) * LANES].astype(_F32)
        gather = lambda stage: jnp.concatenate([stage[c] for c in range(B_W // LANES)], axis=1)
        outs = [_TWO(o1_ref).astype(_F32), gather(s4_ref), gather(s16_ref)]

        def widen(per_head):
            hi = per_head.astype(_BF)
            lo = (per_head - hi.astype(_F32)).astype(_BF)
            return (jnp.dot(hi, e_ref[...], preferred_element_type=_F32)
                    + jnp.dot(lo, e_ref[...], preferred_element_type=_F32))

        stats = (ml1_ref.at[0, 0], t4_ref, t16_ref)
        ms = [r[:, 0:B_HEADS] for r in stats]
        ls = [r[:, B_HEADS:2 * B_HEADS] for r in stats]
        lses = [m + jnp.log(l) for m, l in zip(ms, ls)]
        mx = jnp.maximum(jnp.maximum(lses[0], lses[1]), lses[2])
        es = [jnp.exp(z - mx) for z in lses]
        den = es[0] + es[1] + es[2]
        ob = jnp.zeros((tm, B_W), _F32)
        for e, l, o in zip(es, ls, outs):
            ob = ob + widen(e / (den * l)) * o
        oa = widen(1.0 / _TWO(la_ref)) * _TWO(oa_ref).astype(_F32)
        x1, h = _attn_residual(_TWO(x_ref), oa, ob, _TWO(gta_ref), _TWO(shf_ref), _TWO(scf_ref),
                               goa_ref, gob_ref, wo_ref, gf_ref)
        x1_ref[...] = x1
        h_ref[...] = h

    for k in range(_PHASES):
        @pl.when(phase == k)
        def _(k=k):
            _sample_half(half, sample_refs, t_new=t_new)
            if k == 0:
                mix_and_project()
            part = _ffn_chunk(h_ref[...], wg_ref, wu_ref, wd_ref, slice(_FF_CUTS[k], _FF_CUTS[k + 1]))
            if k == 0:
                acc_ref[...] = part
            elif k < _PHASES - 1:
                acc_ref[...] += part
            else:
                x2 = x1_ref[...] + _TWO(gtf_ref) * (acc_ref[...] + part)
                y_ref[...] = _rms(x2, gfin_ref[...]).reshape(y_ref.shape)


def _small_layout(t_new):
    pieces = (("qa", t_new * A_GROUP, HEAD_DIM), ("ka", t_new, HEAD_DIM), ("va", t_new, HEAD_DIM),
              ("ak8", SUBLANES, LANES), ("av8", SUBLANES, LANES), ("qb", t_new, B_W // 2),
              ("kb8", SUBLANES, B_W // 2), ("vb8", SUBLANES, B_W // 2))
    return _stack_rows(pieces)


def _small_out_layout(t_new):
    return _stack_rows((("oa", t_new * A_GROUP, HEAD_DIM), ("ob", t_new, B_W // 2)))


def _stack_rows(pieces):
    layout, r = {}, 0
    for name, rows, lanes in pieces:
        layout[name] = (r, r + rows, lanes)
        r += -(-rows // SUBLANES) * SUBLANES
    return layout, r


def _ffn_and_sample(x, oa, la, obs, mls, mods, goa, gob, wo, gf, wg, wu, wd, gfin, sinks, small, caches, *, tm):
    b, s, d = x.shape
    ca_k, ca_v, cb_k, cb_v = caches
    n = ca_k.shape[0]
    tiles = s // tm
    steps = 2 * n
    assert steps == b * tiles * _PHASES and _PHASES % 2 == 0 and wg.shape[1] == _FF_CUTS[-1]
    t_new = small["qb"].shape[2]
    layout, rows_in = _small_layout(t_new)
    width = B_W // 2
    slab = jnp.concatenate(
        [jnp.pad(small[k], ((0, 0), (0, 0), (0, (-(r1 - r0)) % SUBLANES), (0, width - lanes)))
         for k, (r0, r1, lanes) in layout.items()], axis=2)
    assert slab.shape[2] == rows_in
    rows_out = _small_out_layout(t_new)[1]

    def tile(j):
        t = j // _PHASES
        return t // tiles, t % tiles

    row = lambda c: pl.BlockSpec((1, tm, c), lambda j: tile(j) + (0,))
    res = lambda dil: pl.BlockSpec((1, dil, tm // dil, B_W), lambda j: (tile(j)[0], 0, tile(j)[1], 0))
    mod_spec = pl.BlockSpec((1, 1, d), lambda j: (tile(j)[0], 0, 0))
    expand = (jnp.arange(B_W)[None, :] // HEAD_DIM == jnp.arange(B_HEADS)[:, None]).astype(_BF)
    ffn_ins = [x, oa, la] + list(obs) + list(mls) + [expand] + list(mods) + [goa, gob, wo, gf, wg, wu, wd, gfin]
    ffn_specs = ([row(d), row(A_QW), row(A_Q_HEADS)] + [res(o.shape[1]) for o in obs]
                 + [pl.BlockSpec((1, o.shape[1], tm // o.shape[1], LANES), lambda j: (tile(j)[0], 0, tile(j)[1], 0))
                    for o in obs]
                 + [_resident(expand.shape)] + [mod_spec] * 4
                 + [_resident(a.shape) for a in (goa, gob, wo, gf, wg, wu, wd, gfin)])
    half_blk = lambda rows: pl.BlockSpec((1, 1, rows, width), lambda j: (j // 2, j % 2, 0, 0))
    cache_blk = lambda a: pl.BlockSpec((1, a.shape[1] // 2, a.shape[2]), lambda j: (j // 2, j % 2, 0))
    s_specs = [half_blk(rows_in)] + [cache_blk(a) for a in caches]
    o_specs = [half_blk(rows_out)] + [cache_blk(a) for a in caches]
    sample_outs = ([jax.ShapeDtypeStruct((n, 2, rows_out, width), _F32)]
                   + [jax.ShapeDtypeStruct(a.shape, _F32) for a in caches])
    stage = pltpu.VMEM((B_W // LANES, tm, LANES), _F32)
    y, small_out, na_k, na_v, nb_k, nb_v = pl.pallas_call(
        functools.partial(_ffn_and_sample_kernel, t_new=t_new),
        grid=(steps,),
        in_specs=ffn_specs + [pl.BlockSpec(memory_space=pltpu.SMEM)] + s_specs,
        out_specs=[row(d)] + o_specs,
        out_shape=[jax.ShapeDtypeStruct((b, s, d), _F32)] + sample_outs,
        scratch_shapes=[stage, stage, pltpu.VMEM((tm, d), _F32), pltpu.VMEM((tm, d), _BF), pltpu.VMEM((tm, d), _F32),
                        pltpu.VMEM((tm, LANES), _F32), pltpu.VMEM((tm, LANES), _F32)],
        compiler_params=_params(("arbitrary",)),
        name="ffn_and_sample",
    )(*ffn_ins, sinks, slab, *caches)
    out_layout = _small_out_layout(t_new)[0]
    take = lambda k: small_out[:, :, out_layout[k][0]:out_layout[k][1], :out_layout[k][2]]
    return y, take("oa"), take("ob"), na_k, na_v, nb_k, nb_v


def _merge_ffn_sample_kernel(x_ref, oa_ref, ob_ref, gta_ref, shf_ref, scf_ref, gtf_ref,
                             goa_ref, gob_ref, wo_ref, gf_ref, wg_ref, wu_ref, wd_ref, gfin_ref, y_ref):
    per_row = lambda r: jnp.concatenate([r[...]] * x_ref.shape[0], axis=0)
    x1, h = _attn_residual(_TWO(x_ref), _TWO(oa_ref), _TWO(ob_ref), per_row(gta_ref), per_row(shf_ref),
                           per_row(scf_ref), goa_ref, gob_ref, wo_ref, gf_ref)
    acc = jnp.zeros_like(x1)
    for k in range(_PHASES):
        acc = acc + _ffn_chunk(h, wg_ref, wu_ref, wd_ref, slice(_FF_CUTS[k], _FF_CUTS[k + 1]))
    y_ref[...] = _rms(x1 + per_row(gtf_ref) * acc, gfin_ref[...]).reshape(y_ref.shape)


def _merge_ffn_sample(x, oa, ob, mods, goa, gob, wo, gf, wg, wu, wd, gfin):
    t, n, d = x.shape
    whole = lambda a: pl.BlockSpec(a.shape, lambda j: (0,) * a.ndim)
    weights = (goa, gob, wo, gf, wg, wu, wd, gfin)
    ins = (x, oa, ob, *mods, *weights)
    return pl.pallas_call(
        _merge_ffn_sample_kernel,
        grid=(1,),
        in_specs=[whole(a) for a in ins[:7]] + [_resident(a.shape) for a in weights],
        out_specs=whole(x),
        out_shape=jax.ShapeDtypeStruct((t, n, d), _F32),
        compiler_params=_params(("arbitrary",)),
        name="merge_ffn_sample",
    )(*ins)


def _rope_tables(pos):
    half = HEAD_DIM // 2
    inv = jnp.exp(-math.log(ROPE_THETA) * jnp.arange(half, dtype=_F32) * (2.0 / HEAD_DIM))
    ang = pos.astype(_F32)[:, None] * inv[None, :]
    cos, sin = jnp.cos(ang), jnp.sin(ang)
    return jnp.tile(cos, (1, LANES // half)), jnp.tile(jnp.concatenate([-sin, sin], axis=1), (1, LANES // HEAD_DIM))


def _window_on_lanes(cache):
    _, n, w, h, dh = cache.shape
    return cache.transpose(0, 1, 3, 4, 2).reshape(n, h * dh, w)


def _window_off_lanes(t, h):
    n, _, w = t.shape
    return t.reshape(1, n, h, HEAD_DIM, w).transpose(0, 1, 4, 2, 3)


def _halves_last_rows(new_t, pad_to):
    t, n, c = new_t.shape
    v = new_t.reshape(t, n, 2, c // 2).transpose(1, 2, 0, 3)
    return jnp.pad(v, ((0, 0), (0, 0), (SUBLANES - t, 0), (0, pad_to - c // 2)))


def kernel(x_prompt, x_sample, c_prompt, c_sample, cache_a_k, cache_a_v, cache_b_k, cache_b_v, w_ada, b_ada, g_attn, w_in, sinks, g_out_a, g_out_b, w_o, g_ffn, w_gate, w_up, w_down, g_final):
    nb, s, d = x_prompt.shape
    ns, t_new, _ = x_sample.shape
    assert w_ada.shape[0] == 1, "single trunk layer"
    l = 0
    bf = lambda w: w.astype(_BF)
    row = lambda g: g.reshape(1, -1)

    c_all = jnp.concatenate([c_prompt, c_sample], axis=0)
    c_all = jnp.pad(c_all, ((0, (-c_all.shape[0]) % SUBLANES), (0, 0)))
    mod = _adaln(c_all, w_ada[l], b_ada[l].reshape(1, -1))
    mod_p = [m.reshape(nb, 1, d) for m in jnp.split(mod[:nb], 6, axis=-1)]
    mod_s = jnp.split(mod[nb:nb + ns], 6, axis=-1)

    cos_p, sin_p = _rope_tables(jnp.arange(s, dtype=jnp.int32))
    cos_s, sin_s = _rope_tables(PAST_LEN + jnp.arange(t_new, dtype=jnp.int32))
    w_in_bf = bf(w_in[l])
    ffn_w = (row(g_out_a[l]), row(g_out_b[l]), bf(w_o[l]), row(g_ffn[l]), bf(w_gate[l]), bf(w_up[l]), bf(w_down[l]),
             row(g_final))

    wa_p, wb_p = min(A_WINDOW, s), min(B_WINDOW, s)
    (qa, ka, va, q1, k1, v1, q4, k4, v4, q16, k16, v16, ka_t, va_t, kb_t, vb_t) = _inproj_prompt(
        x_prompt, mod_p[0], mod_p[1], row(g_attn[l]), w_in_bf, cos_p, sin_p, tm=512, tail_a=wa_p, tail_b=wb_p)
    xs_t = x_sample.transpose(1, 0, 2)
    qa_s, ka_s, va_s, qb_s, kb_s, vb_s = _inproj_sample(
        xs_t, mod_s[0], mod_s[1], row(g_attn[l]), w_in_bf, cos_s.reshape(t_new, 1, LANES),
        sin_s.reshape(t_new, 1, LANES))

    oa, la = _swa_prompt(qa, ka, va, sinks[l])
    obs, mls = [], []
    for (q, k, v), (_, dil) in zip(((q1, k1, v1), (q4, k4, v4), (q16, k16, v16)), B_PATTERNS):
        o, ml = _band_attn(q, k, v, dil)
        obs.append(o)
        mls.append(ml)

    qa_g = (qa_s.reshape(t_new, ns, A_KV_HEADS, A_GROUP, HEAD_DIM).transpose(1, 2, 0, 3, 4)
            .reshape(ns, A_KV_HEADS, t_new * A_GROUP, HEAD_DIM))
    kv_g = lambda t: t.reshape(t_new, ns, A_KV_HEADS, HEAD_DIM).transpose(1, 2, 0, 3)
    qb_h = qb_s.reshape(t_new, ns, 2, B_W // 2).transpose(1, 2, 0, 3)
    small = dict(qa=qa_g, ka=kv_g(ka_s), va=kv_g(va_s), ak8=_halves_last_rows(ka_s, LANES),
                 av8=_halves_last_rows(va_s, LANES), qb=qb_h, kb8=_halves_last_rows(kb_s, B_W // 2),
                 vb8=_halves_last_rows(vb_s, B_W // 2))
    caches = tuple(_window_on_lanes(c) for c in (cache_a_k, cache_a_v, cache_b_k, cache_b_v))
    y_prompt, oa_g, ob_h, na_k, na_v, nb_k, nb_v = _ffn_and_sample(
        x_prompt, oa, la, obs, mls, mod_p[2:6], *ffn_w, sinks[l], small, caches, tm=256)
    pa_k = ka_t.reshape(1, nb, wa_p, A_KV_HEADS, HEAD_DIM)
    pa_v = va_t.reshape(1, nb, wa_p, A_KV_HEADS, HEAD_DIM)
    pb_k = kb_t.reshape(1, nb, wb_p, B_HEADS, HEAD_DIM)
    pb_v = vb_t.reshape(1, nb, wb_p, B_HEADS, HEAD_DIM)

    oa_t = (oa_g.reshape(ns, A_KV_HEADS, t_new, A_GROUP, HEAD_DIM).transpose(2, 0, 1, 3, 4)
            .reshape(t_new, ns, A_QW))
    ob_t = ob_h.transpose(2, 0, 1, 3).reshape(t_new, ns, B_W)
    y_s = _merge_ffn_sample(xs_t, oa_t, ob_t, mod_s[2:6], *ffn_w)
    y_sample = y_s.transpose(1, 0, 2)
    sa_k, sa_v = _window_off_lanes(na_k, A_KV_HEADS), _window_off_lanes(na_v, A_KV_HEADS)
    sb_k, sb_v = _window_off_lanes(nb_k, B_HEADS), _window_off_lanes(nb_v, B_HEADS)

    return (y_prompt, y_sample, pa_k, pa_v, pb_k, pb_v, sa_k, sa_v, sb_k, sb_v)
```

```python
import functools
import math

import jax
import jax.numpy as jnp
from jax import lax
from jax.experimental import pallas as pl
from jax.experimental.pallas import tpu as pltpu

HEAD_DIM = 64
A_Q_HEADS = 8
A_KV_HEADS = 2
A_GROUP = A_Q_HEADS // A_KV_HEADS
B_HEADS = 8
A_WINDOW = 128
BLOCK = 128
B_PATTERNS = ((128, 1), (512, 4), (2048, 16))
B_WINDOW = 2048
PAST_LEN = 8192
ROPE_THETA = 10000.0
EPS = 1e-6
A_QW = A_Q_HEADS * HEAD_DIM
A_KVW = A_KV_HEADS * HEAD_DIM
B_W = B_HEADS * HEAD_DIM
SCALE = HEAD_DIM ** -0.5
LANES = 128
SUBLANES = 8
NEG = -1e30
VMEM_LIMIT = 56 * 1024 * 1024
Q_BLOCKS = 16

_BF = jnp.bfloat16
_F32 = jnp.float32


def _params(sem, vmem=VMEM_LIMIT):
    return pltpu.CompilerParams(dimension_semantics=sem, vmem_limit_bytes=vmem)


def _resident(shape):
    nd = len(shape)
    return pl.BlockSpec(shape, lambda *_: (0,) * nd, pipeline_mode=pl.Buffered(1))


def _rms(x, g):
    return x * lax.rsqrt(jnp.mean(x * x, axis=-1, keepdims=True) + EPS) * g


def _log2(n):
    assert n > 0 and n & (n - 1) == 0, "power of two expected"
    return n.bit_length() - 1


def _div(x, n):
    return x >> _log2(n)


def _multiple(x, n):
    _log2(n)
    return (x & (n - 1)) == 0


def _adaln_kernel(c_ref, w_ref, b_ref, o_ref):
    c = c_ref[...]
    a = (c * jax.nn.sigmoid(c)).astype(_BF)
    o_ref[...] = jnp.dot(a, w_ref[...].astype(_BF), preferred_element_type=_F32) + b_ref[...]


def _adaln(c_all, w, b):
    m, d = c_all.shape
    n = w.shape[1]
    tn = 768
    return pl.pallas_call(
        _adaln_kernel,
        grid=(n // tn,),
        in_specs=[pl.BlockSpec((m, d), lambda j: (0, 0)),
                  pl.BlockSpec((d, tn), lambda j: (0, j)),
                  pl.BlockSpec((1, tn), lambda j: (0, j))],
        out_specs=pl.BlockSpec((m, tn), lambda j: (0, j)),
        out_shape=jax.ShapeDtypeStruct((m, n), _F32),
        compiler_params=_params(("arbitrary",)),
        name="adaln",
    )(c_all, w, b)


def _rope(x, cos, sin_signed, first_half):
    sw = jnp.where(first_half, pltpu.roll(x, 96, 1), pltpu.roll(x, 32, 1))
    return x * cos + sw * sin_signed


def _project(x_ref, sh_ref, sc_ref, g_ref, w_ref, cos_ref, sin_ref):
    d = x_ref.shape[-1]
    x = x_ref[...].reshape(-1, d)
    sh = sh_ref[...].reshape(-1, d)
    sc = sc_ref[...].reshape(-1, d)
    h = _rms(x, g_ref[...]) * (1.0 + sc) + sh
    proj = jnp.dot(h.astype(_BF), w_ref[...], preferred_element_type=_F32)
    cos = cos_ref[...].reshape(-1, LANES)
    sin = sin_ref[...].reshape(-1, LANES)
    lane = lax.broadcasted_iota(jnp.int32, (1, LANES), 1)
    first_half = (lane & (HEAD_DIM - 1)) < (HEAD_DIM // 2)

    def rot(col0, ncols):
        parts = [_rope(proj[:, col0 + j * LANES: col0 + (j + 1) * LANES], cos, sin, first_half)
                 for j in range(ncols // LANES)]
        return parts[0] if len(parts) == 1 else jnp.concatenate(parts, axis=1)

    o = 0
    qa = rot(o, A_QW) * SCALE; o += A_QW
    ka = rot(o, A_KVW); o += A_KVW
    va = proj[:, o:o + A_KVW]; o += A_KVW
    qb = rot(o, B_W) * SCALE; o += B_W
    kb = rot(o, B_W); o += B_W
    vb = proj[:, o:o + B_W]
    return qa, ka, va, qb, kb, vb


def _put(ref, val):
    ref[...] = val.astype(ref.dtype).reshape(ref.shape)


def _inproj_prompt_kernel(x_ref, sh_ref, sc_ref, g_ref, w_ref, cos_ref, sin_ref,
                          qa_ref, ka_ref, va_ref, q1_ref, k1_ref, v1_ref, q4_ref, k4_ref, v4_ref,
                          q16_ref, k16_ref, v16_ref, kat_ref, vat_ref, kbt_ref, vbt_ref,
                          sq_ref, sk_ref, sv_ref):
    qa, ka, va, qb, kb, vb = _project(x_ref, sh_ref, sc_ref, g_ref, w_ref, cos_ref, sin_ref)
    tm = qa.shape[0]
    _put(qa_ref, qa)
    lo = lax.broadcasted_iota(jnp.int32, (1, LANES), 1) < HEAD_DIM
    for ref, val in ((ka_ref, ka), (va_ref, va)):
        swapped = pltpu.roll(val, HEAD_DIM, 1)
        _put(ref, jnp.concatenate([jnp.where(lo, val, swapped), jnp.where(lo, swapped, val)], axis=1))
    _put(q1_ref, qb); _put(k1_ref, kb); _put(v1_ref, vb)
    ta = kat_ref.shape[1]
    _put(kat_ref, ka[tm - ta:]); _put(vat_ref, va[tm - ta:])
    _put(kbt_ref, kb); _put(vbt_ref, vb)
    for val, stage, r4, r16 in ((qb, sq_ref, q4_ref, q16_ref), (kb, sk_ref, k4_ref, k16_ref),
                                (vb, sv_ref, v4_ref, v16_ref)):
        for c in range(B_W // LANES):
            stage[c] = val[:, c * LANES:(c + 1) * LANES]
        for dil, out in ((4, r4), (16, r16)):
            n = tm // dil
            for r in range(dil):
                for c in range(B_W // LANES):
                    out[0, r, :, c * LANES:(c + 1) * LANES] = stage[c, pl.ds(r, n, stride=dil), :].astype(out.dtype)


def _inproj_prompt(x, sh, sc, g, w_bf, cos, sin, *, tm, tail_a, tail_b):
    b, s, d = x.shape
    nt = s // tm
    assert tail_b % tm == 0 and tail_a <= tm
    row = lambda c: pl.BlockSpec((1, tm, c), lambda bb, i: (bb, i, 0))
    res = lambda dil: pl.BlockSpec((1, dil, tm // dil, B_W), lambda bb, i: (bb, 0, i, 0))
    mod_spec = pl.BlockSpec((1, 1, d), lambda bb, i: (bb, 0, 0))
    tab_spec = pl.BlockSpec((tm, LANES), lambda bb, i: (i, 0))
    tail_a_spec = pl.BlockSpec((1, tail_a, A_KVW), lambda bb, i: (bb, 0, 0))
    tail_b_spec = pl.BlockSpec((1, tm, B_W), lambda bb, i: (bb, jnp.maximum(i - (nt - tail_b // tm), 0), 0))
    shp = lambda c, dt: jax.ShapeDtypeStruct((b, s, c), dt)
    rshp = lambda dil: jax.ShapeDtypeStruct((b, dil, s // dil, B_W), _BF)
    return pl.pallas_call(
        _inproj_prompt_kernel,
        grid=(b, nt),
        in_specs=[row(d), mod_spec, mod_spec, _resident((1, d)), _resident(w_bf.shape), tab_spec, tab_spec],
        out_specs=[row(A_QW), row(2 * A_KVW), row(2 * A_KVW)] + [res(1)] * 3 + [res(4)] * 3 + [res(16)] * 3
                  + [tail_a_spec] * 2 + [tail_b_spec] * 2,
        out_shape=[shp(A_QW, _BF), shp(2 * A_KVW, _BF), shp(2 * A_KVW, _BF)] + [rshp(1)] * 3 + [rshp(4)] * 3 + [rshp(16)] * 3
                  + [jax.ShapeDtypeStruct((b, tail_a, A_KVW), _F32)] * 2
                  + [jax.ShapeDtypeStruct((b, tail_b, B_W), _F32)] * 2,
        scratch_shapes=[pltpu.VMEM((B_W // LANES, tm, LANES), _F32)] * 3,
        compiler_params=_params(("parallel", "arbitrary")),
        name="in_proj_prompt",
    )(x, sh, sc, g, w_bf, cos, sin)


def _inproj_sample_kernel(x_ref, sh_ref, sc_ref, g_ref, w_ref, cos_ref, sin_ref,
                          qa_ref, ka_ref, va_ref, qb_ref, kb_ref, vb_ref):
    vals = _project(x_ref, sh_ref, sc_ref, g_ref, w_ref, cos_ref, sin_ref)
    for ref, val in zip((qa_ref, ka_ref, va_ref, qb_ref, kb_ref, vb_ref), vals):
        _put(ref, val)


def _inproj_sample(x, sh, sc, g, w_bf, cos, sin):
    t, n, d = x.shape
    row = lambda c: pl.BlockSpec((1, n, c), lambda j: (j, 0, 0))
    mod_spec = pl.BlockSpec((n, d), lambda j: (0, 0))
    tab_spec = pl.BlockSpec((1, 1, LANES), lambda j: (j, 0, 0))
    widths = (A_QW, A_KVW, A_KVW, B_W, B_W, B_W)
    return pl.pallas_call(
        _inproj_sample_kernel,
        grid=(t,),
        in_specs=[row(d), mod_spec, mod_spec, _resident((1, d)), _resident(w_bf.shape), tab_spec, tab_spec],
        out_specs=[row(c) for c in widths],
        out_shape=[jax.ShapeDtypeStruct((t, n, c), _F32) for c in widths],
        compiler_params=_params(("parallel",)),
        name="in_proj_sample",
    )(x, sh, sc, g, w_bf, cos, sin)


def _band_bias(max_dist, drop_prev):
    qi = lax.broadcasted_iota(jnp.int32, (BLOCK, 2 * BLOCK), 0)
    ki = lax.broadcasted_iota(jnp.int32, (BLOCK, 2 * BLOCK), 1)
    rel = qi + BLOCK - ki
    valid = (rel >= 0) & (rel <= max_dist)
    if drop_prev is not None:
        valid = valid & ((ki >= BLOCK) | jnp.logical_not(drop_prev))
    return jnp.where(valid, 0.0, NEG).astype(_F32)


def _kv_window(prev_ref, cur_ref, j, lead, cols):
    if j == 0:
        return jnp.concatenate([prev_ref[lead + (slice(None), cols)], cur_ref[lead + (slice(0, BLOCK), cols)]], axis=0)
    return cur_ref[lead + (slice((j - 1) * BLOCK, (j + 1) * BLOCK), cols)]


def _stat_columns(stats, cols, acc):
    lane = lax.broadcasted_iota(jnp.int32, (1, acc.shape[1]), 1)
    for col, stat in zip(cols, stats):
        acc = jnp.where(lane == col, stat, acc)
    return acc


def _band_attn_kernel(q_ref, kp_ref, kc_ref, vp_ref, vc_ref, o_ref, ml_ref, *, max_dist, nq):
    first_step = pl.program_id(2) == 0
    bias = _band_bias(max_dist, None)
    bias0 = _band_bias(max_dist, first_step)
    lane = lax.broadcasted_iota(jnp.int32, (1, LANES), 1)
    lo = lane < HEAD_DIM
    zero = jnp.zeros((), _BF)
    for rr, j in [(rr, j) for rr in range(q_ref.shape[1]) for j in range(nq)]:
        rows = slice(j * BLOCK, (j + 1) * BLOCK)
        bj = bias0 if j == 0 else bias
        bj = jnp.concatenate([bj, bj], axis=0)
        ml_all = jnp.zeros((BLOCK, LANES), _F32)
        for c in range(B_W // LANES):
            cs = slice(c * LANES, (c + 1) * LANES)
            q2 = q_ref[0, rr, rows, cs]
            k2 = _kv_window(kp_ref, kc_ref, j, (0, rr), cs)
            v2 = _kv_window(vp_ref, vc_ref, j, (0, rr), cs)
            qs = jnp.concatenate([jnp.where(lo, q2, zero), jnp.where(lo, zero, q2)], axis=0)
            s = lax.dot_general(qs, k2, (((1,), (1,)), ((), ())), preferred_element_type=_F32) + bj
            m = jnp.max(s, axis=-1, keepdims=True)
            p = jnp.exp(s - m)
            l = jnp.sum(p, axis=-1, keepdims=True)
            pv = jnp.dot(p.astype(_BF), v2, preferred_element_type=_F32)
            o_ref[0, rr, rows, cs] = jnp.where(lo, pv[:BLOCK], pv[BLOCK:]).astype(o_ref.dtype)
            ml_all = _stat_columns((m[:BLOCK], m[BLOCK:], l[:BLOCK], l[BLOCK:]),
                                   (2 * c, 2 * c + 1, B_HEADS + 2 * c, B_HEADS + 2 * c + 1), ml_all)
        ml_ref[0, rr, rows, :] = ml_all


def _band_attn(q, k, v, dil):
    b, _, m, w = q.shape
    nq = min(Q_BLOCKS, m // BLOCK)
    tq = nq * BLOCK
    rb = min(dil, Q_BLOCKS // nq)
    cur = pl.BlockSpec((1, rb, tq, w), lambda bb, r, i: (bb, r, i, 0))
    prev = pl.BlockSpec((1, rb, BLOCK, w), lambda bb, r, i: (bb, r, jnp.maximum(i * nq - 1, 0), 0))
    window, _ = [p for p in B_PATTERNS if p[1] == dil][0]
    return pl.pallas_call(
        functools.partial(_band_attn_kernel, max_dist=window // dil, nq=nq),
        grid=(b, dil // rb, m // tq),
        in_specs=[cur, prev, cur, prev, cur],
        out_specs=[cur, pl.BlockSpec((1, rb, tq, LANES), lambda bb, r, i: (bb, r, i, 0))],
        out_shape=[jax.ShapeDtypeStruct((b, dil, m, w), _BF),
                   jax.ShapeDtypeStruct((b, dil, m, LANES), _F32)],
        compiler_params=_params(("parallel", "parallel", "arbitrary")),
        name=f"band_attn_d{dil}",
    )(q, k, k, v, v)


def _swa_kernel(sink_ref, q_ref, kp_ref, kc_ref, vp_ref, vc_ref, o_ref, l_ref, *, nq):
    first_step = pl.program_id(1) == 0
    bias = _band_bias(A_WINDOW - 1, None)
    bias0 = _band_bias(A_WINDOW - 1, first_step)
    lane = lax.broadcasted_iota(jnp.int32, (1, LANES), 1)
    lo = lane < HEAD_DIM
    zero = jnp.zeros((), _BF)
    col0 = lax.broadcasted_iota(jnp.int32, (1, 2 * BLOCK), 1) == 0
    for j in range(nq):
        rows = slice(j * BLOCK, (j + 1) * BLOCK)
        bj = bias0 if j == 0 else bias
        bj = jnp.concatenate([bj, bj], axis=0)
        l_all = jnp.zeros((BLOCK, A_Q_HEADS), _F32)
        for c in range(A_QW // LANES):
            g = (2 * c) // A_GROUP
            gs = slice(g * LANES, (g + 1) * LANES)
            q2 = q_ref[0, rows, c * LANES:(c + 1) * LANES]
            k2 = _kv_window(kp_ref, kc_ref, j, (0,), gs)
            v2 = _kv_window(vp_ref, vc_ref, j, (0,), gs)
            qs = jnp.concatenate([jnp.where(lo, q2, zero), jnp.where(lo, zero, q2)], axis=0)
            s = lax.dot_general(qs, k2, (((1,), (1,)), ((), ())), preferred_element_type=_F32) + bj
            s = jnp.concatenate([jnp.where(col0, sink_ref[2 * c + half], s[half * BLOCK:(half + 1) * BLOCK])
                                 for half in range(2)], axis=0)
            m = jnp.max(s, axis=-1, keepdims=True)
            p = jnp.exp(s - m)
            l = jnp.sum(p, axis=-1, keepdims=True)
            p = jnp.where(col0, 0.0, p).astype(_BF)
            pv = jnp.dot(p, v2, preferred_element_type=_F32)
            _put(o_ref.at[0, rows, c * LANES:(c + 1) * LANES], jnp.where(lo, pv[:BLOCK], pv[BLOCK:]))
            l_all = _stat_columns((l[:BLOCK], l[BLOCK:]), (2 * c, 2 * c + 1), l_all)
        l_ref[0, rows, :] = l_all


def _swa_prompt(q, k, v, sinks):
    b, s, _ = q.shape
    nq = Q_BLOCKS
    tq = nq * BLOCK
    assert A_WINDOW <= BLOCK, "the sink logit borrows a key column that no query row can reach"
    qspec = pl.BlockSpec((1, tq, A_QW), lambda bb, i: (bb, i, 0))
    cur = pl.BlockSpec((1, tq, 2 * A_KVW), lambda bb, i: (bb, i, 0))
    prev = pl.BlockSpec((1, BLOCK, 2 * A_KVW), lambda bb, i: (bb, jnp.maximum(i * nq - 1, 0), 0))
    return pl.pallas_call(
        functools.partial(_swa_kernel, nq=nq),
        grid=(b, s // tq),
        in_specs=[pl.BlockSpec(memory_space=pltpu.SMEM), qspec, prev, cur, prev, cur],
        out_specs=[qspec, pl.BlockSpec((1, tq, A_Q_HEADS), lambda bb, i: (bb, i, 0))],
        out_shape=[jax.ShapeDtypeStruct((b, s, A_QW), _BF), jax.ShapeDtypeStruct((b, s, A_Q_HEADS), _F32)],
        compiler_params=_params(("parallel", "arbitrary")),
        name="swa_prompt",
    )(sinks, q, k, k, v, v)


def _roll_in(x_ref, new8_ref, o_ref, t_new):
    rows, width = x_ref.shape
    new8 = new8_ref[0, 0]
    tail = jnp.concatenate([jnp.zeros((LANES - new8.shape[0], new8.shape[1]), _F32), new8], axis=0).T[:rows]
    lane = lax.broadcasted_iota(jnp.int32, (1, LANES), 1)
    keep = lane < LANES - t_new
    nxt = pltpu.roll(x_ref[:, 0:LANES], LANES - t_new, 1)
    for c in range(width // LANES):
        cur = nxt
        if c + 1 < width // LANES:
            nxt = pltpu.roll(x_ref[:, (c + 1) * LANES:(c + 2) * LANES], LANES - t_new, 1)
        else:
            nxt = tail
        o_ref[:, c * LANES:(c + 1) * LANES] = jnp.where(keep, cur, nxt)


def _sample_half(g, refs, *, t_new):
    (sink_ref, qa_ref, kan_ref, van_ref, cak_ref, cav_ref, ak8_ref, av8_ref,
     qb_ref, kb8_ref, vb8_ref, cbk_ref, cbv_ref,
     oa_ref, ob_ref, nak_ref, nav_ref, nbk_ref, nbv_ref) = refs
    wa, wb = cak_ref.shape[-1], cbk_ref.shape[-1]

    rows_a = t_new * A_GROUP
    ra = lax.broadcasted_iota(jnp.int32, (rows_a, 1), 0)
    qi_a = _div(ra, A_GROUP)
    rel_a = wa + qi_a - lax.broadcasted_iota(jnp.int32, (1, wa), 1)
    valid_a = (rel_a >= 0) & (rel_a < A_WINDOW)
    hd = slice(0, HEAD_DIM)
    q = qa_ref[0, 0, 0:rows_a, hd]
    s = jnp.dot(q.astype(_BF), cak_ref[0].astype(_BF), preferred_element_type=_F32)
    s = jnp.where(valid_a, s, NEG)
    s_new = [jnp.where(qi_a >= j, jnp.sum(q * kan_ref[0, 0, j:j + 1, hd], axis=-1, keepdims=True), NEG)
             for j in range(t_new)]
    sink = jnp.zeros((rows_a, 1), _F32)
    for u in range(A_GROUP):
        sink = jnp.where((ra & (A_GROUP - 1)) == u, sink_ref[g * A_GROUP + u], sink)
    m = functools.reduce(jnp.maximum, [jnp.max(s, axis=-1, keepdims=True), sink] + s_new)
    p = jnp.exp(s - m)
    p_new = [jnp.exp(z - m) for z in s_new]
    l = functools.reduce(jnp.add, [jnp.sum(p, axis=-1, keepdims=True), jnp.exp(sink - m)] + p_new)
    o = lax.dot_general(p.astype(_BF), cav_ref[0].astype(_BF), (((1,), (1,)), ((), ())),
                        preferred_element_type=_F32)
    for j in range(t_new):
        o = o + p_new[j] * van_ref[0, 0, j:j + 1, hd]
    oa_ref[0, 0, 0:rows_a, hd] = o * (1.0 / l)
    _roll_in(cak_ref.at[0], ak8_ref, nak_ref.at[0], t_new)
    _roll_in(cav_ref.at[0], av8_ref, nav_ref.at[0], t_new)

    width_b = cbk_ref.shape[1]
    r0 = kb8_ref.shape[2] - t_new
    rows_b = t_new * SUBLANES
    sub = lax.broadcasted_iota(jnp.int32, (SUBLANES, width_b), 0)
    own = _div(lax.broadcasted_iota(jnp.int32, (SUBLANES, width_b), 1), HEAD_DIM) == sub
    qbd = jnp.concatenate([jnp.where(own, qb_ref[0, 0, i:i + 1, :], 0.0) for i in range(t_new)], axis=0)
    s_all = jnp.dot(qbd.astype(_BF), cbk_ref[0].astype(_BF), preferred_element_type=_F32)
    qi = _div(lax.broadcasted_iota(jnp.int32, (rows_b, 1), 0), SUBLANES)
    s_new = [jnp.sum(qbd * kb8_ref[0, 0, r0 + j:r0 + j + 1, :], axis=-1, keepdims=True) for j in range(t_new)]
    pats = []
    for window, dil in B_PATTERNS:
        lo_lane = max(wb - (-(-window // LANES) * LANES), 0)
        wl = lo_lane + lax.broadcasted_iota(jnp.int32, (1, wb - lo_lane), 1)
        rel = wb + qi - wl
        valid = (rel <= window) & _multiple(rel, dil)
        s = jnp.where(valid, s_all[:, lo_lane:], NEG)
        sn = [jnp.where((qi - j >= 0) & _multiple(qi - j, dil), s_new[j], NEG) for j in range(t_new)]
        m = functools.reduce(jnp.maximum, [jnp.max(s, axis=-1, keepdims=True)] + sn)
        p = jnp.exp(s - m)
        pn = [jnp.exp(z - m) for z in sn]
        l = functools.reduce(jnp.add, [jnp.sum(p, axis=-1, keepdims=True)] + pn)
        pats.append((lo_lane, p, pn, l, m + jnp.log(l)))
    mx = functools.reduce(jnp.maximum, [t[4] for t in pats])
    es = [jnp.exp(t[4] - mx) for t in pats]
    den = functools.reduce(jnp.add, es)
    coef = [e / (den * t[3]) for e, t in zip(es, pats)]
    starts = sorted({t[0] for t in pats} | {wb})
    blocks = []
    for a, b in zip(starts[:-1], starts[1:]):
        acc = None
        for c, (lo_lane, p, _, _, _) in zip(coef, pats):
            if lo_lane <= a:
                term = c * p[:, a - lo_lane:b - lo_lane]
                acc = term if acc is None else acc + term
        blocks.append(acc)
    lead = starts[0]
    p_comb = jnp.concatenate(blocks, axis=1) if len(blocks) > 1 else blocks[0]
    o_full = lax.dot_general(p_comb.astype(_BF), cbv_ref[0, :, lead:].astype(_BF), (((1,), (1,)), ((), ())),
                             preferred_element_type=_F32)
    for j in range(t_new):
        pj = functools.reduce(jnp.add, [c * t[2][j] for c, t in zip(coef, pats)])
        o_full = o_full + pj * vb8_ref[0, 0, r0 + j:r0 + j + 1, :]
    for i in range(t_new):
        blk = jnp.where(own, o_full[i * SUBLANES:(i + 1) * SUBLANES], 0.0)
        ob_ref[0, 0, i:i + 1, :] = jnp.sum(blk, axis=0, keepdims=True)
    _roll_in(cbk_ref.at[0], kb8_ref, nbk_ref, t_new)
    _roll_in(cbv_ref.at[0], vb8_ref, nbv_ref, t_new)


def _ffn_chunk(h, wg_ref, wu_ref, wd_ref, cs):
    gt = jnp.dot(h, wg_ref[:, cs], preferred_element_type=_F32)
    up = jnp.dot(h, wu_ref[:, cs], preferred_element_type=_F32)
    act = (gt * jax.nn.sigmoid(gt) * up).astype(_BF)
    return jnp.dot(act, wd_ref[cs, :], preferred_element_type=_F32)


def _attn_residual(x, oa, ob, gta, shf, scf, goa_ref, gob_ref, wo_ref, gf_ref):
    merged = jnp.concatenate([_rms(oa, goa_ref[...]), _rms(ob, gob_ref[...])], axis=1)
    x1 = x + gta * jnp.dot(merged.astype(_BF), wo_ref[...], preferred_element_type=_F32)
    return x1, (_rms(x1, gf_ref[...]) * (1.0 + scf) + shf).astype(_BF)


_TWO = lambda r: r[...].reshape(-1, r.shape[-1])
_FF_CUTS = (0, 384, 1152, 2048, 2816)
_PHASES = len(_FF_CUTS) - 1


def _ffn_and_sample_kernel(*refs, t_new):
    (x_ref, oa_ref, la_ref, o1_ref, o4_ref, o16_ref, ml1_ref, ml2_ref, ml3_ref, e_ref,
     gta_ref, shf_ref, scf_ref, gtf_ref, goa_ref, gob_ref, wo_ref, gf_ref, wg_ref, wu_ref, wd_ref, gfin_ref) = refs[:22]
    sample_in = refs[22:35]
    y_ref = refs[35]
    oa_s_ref, ob_s_ref, nak_ref, nav_ref, nbk_ref, nbv_ref = refs[36:42]
    s4_ref, s16_ref, x1_ref, h_ref, acc_ref, t4_ref, t16_ref = refs[42:]
    step = pl.program_id(0)
    half = step % 2
    phase = step % _PHASES
    tm = x_ref.shape[1]
    sample_refs = tuple(sample_in) + (oa_s_ref, ob_s_ref, nak_ref, nav_ref, nbk_ref.at[0], nbv_ref.at[0])

    def mix_and_project():
        for dil, src, stage, stat_src, stat_stage in ((4, o4_ref, s4_ref, ml2_ref, t4_ref),
                                                      (16, o16_ref, s16_ref, ml3_ref, t16_ref)):
            for r in range(dil):
                rows_r = pl.ds(r, tm // dil, stride=dil)
                stat_stage[rows_r, :] = stat_src[0, r]
                for c in range(B_W // LANES):
                    stage[c, rows_r, :] = src[0, r, :, c * LANES:(c + 1) * LANES].astype(_F32)
        gather = lambda stage: jnp.concatenate([stage[c] for c in range(B_W // LANES)], axis=1)
        outs = [_TWO(o1_ref).astype(_F32), gather(s4_ref), gather(s16_ref)]

        def widen(per_head):
            hi = per_head.astype(_BF)
            lo = (per_head - hi.astype(_F32)).astype(_BF)
            return (jnp.dot(hi, e_ref[...], preferred_element_type=_F32)
                    + jnp.dot(lo, e_ref[...], preferred_element_type=_F32))

        stats = (ml1_ref.at[0, 0], t4_ref, t16_ref)
        ms = [r[:, 0:B_HEADS] for r in stats]
        ls = [r[:, B_HEADS:2 * B_HEADS] for r in stats]
        lses = [m + jnp.log(l) for m, l in zip(ms, ls)]
        mx = jnp.maximum(jnp.maximum(lses[0], lses[1]), lses[2])
        es = [jnp.exp(z - mx) for z in lses]
        den = es[0] + es[1] + es[2]
        ob = jnp.zeros((tm, B_W), _F32)
        for e, l, o in zip(es, ls, outs):
            ob = ob + widen(e / (den * l)) * o
        oa = widen(1.0 / _TWO(la_ref)) * _TWO(oa_ref).astype(_F32)
        x1, h = _attn_residual(_TWO(x_ref), oa, ob, _TWO(gta_ref), _TWO(shf_ref), _TWO(scf_ref),
                               goa_ref, gob_ref, wo_ref, gf_ref)
        x1_ref[...] = x1
        h_ref[...] = h

    for k in range(_PHASES):
        @pl.when(phase == k)
        def _(k=k):
            _sample_half(half, sample_refs, t_new=t_new)
            if k == 0:
                mix_and_project()
            part = _ffn_chunk(h_ref[...], wg_ref, wu_ref, wd_ref, slice(_FF_CUTS[k], _FF_CUTS[k + 1]))
            if k == 0:
                acc_ref[...] = part
            elif k < _PHASES - 1:
                acc_ref[...] += part
            else:
                x2 = x1_ref[...] + _TWO(gtf_ref) * (acc_ref[...] + part)
                y_ref[...] = _rms(x2, gfin_ref[...]).reshape(y_ref.shape)


def _ffn_and_sample(x, oa, la, obs, mls, mods, goa, gob, wo, gf, wg, wu, wd, gfin, sinks, sample_ins, *, tm):
    b, s, d = x.shape
    n = sample_ins[0].shape[0]
    tiles = s // tm
    steps = 2 * n
    assert steps == b * tiles * _PHASES and _PHASES % 2 == 0 and wg.shape[1] == _FF_CUTS[-1]
    t_new = sample_ins[0].shape[2] // A_GROUP

    def tile(j):
        t = j // _PHASES
        return t // tiles, t % tiles

    row = lambda c: pl.BlockSpec((1, tm, c), lambda j: tile(j) + (0,))
    res = lambda dil: pl.BlockSpec((1, dil, tm // dil, B_W), lambda j: (tile(j)[0], 0, tile(j)[1], 0))
    mod_spec = pl.BlockSpec((1, 1, d), lambda j: (tile(j)[0], 0, 0))
    expand = (jnp.arange(B_W)[None, :] // HEAD_DIM == jnp.arange(B_HEADS)[:, None]).astype(_BF)
    ffn_ins = [x, oa, la] + list(obs) + list(mls) + [expand] + list(mods) + [goa, gob, wo, gf, wg, wu, wd, gfin]
    ffn_specs = ([row(d), row(A_QW), row(A_Q_HEADS)] + [res(o.shape[1]) for o in obs]
                 + [pl.BlockSpec((1, o.shape[1], tm // o.shape[1], LANES), lambda j: (tile(j)[0], 0, tile(j)[1], 0))
                    for o in obs]
                 + [_resident(expand.shape)] + [mod_spec] * 4
                 + [_resident(a.shape) for a in (goa, gob, wo, gf, wg, wu, wd, gfin)])
    half_blk = lambda a: pl.BlockSpec((1, 1) + a.shape[2:], lambda j: (j // 2, j % 2) + (0,) * (a.ndim - 2))
    cache_blk = lambda a: pl.BlockSpec((1, a.shape[1] // 2, a.shape[2]), lambda j: (j // 2, j % 2, 0))
    (qa_g, ka_n, va_n, ca_k, ca_v, ak8, av8, qb_h, kb8, vb8, cb_k, cb_v) = sample_ins
    s_specs = [half_blk(qa_g), half_blk(ka_n), half_blk(va_n), cache_blk(ca_k), cache_blk(ca_v), half_blk(ak8),
               half_blk(av8), half_blk(qb_h), half_blk(kb8), half_blk(vb8), cache_blk(cb_k), cache_blk(cb_v)]
    o_specs = [half_blk(qa_g), half_blk(qb_h), cache_blk(ca_k), cache_blk(ca_v), cache_blk(cb_k), cache_blk(cb_v)]
    sample_outs = [jax.ShapeDtypeStruct(a.shape, _F32) for a in (qa_g, qb_h, ca_k, ca_v, cb_k, cb_v)]
    stage = pltpu.VMEM((B_W // LANES, tm, LANES), _F32)
    return pl.pallas_call(
        functools.partial(_ffn_and_sample_kernel, t_new=t_new),
        grid=(steps,),
        in_specs=ffn_specs + [pl.BlockSpec(memory_space=pltpu.SMEM)] + s_specs,
        out_specs=[row(d)] + o_specs,
        out_shape=[jax.ShapeDtypeStruct((b, s, d), _F32)] + sample_outs,
        scratch_shapes=[stage, stage, pltpu.VMEM((tm, d), _F32), pltpu.VMEM((tm, d), _BF), pltpu.VMEM((tm, d), _F32),
                        pltpu.VMEM((tm, LANES), _F32), pltpu.VMEM((tm, LANES), _F32)],
        compiler_params=_params(("arbitrary",)),
        name="ffn_and_sample",
    )(*ffn_ins, sinks, *sample_ins)


def _merge_ffn_sample_kernel(x_ref, oa_ref, ob_ref, gta_ref, shf_ref, scf_ref, gtf_ref,
                             goa_ref, gob_ref, wo_ref, gf_ref, wg_ref, wu_ref, wd_ref, gfin_ref, y_ref):
    per_row = lambda r: jnp.concatenate([r[...]] * x_ref.shape[0], axis=0)
    x1, h = _attn_residual(_TWO(x_ref), _TWO(oa_ref), _TWO(ob_ref), per_row(gta_ref), per_row(shf_ref),
                           per_row(scf_ref), goa_ref, gob_ref, wo_ref, gf_ref)
    acc = jnp.zeros_like(x1)
    for k in range(_PHASES):
        acc = acc + _ffn_chunk(h, wg_ref, wu_ref, wd_ref, slice(_FF_CUTS[k], _FF_CUTS[k + 1]))
    y_ref[...] = _rms(x1 + per_row(gtf_ref) * acc, gfin_ref[...]).reshape(y_ref.shape)


def _merge_ffn_sample(x, oa, ob, mods, goa, gob, wo, gf, wg, wu, wd, gfin):
    t, n, d = x.shape
    whole = lambda a: pl.BlockSpec(a.shape, lambda j: (0,) * a.ndim)
    weights = (goa, gob, wo, gf, wg, wu, wd, gfin)
    ins = (x, oa, ob, *mods, *weights)
    return pl.pallas_call(
        _merge_ffn_sample_kernel,
        grid=(1,),
        in_specs=[whole(a) for a in ins[:7]] + [_resident(a.shape) for a in weights],
        out_specs=whole(x),
        out_shape=jax.ShapeDtypeStruct((t, n, d), _F32),
        compiler_params=_params(("arbitrary",)),
        name="merge_ffn_sample",
    )(*ins)


def _rope_tables(pos):
    half = HEAD_DIM // 2
    inv = jnp.exp(-math.log(ROPE_THETA) * jnp.arange(half, dtype=_F32) * (2.0 / HEAD_DIM))
    ang = pos.astype(_F32)[:, None] * inv[None, :]
    cos, sin = jnp.cos(ang), jnp.sin(ang)
    return jnp.tile(cos, (1, LANES // half)), jnp.tile(jnp.concatenate([-sin, sin], axis=1), (1, LANES // HEAD_DIM))


def _window_on_lanes(cache):
    _, n, w, h, dh = cache.shape
    return cache.transpose(0, 1, 3, 4, 2).reshape(n, h * dh, w)


def _window_off_lanes(t, h):
    n, _, w = t.shape
    return t.reshape(1, n, h, HEAD_DIM, w).transpose(0, 1, 4, 2, 3)


def _halves_last_rows(new_t, pad_to):
    t, n, c = new_t.shape
    v = new_t.reshape(t, n, 2, c // 2).transpose(1, 2, 0, 3)
    return jnp.pad(v, ((0, 0), (0, 0), (SUBLANES - t, 0), (0, pad_to - c // 2)))


def kernel(x_prompt, x_sample, c_prompt, c_sample, cache_a_k, cache_a_v, cache_b_k, cache_b_v, w_ada, b_ada, g_attn, w_in, sinks, g_out_a, g_out_b, w_o, g_ffn, w_gate, w_up, w_down, g_final):
    nb, s, d = x_prompt.shape
    ns, t_new, _ = x_sample.shape
    assert w_ada.shape[0] == 1, "single trunk layer"
    l = 0
    bf = lambda w: w.astype(_BF)
    row = lambda g: g.reshape(1, -1)

    c_all = jnp.concatenate([c_prompt, c_sample], axis=0)
    c_all = jnp.pad(c_all, ((0, (-c_all.shape[0]) % SUBLANES), (0, 0)))
    mod = _adaln(c_all, w_ada[l], b_ada[l].reshape(1, -1))
    mod_p = [m.reshape(nb, 1, d) for m in jnp.split(mod[:nb], 6, axis=-1)]
    mod_s = jnp.split(mod[nb:nb + ns], 6, axis=-1)

    cos_p, sin_p = _rope_tables(jnp.arange(s, dtype=jnp.int32))
    cos_s, sin_s = _rope_tables(PAST_LEN + jnp.arange(t_new, dtype=jnp.int32))
    w_in_bf = bf(w_in[l])
    ffn_w = (row(g_out_a[l]), row(g_out_b[l]), bf(w_o[l]), row(g_ffn[l]), bf(w_gate[l]), bf(w_up[l]), bf(w_down[l]),
             row(g_final))

    wa_p, wb_p = min(A_WINDOW, s), min(B_WINDOW, s)
    (qa, ka, va, q1, k1, v1, q4, k4, v4, q16, k16, v16, ka_t, va_t, kb_t, vb_t) = _inproj_prompt(
        x_prompt, mod_p[0], mod_p[1], row(g_attn[l]), w_in_bf, cos_p, sin_p, tm=512, tail_a=wa_p, tail_b=wb_p)
    xs_t = x_sample.transpose(1, 0, 2)
    qa_s, ka_s, va_s, qb_s, kb_s, vb_s = _inproj_sample(
        xs_t, mod_s[0], mod_s[1], row(g_attn[l]), w_in_bf, cos_s.reshape(t_new, 1, LANES),
        sin_s.reshape(t_new, 1, LANES))

    oa, la = _swa_prompt(qa, ka, va, sinks[l])
    obs, mls = [], []
    for (q, k, v), (_, dil) in zip(((q1, k1, v1), (q4, k4, v4), (q16, k16, v16)), B_PATTERNS):
        o, ml = _band_attn(q, k, v, dil)
        obs.append(o)
        mls.append(ml)

    qa_g = (qa_s.reshape(t_new, ns, A_KV_HEADS, A_GROUP, HEAD_DIM).transpose(1, 2, 0, 3, 4)
            .reshape(ns, A_KV_HEADS, t_new * A_GROUP, HEAD_DIM))
    kv_g = lambda t: t.reshape(t_new, ns, A_KV_HEADS, HEAD_DIM).transpose(1, 2, 0, 3)
    qb_h = qb_s.reshape(t_new, ns, 2, B_W // 2).transpose(1, 2, 0, 3)
    sample_ins = (qa_g, kv_g(ka_s), kv_g(va_s), _window_on_lanes(cache_a_k), _window_on_lanes(cache_a_v),
                  _halves_last_rows(ka_s, LANES), _halves_last_rows(va_s, LANES), qb_h,
                  _halves_last_rows(kb_s, B_W // 2), _halves_last_rows(vb_s, B_W // 2),
                  _window_on_lanes(cache_b_k), _window_on_lanes(cache_b_v))
    y_prompt, oa_g, ob_h, na_k, na_v, nb_k, nb_v = _ffn_and_sample(
        x_prompt, oa, la, obs, mls, mod_p[2:6], *ffn_w, sinks[l], sample_ins, tm=256)
    pa_k = ka_t.reshape(1, nb, wa_p, A_KV_HEADS, HEAD_DIM)
    pa_v = va_t.reshape(1, nb, wa_p, A_KV_HEADS, HEAD_DIM)
    pb_k = kb_t.reshape(1, nb, wb_p, B_HEADS, HEAD_DIM)
    pb_v = vb_t.reshape(1, nb, wb_p, B_HEADS, HEAD_DIM)

    oa_t = (oa_g.reshape(ns, A_KV_HEADS, t_new, A_GROUP, HEAD_DIM).transpose(2, 0, 1, 3, 4)
            .reshape(t_new, ns, A_QW))
    ob_t = ob_h.transpose(2, 0, 1, 3).reshape(t_new, ns, B_W)
    y_s = _merge_ffn_sample(xs_t, oa_t, ob_t, mod_s[2:6], *ffn_w)
    y_sample = y_s.transpose(1, 0, 2)
    sa_k, sa_v = _window_off_lanes(na_k, A_KV_HEADS), _window_off_lanes(na_v, A_KV_HEADS)
    sb_k, sb_v = _window_off_lanes(nb_k, B_HEADS), _window_off_lanes(nb_v, B_HEADS)

    return (y_prompt, y_sample, pa_k, pa_v, pb_k, pb_v, sa_k, sa_v, sb_k, sb_v)
```

```python
import functools
import math

import jax
import jax.numpy as jnp
from jax import lax
from jax.experimental import pallas as pl
from jax.experimental.pallas import tpu as pltpu

HEAD_DIM = 64
A_Q_HEADS = 8
A_KV_HEADS = 2
A_GROUP = A_Q_HEADS // A_KV_HEADS
B_HEADS = 8
A_WINDOW = 128
BLOCK = 128
B_PATTERNS = ((128, 1), (512, 4), (2048, 16))
B_WINDOW = 2048
PAST_LEN = 8192
ROPE_THETA = 10000.0
EPS = 1e-6
A_QW = A_Q_HEADS * HEAD_DIM
A_KVW = A_KV_HEADS * HEAD_DIM
B_W = B_HEADS * HEAD_DIM
SCALE = HEAD_DIM ** -0.5
LANES = 128
SUBLANES = 8
NEG = -1e30
VMEM_LIMIT = 56 * 1024 * 1024
Q_BLOCKS = 32

_BF = jnp.bfloat16
_F32 = jnp.float32


def _params(sem, vmem=VMEM_LIMIT):
    return pltpu.CompilerParams(dimension_semantics=sem, vmem_limit_bytes=vmem)


def _resident(shape):
    nd = len(shape)
    return pl.BlockSpec(shape, lambda *_: (0,) * nd, pipeline_mode=pl.Buffered(1))


def _rms(x, g):
    return x * lax.rsqrt(jnp.mean(x * x, axis=-1, keepdims=True) + EPS) * g


def _log2(n):
    assert n > 0 and n & (n - 1) == 0, "power of two expected"
    return n.bit_length() - 1


def _div(x, n):
    return x >> _log2(n)


def _multiple(x, n):
    _log2(n)
    return (x & (n - 1)) == 0


def _adaln_kernel(c_ref, w_ref, b_ref, o_ref):
    c = c_ref[...]
    a = (c * jax.nn.sigmoid(c)).astype(_BF)
    o_ref[...] = jnp.dot(a, w_ref[...].astype(_BF), preferred_element_type=_F32) + b_ref[...]


def _adaln(c_all, w, b):
    m, d = c_all.shape
    n = w.shape[1]
    tn = 1536
    return pl.pallas_call(
        _adaln_kernel,
        grid=(n // tn,),
        in_specs=[pl.BlockSpec((m, d), lambda j: (0, 0)),
                  pl.BlockSpec((d, tn), lambda j: (0, j)),
                  pl.BlockSpec((1, tn), lambda j: (0, j))],
        out_specs=pl.BlockSpec((m, tn), lambda j: (0, j)),
        out_shape=jax.ShapeDtypeStruct((m, n), _F32),
        compiler_params=_params(("arbitrary",)),
        name="adaln",
    )(c_all, w, b)


def _rope(x, cos, sin_signed, first_half):
    sw = jnp.where(first_half, pltpu.roll(x, 96, 1), pltpu.roll(x, 32, 1))
    return x * cos + sw * sin_signed


def _project(x_ref, sh_ref, sc_ref, g_ref, w_ref, cos_ref, sin_ref):
    d = x_ref.shape[-1]
    x = x_ref[...].reshape(-1, d)
    sh = sh_ref[...].reshape(-1, d)
    sc = sc_ref[...].reshape(-1, d)
    h = _rms(x, g_ref[...]) * (1.0 + sc) + sh
    proj = jnp.dot(h.astype(_BF), w_ref[...], preferred_element_type=_F32)
    cos = cos_ref[...].reshape(-1, LANES)
    sin = sin_ref[...].reshape(-1, LANES)
    lane = lax.broadcasted_iota(jnp.int32, (1, LANES), 1)
    first_half = (lane & (HEAD_DIM - 1)) < (HEAD_DIM // 2)

    def rot(col0, ncols):
        parts = [_rope(proj[:, col0 + j * LANES: col0 + (j + 1) * LANES], cos, sin, first_half)
                 for j in range(ncols // LANES)]
        return parts[0] if len(parts) == 1 else jnp.concatenate(parts, axis=1)

    o = 0
    qa = rot(o, A_QW) * SCALE; o += A_QW
    ka = rot(o, A_KVW); o += A_KVW
    va = proj[:, o:o + A_KVW]; o += A_KVW
    qb = rot(o, B_W) * SCALE; o += B_W
    kb = rot(o, B_W); o += B_W
    vb = proj[:, o:o + B_W]
    return qa, ka, va, qb, kb, vb


def _put(ref, val):
    ref[...] = val.astype(ref.dtype).reshape(ref.shape)


def _inproj_prompt_kernel(x_ref, sh_ref, sc_ref, g_ref, w_ref, cos_ref, sin_ref,
                          qa_ref, ka_ref, va_ref, q1_ref, k1_ref, v1_ref, q4_ref, k4_ref, v4_ref,
                          q16_ref, k16_ref, v16_ref, kat_ref, vat_ref, kbt_ref, vbt_ref,
                          sq_ref, sk_ref, sv_ref):
    qa, ka, va, qb, kb, vb = _project(x_ref, sh_ref, sc_ref, g_ref, w_ref, cos_ref, sin_ref)
    tm = qa.shape[0]
    _put(qa_ref, qa)
    lo = lax.broadcasted_iota(jnp.int32, (1, LANES), 1) < HEAD_DIM
    for ref, val in ((ka_ref, ka), (va_ref, va)):
        swapped = pltpu.roll(val, HEAD_DIM, 1)
        _put(ref, jnp.concatenate([jnp.where(lo, val, swapped), jnp.where(lo, swapped, val)], axis=1))
    _put(q1_ref, qb); _put(k1_ref, kb); _put(v1_ref, vb)
    ta = kat_ref.shape[1]
    _put(kat_ref, ka[tm - ta:]); _put(vat_ref, va[tm - ta:])
    _put(kbt_ref, kb); _put(vbt_ref, vb)
    for val, stage, r4, r16 in ((qb, sq_ref, q4_ref, q16_ref), (kb, sk_ref, k4_ref, k16_ref),
                                (vb, sv_ref, v4_ref, v16_ref)):
        for c in range(B_W // LANES):
            stage[c] = val[:, c * LANES:(c + 1) * LANES]
        for dil, out in ((4, r4), (16, r16)):
            n = tm // dil
            for r in range(dil):
                for c in range(B_W // LANES):
                    out[0, r, :, c * LANES:(c + 1) * LANES] = stage[c, pl.ds(r, n, stride=dil), :].astype(out.dtype)


def _inproj_prompt(x, sh, sc, g, w_bf, cos, sin, *, tm, tail_a, tail_b):
    b, s, d = x.shape
    nt = s // tm
    assert tail_b % tm == 0 and tail_a <= tm
    row = lambda c: pl.BlockSpec((1, tm, c), lambda bb, i: (bb, i, 0))
    res = lambda dil: pl.BlockSpec((1, dil, tm // dil, B_W), lambda bb, i: (bb, 0, i, 0))
    mod_spec = pl.BlockSpec((1, 1, d), lambda bb, i: (bb, 0, 0))
    tab_spec = pl.BlockSpec((tm, LANES), lambda bb, i: (i, 0))
    tail_a_spec = pl.BlockSpec((1, tail_a, A_KVW), lambda bb, i: (bb, 0, 0))
    tail_b_spec = pl.BlockSpec((1, tm, B_W), lambda bb, i: (bb, jnp.maximum(i - (nt - tail_b // tm), 0), 0))
    shp = lambda c, dt: jax.ShapeDtypeStruct((b, s, c), dt)
    rshp = lambda dil: jax.ShapeDtypeStruct((b, dil, s // dil, B_W), _BF)
    return pl.pallas_call(
        _inproj_prompt_kernel,
        grid=(b, nt),
        in_specs=[row(d), mod_spec, mod_spec, _resident((1, d)), _resident(w_bf.shape), tab_spec, tab_spec],
        out_specs=[row(A_QW), row(2 * A_KVW), row(2 * A_KVW)] + [res(1)] * 3 + [res(4)] * 3 + [res(16)] * 3
                  + [tail_a_spec] * 2 + [tail_b_spec] * 2,
        out_shape=[shp(A_QW, _BF), shp(2 * A_KVW, _BF), shp(2 * A_KVW, _BF)] + [rshp(1)] * 3 + [rshp(4)] * 3 + [rshp(16)] * 3
                  + [jax.ShapeDtypeStruct((b, tail_a, A_KVW), _F32)] * 2
                  + [jax.ShapeDtypeStruct((b, tail_b, B_W), _F32)] * 2,
        scratch_shapes=[pltpu.VMEM((B_W // LANES, tm, LANES), _F32)] * 3,
        compiler_params=_params(("parallel", "arbitrary")),
        name="in_proj_prompt",
    )(x, sh, sc, g, w_bf, cos, sin)


def _inproj_sample_kernel(x_ref, sh_ref, sc_ref, g_ref, w_ref, cos_ref, sin_ref,
                          qa_ref, ka_ref, va_ref, qb_ref, kb_ref, vb_ref):
    vals = _project(x_ref, sh_ref, sc_ref, g_ref, w_ref, cos_ref, sin_ref)
    for ref, val in zip((qa_ref, ka_ref, va_ref, qb_ref, kb_ref, vb_ref), vals):
        _put(ref, val)


def _inproj_sample(x, sh, sc, g, w_bf, cos, sin):
    t, n, d = x.shape
    row = lambda c: pl.BlockSpec((1, n, c), lambda j: (j, 0, 0))
    mod_spec = pl.BlockSpec((n, d), lambda j: (0, 0))
    tab_spec = pl.BlockSpec((1, 1, LANES), lambda j: (j, 0, 0))
    widths = (A_QW, A_KVW, A_KVW, B_W, B_W, B_W)
    return pl.pallas_call(
        _inproj_sample_kernel,
        grid=(t,),
        in_specs=[row(d), mod_spec, mod_spec, _resident((1, d)), _resident(w_bf.shape), tab_spec, tab_spec],
        out_specs=[row(c) for c in widths],
        out_shape=[jax.ShapeDtypeStruct((t, n, c), _F32) for c in widths],
        compiler_params=_params(("parallel",)),
        name="in_proj_sample",
    )(x, sh, sc, g, w_bf, cos, sin)


def _band_bias(max_dist, drop_prev):
    qi = lax.broadcasted_iota(jnp.int32, (BLOCK, 2 * BLOCK), 0)
    ki = lax.broadcasted_iota(jnp.int32, (BLOCK, 2 * BLOCK), 1)
    rel = qi + BLOCK - ki
    valid = (rel >= 0) & (rel <= max_dist)
    if drop_prev is not None:
        valid = valid & ((ki >= BLOCK) | jnp.logical_not(drop_prev))
    return jnp.where(valid, 0.0, NEG).astype(_F32)


def _kv_window(prev_ref, cur_ref, j, lead, cols):
    if j == 0:
        return jnp.concatenate([prev_ref[lead + (slice(None), cols)], cur_ref[lead + (slice(0, BLOCK), cols)]], axis=0)
    return cur_ref[lead + (slice((j - 1) * BLOCK, (j + 1) * BLOCK), cols)]


def _stat_columns(stats, cols, acc):
    lane = lax.broadcasted_iota(jnp.int32, (1, acc.shape[1]), 1)
    for col, stat in zip(cols, stats):
        acc = jnp.where(lane == col, stat, acc)
    return acc


def _band_attn_kernel(q_ref, kp_ref, kc_ref, vp_ref, vc_ref, o_ref, ml_ref, *, max_dist, nq):
    first_step = pl.program_id(2) == 0
    bias = _band_bias(max_dist, None)
    bias0 = _band_bias(max_dist, first_step)
    lane = lax.broadcasted_iota(jnp.int32, (1, LANES), 1)
    lo = lane < HEAD_DIM
    zero = jnp.zeros((), _BF)
    for rr, j in [(rr, j) for rr in range(q_ref.shape[1]) for j in range(nq)]:
        rows = slice(j * BLOCK, (j + 1) * BLOCK)
        bj = bias0 if j == 0 else bias
        bj = jnp.concatenate([bj, bj], axis=0)
        ml_all = jnp.zeros((BLOCK, LANES), _F32)
        for c in range(B_W // LANES):
            cs = slice(c * LANES, (c + 1) * LANES)
            q2 = q_ref[0, rr, rows, cs]
            k2 = _kv_window(kp_ref, kc_ref, j, (0, rr), cs)
            v2 = _kv_window(vp_ref, vc_ref, j, (0, rr), cs)
            qs = jnp.concatenate([jnp.where(lo, q2, zero), jnp.where(lo, zero, q2)], axis=0)
            s = lax.dot_general(qs, k2, (((1,), (1,)), ((), ())), preferred_element_type=_F32) + bj
            m = jnp.max(s, axis=-1, keepdims=True)
            p = jnp.exp(s - m)
            l = jnp.sum(p, axis=-1, keepdims=True)
            pv = jnp.dot(p.astype(_BF), v2, preferred_element_type=_F32)
            o_ref[0, rr, rows, cs] = jnp.where(lo, pv[:BLOCK], pv[BLOCK:]).astype(o_ref.dtype)
            ml_all = _stat_columns((m[:BLOCK], m[BLOCK:], l[:BLOCK], l[BLOCK:]),
                                   (2 * c, 2 * c + 1, B_HEADS + 2 * c, B_HEADS + 2 * c + 1), ml_all)
        ml_ref[0, rr, rows, :] = ml_all


def _band_attn(q, k, v, dil):
    b, _, m, w = q.shape
    nq = min(Q_BLOCKS, m // BLOCK)
    tq = nq * BLOCK
    rb = min(dil, Q_BLOCKS // nq)
    cur = pl.BlockSpec((1, rb, tq, w), lambda bb, r, i: (bb, r, i, 0))
    prev = pl.BlockSpec((1, rb, BLOCK, w), lambda bb, r, i: (bb, r, jnp.maximum(i * nq - 1, 0), 0))
    window, _ = [p for p in B_PATTERNS if p[1] == dil][0]
    return pl.pallas_call(
        functools.partial(_band_attn_kernel, max_dist=window // dil, nq=nq),
        grid=(b, dil // rb, m // tq),
        in_specs=[cur, prev, cur, prev, cur],
        out_specs=[cur, pl.BlockSpec((1, rb, tq, LANES), lambda bb, r, i: (bb, r, i, 0))],
        out_shape=[jax.ShapeDtypeStruct((b, dil, m, w), _BF),
                   jax.ShapeDtypeStruct((b, dil, m, LANES), _F32)],
        compiler_params=_params(("parallel", "parallel", "arbitrary")),
        name=f"band_attn_d{dil}",
    )(q, k, k, v, v)


def _swa_kernel(sink_ref, q_ref, kp_ref, kc_ref, vp_ref, vc_ref, o_ref, l_ref, *, nq):
    first_step = pl.program_id(1) == 0
    bias = _band_bias(A_WINDOW - 1, None)
    bias0 = _band_bias(A_WINDOW - 1, first_step)
    lane = lax.broadcasted_iota(jnp.int32, (1, LANES), 1)
    lo = lane < HEAD_DIM
    zero = jnp.zeros((), _BF)
    col0 = lax.broadcasted_iota(jnp.int32, (1, 2 * BLOCK), 1) == 0
    for j in range(nq):
        rows = slice(j * BLOCK, (j + 1) * BLOCK)
        bj = bias0 if j == 0 else bias
        bj = jnp.concatenate([bj, bj], axis=0)
        l_all = jnp.zeros((BLOCK, A_Q_HEADS), _F32)
        for c in range(A_QW // LANES):
            g = (2 * c) // A_GROUP
            gs = slice(g * LANES, (g + 1) * LANES)
            q2 = q_ref[0, rows, c * LANES:(c + 1) * LANES]
            k2 = _kv_window(kp_ref, kc_ref, j, (0,), gs)
            v2 = _kv_window(vp_ref, vc_ref, j, (0,), gs)
            qs = jnp.concatenate([jnp.where(lo, q2, zero), jnp.where(lo, zero, q2)], axis=0)
            s = lax.dot_general(qs, k2, (((1,), (1,)), ((), ())), preferred_element_type=_F32) + bj
            s = jnp.concatenate([jnp.where(col0, sink_ref[2 * c + half], s[half * BLOCK:(half + 1) * BLOCK])
                                 for half in range(2)], axis=0)
            m = jnp.max(s, axis=-1, keepdims=True)
            p = jnp.exp(s - m)
            l = jnp.sum(p, axis=-1, keepdims=True)
            p = jnp.where(col0, 0.0, p).astype(_BF)
            pv = jnp.dot(p, v2, preferred_element_type=_F32)
            _put(o_ref.at[0, rows, c * LANES:(c + 1) * LANES], jnp.where(lo, pv[:BLOCK], pv[BLOCK:]))
            l_all = _stat_columns((l[:BLOCK], l[BLOCK:]), (2 * c, 2 * c + 1), l_all)
        l_ref[0, rows, :] = l_all


def _swa_prompt(q, k, v, sinks):
    b, s, _ = q.shape
    nq = Q_BLOCKS
    tq = nq * BLOCK
    assert A_WINDOW <= BLOCK, "the sink logit borrows a key column that no query row can reach"
    qspec = pl.BlockSpec((1, tq, A_QW), lambda bb, i: (bb, i, 0))
    cur = pl.BlockSpec((1, tq, 2 * A_KVW), lambda bb, i: (bb, i, 0))
    prev = pl.BlockSpec((1, BLOCK, 2 * A_KVW), lambda bb, i: (bb, jnp.maximum(i * nq - 1, 0), 0))
    return pl.pallas_call(
        functools.partial(_swa_kernel, nq=nq),
        grid=(b, s // tq),
        in_specs=[pl.BlockSpec(memory_space=pltpu.SMEM), qspec, prev, cur, prev, cur],
        out_specs=[qspec, pl.BlockSpec((1, tq, A_Q_HEADS), lambda bb, i: (bb, i, 0))],
        out_shape=[jax.ShapeDtypeStruct((b, s, A_QW), _BF), jax.ShapeDtypeStruct((b, s, A_Q_HEADS), _F32)],
        compiler_params=_params(("parallel", "arbitrary")),
        name="swa_prompt",
    )(sinks, q, k, k, v, v)


def _roll_in(x_ref, new8_ref, o_ref, t_new):
    rows, width = x_ref.shape
    new8 = new8_ref[0, 0]
    tail = jnp.concatenate([jnp.zeros((LANES - new8.shape[0], new8.shape[1]), _F32), new8], axis=0).T[:rows]
    lane = lax.broadcasted_iota(jnp.int32, (1, LANES), 1)
    keep = lane < LANES - t_new
    nxt = pltpu.roll(x_ref[:, 0:LANES], LANES - t_new, 1)
    for c in range(width // LANES):
        cur = nxt
        if c + 1 < width // LANES:
            nxt = pltpu.roll(x_ref[:, (c + 1) * LANES:(c + 2) * LANES], LANES - t_new, 1)
        else:
            nxt = tail
        o_ref[:, c * LANES:(c + 1) * LANES] = jnp.where(keep, cur, nxt)


def _sample_half(g, refs, *, t_new):
    (sink_ref, qa_ref, kan_ref, van_ref, cak_ref, cav_ref, ak8_ref, av8_ref,
     qb_ref, kb8_ref, vb8_ref, cbk_ref, cbv_ref,
     oa_ref, ob_ref, nak_ref, nav_ref, nbk_ref, nbv_ref) = refs
    wa, wb = cak_ref.shape[-1], cbk_ref.shape[-1]

    rows_a = t_new * A_GROUP
    ra = lax.broadcasted_iota(jnp.int32, (rows_a, 1), 0)
    qi_a = _div(ra, A_GROUP)
    rel_a = wa + qi_a - lax.broadcasted_iota(jnp.int32, (1, wa), 1)
    valid_a = (rel_a >= 0) & (rel_a < A_WINDOW)
    hd = slice(0, HEAD_DIM)
    q = qa_ref[0, 0, 0:rows_a, hd]
    s = jnp.dot(q.astype(_BF), cak_ref[0].astype(_BF), preferred_element_type=_F32)
    s = jnp.where(valid_a, s, NEG)
    s_new = [jnp.where(qi_a >= j, jnp.sum(q * kan_ref[0, 0, j:j + 1, hd], axis=-1, keepdims=True), NEG)
             for j in range(t_new)]
    sink = jnp.zeros((rows_a, 1), _F32)
    for u in range(A_GROUP):
        sink = jnp.where((ra & (A_GROUP - 1)) == u, sink_ref[g * A_GROUP + u], sink)
    m = functools.reduce(jnp.maximum, [jnp.max(s, axis=-1, keepdims=True), sink] + s_new)
    p = jnp.exp(s - m)
    p_new = [jnp.exp(z - m) for z in s_new]
    l = functools.reduce(jnp.add, [jnp.sum(p, axis=-1, keepdims=True), jnp.exp(sink - m)] + p_new)
    o = lax.dot_general(p.astype(_BF), cav_ref[0].astype(_BF), (((1,), (1,)), ((), ())),
                        preferred_element_type=_F32)
    for j in range(t_new):
        o = o + p_new[j] * van_ref[0, 0, j:j + 1, hd]
    oa_ref[0, 0, 0:rows_a, hd] = o * (1.0 / l)
    _roll_in(cak_ref.at[0], ak8_ref, nak_ref.at[0], t_new)
    _roll_in(cav_ref.at[0], av8_ref, nav_ref.at[0], t_new)

    width_b = cbk_ref.shape[1]
    r0 = kb8_ref.shape[2] - t_new
    rows_b = t_new * SUBLANES
    sub = lax.broadcasted_iota(jnp.int32, (SUBLANES, width_b), 0)
    own = _div(lax.broadcasted_iota(jnp.int32, (SUBLANES, width_b), 1), HEAD_DIM) == sub
    qbd = jnp.concatenate([jnp.where(own, qb_ref[0, 0, i:i + 1, :], 0.0) for i in range(t_new)], axis=0)
    s_all = jnp.dot(qbd.astype(_BF), cbk_ref[0].astype(_BF), preferred_element_type=_F32)
    qi = _div(lax.broadcasted_iota(jnp.int32, (rows_b, 1), 0), SUBLANES)
    s_new = [jnp.sum(qbd * kb8_ref[0, 0, r0 + j:r0 + j + 1, :], axis=-1, keepdims=True) for j in range(t_new)]
    pats = []
    for window, dil in B_PATTERNS:
        lo_lane = max(wb - (-(-window // LANES) * LANES), 0)
        wl = lo_lane + lax.broadcasted_iota(jnp.int32, (1, wb - lo_lane), 1)
        rel = wb + qi - wl
        valid = (rel <= window) & _multiple(rel, dil)
        s = jnp.where(valid, s_all[:, lo_lane:], NEG)
        sn = [jnp.where((qi - j >= 0) & _multiple(qi - j, dil), s_new[j], NEG) for j in range(t_new)]
        m = functools.reduce(jnp.maximum, [jnp.max(s, axis=-1, keepdims=True)] + sn)
        p = jnp.exp(s - m)
        pn = [jnp.exp(z - m) for z in sn]
        l = functools.reduce(jnp.add, [jnp.sum(p, axis=-1, keepdims=True)] + pn)
        pats.append((lo_lane, p, pn, l, m + jnp.log(l)))
    mx = functools.reduce(jnp.maximum, [t[4] for t in pats])
    es = [jnp.exp(t[4] - mx) for t in pats]
    den = functools.reduce(jnp.add, es)
    coef = [e / (den * t[3]) for e, t in zip(es, pats)]
    starts = sorted({t[0] for t in pats} | {wb})
    blocks = []
    for a, b in zip(starts[:-1], starts[1:]):
        acc = None
        for c, (lo_lane, p, _, _, _) in zip(coef, pats):
            if lo_lane <= a:
                term = c * p[:, a - lo_lane:b - lo_lane]
                acc = term if acc is None else acc + term
        blocks.append(acc)
    lead = starts[0]
    p_comb = jnp.concatenate(blocks, axis=1) if len(blocks) > 1 else blocks[0]
    o_full = lax.dot_general(p_comb.astype(_BF), cbv_ref[0, :, lead:].astype(_BF), (((1,), (1,)), ((), ())),
                             preferred_element_type=_F32)
    for j in range(t_new):
        pj = functools.reduce(jnp.add, [c * t[2][j] for c, t in zip(coef, pats)])
        o_full = o_full + pj * vb8_ref[0, 0, r0 + j:r0 + j + 1, :]
    for i in range(t_new):
        blk = jnp.where(own, o_full[i * SUBLANES:(i + 1) * SUBLANES], 0.0)
        ob_ref[0, 0, i:i + 1, :] = jnp.sum(blk, axis=0, keepdims=True)
    _roll_in(cbk_ref.at[0], kb8_ref, nbk_ref, t_new)
    _roll_in(cbv_ref.at[0], vb8_ref, nbv_ref, t_new)


def _ffn_chunk(h, wg_ref, wu_ref, wd_ref, cs):
    gt = jnp.dot(h, wg_ref[:, cs], preferred_element_type=_F32)
    up = jnp.dot(h, wu_ref[:, cs], preferred_element_type=_F32)
    act = (gt * jax.nn.sigmoid(gt) * up).astype(_BF)
    return jnp.dot(act, wd_ref[cs, :], preferred_element_type=_F32)


def _attn_residual(x, oa, ob, gta, shf, scf, goa_ref, gob_ref, wo_ref, gf_ref):
    merged = jnp.concatenate([_rms(oa, goa_ref[...]), _rms(ob, gob_ref[...])], axis=1)
    x1 = x + gta * jnp.dot(merged.astype(_BF), wo_ref[...], preferred_element_type=_F32)
    return x1, (_rms(x1, gf_ref[...]) * (1.0 + scf) + shf).astype(_BF)


_TWO = lambda r: r[...].reshape(-1, r.shape[-1])
_FF_CUTS = (0, 384, 1152, 2048, 2816)
_PHASES = len(_FF_CUTS) - 1


def _ffn_and_sample_kernel(*refs, t_new):
    (x_ref, oa_ref, la_ref, o1_ref, o4_ref, o16_ref, ml1_ref, ml2_ref, ml3_ref, e_ref,
     gta_ref, shf_ref, scf_ref, gtf_ref, goa_ref, gob_ref, wo_ref, gf_ref, wg_ref, wu_ref, wd_ref, gfin_ref) = refs[:22]
    sample_in = refs[22:35]
    y_ref = refs[35]
    oa_s_ref, ob_s_ref, nak_ref, nav_ref, nbk_ref, nbv_ref = refs[36:42]
    s4_ref, s16_ref, x1_ref, h_ref, acc_ref, t4_ref, t16_ref = refs[42:]
    step = pl.program_id(0)
    half = step % 2
    phase = step % _PHASES
    tm = x_ref.shape[1]
    sample_refs = tuple(sample_in) + (oa_s_ref, ob_s_ref, nak_ref, nav_ref, nbk_ref.at[0], nbv_ref.at[0])

    def mix_and_project():
        for dil, src, stage, stat_src, stat_stage in ((4, o4_ref, s4_ref, ml2_ref, t4_ref),
                                                      (16, o16_ref, s16_ref, ml3_ref, t16_ref)):
            for r in range(dil):
                rows_r = pl.ds(r, tm // dil, stride=dil)
                stat_stage[rows_r, :] = stat_src[0, r]
                for c in range(B_W // LANES):
                    stage[c, rows_r, :] = src[0, r, :, c * LANES:(c + 1) * LANES].astype(_F32)
        gather = lambda stage: jnp.concatenate([stage[c] for c in range(B_W // LANES)], axis=1)
        outs = [_TWO(o1_ref).astype(_F32), gather(s4_ref), gather(s16_ref)]

        def widen(per_head):
            hi = per_head.astype(_BF)
            lo = (per_head - hi.astype(_F32)).astype(_BF)
            return (jnp.dot(hi, e_ref[...], preferred_element_type=_F32)
                    + jnp.dot(lo, e_ref[...], preferred_element_type=_F32))

        stats = (ml1_ref.at[0, 0], t4_ref, t16_ref)
        ms = [r[:, 0:B_HEADS] for r in stats]
        ls = [r[:, B_HEADS:2 * B_HEADS] for r in stats]
        lses = [m + jnp.log(l) for m, l in zip(ms, ls)]
        mx = jnp.maximum(jnp.maximum(lses[0], lses[1]), lses[2])
        es = [jnp.exp(z - mx) for z in lses]
        den = es[0] + es[1] + es[2]
        ob = jnp.zeros((tm, B_W), _F32)
        for e, l, o in zip(es, ls, outs):
            ob = ob + widen(e / (den * l)) * o
        oa = widen(1.0 / _TWO(la_ref)) * _TWO(oa_ref).astype(_F32)
        x1, h = _attn_residual(_TWO(x_ref), oa, ob, _TWO(gta_ref), _TWO(shf_ref), _TWO(scf_ref),
                               goa_ref, gob_ref, wo_ref, gf_ref)
        x1_ref[...] = x1
        h_ref[...] = h

    for k in range(_PHASES):
        @pl.when(phase == k)
        def _(k=k):
            _sample_half(half, sample_refs, t_new=t_new)
            if k == 0:
                mix_and_project()
            part = _ffn_chunk(h_ref[...], wg_ref, wu_ref, wd_ref, slice(_FF_CUTS[k], _FF_CUTS[k + 1]))
            if k == 0:
                acc_ref[...] = part
            elif k < _PHASES - 1:
                acc_ref[...] += part
            else:
                x2 = x1_ref[...] + _TWO(gtf_ref) * (acc_ref[...] + part)
                y_ref[...] = _rms(x2, gfin_ref[...]).reshape(y_ref.shape)


def _ffn_and_sample(x, oa, la, obs, mls, mods, goa, gob, wo, gf, wg, wu, wd, gfin, sinks, sample_ins, *, tm):
    b, s, d = x.shape
    n = sample_ins[0].shape[0]
    tiles = s // tm
    steps = 2 * n
    assert steps == b * tiles * _PHASES and _PHASES % 2 == 0 and wg.shape[1] == _FF_CUTS[-1]
    t_new = sample_ins[0].shape[2] // A_GROUP

    def tile(j):
        t = j // _PHASES
        return t // tiles, t % tiles

    row = lambda c: pl.BlockSpec((1, tm, c), lambda j: tile(j) + (0,))
    res = lambda dil: pl.BlockSpec((1, dil, tm // dil, B_W), lambda j: (tile(j)[0], 0, tile(j)[1], 0))
    mod_spec = pl.BlockSpec((1, 1, d), lambda j: (tile(j)[0], 0, 0))
    expand = (jnp.arange(B_W)[None, :] // HEAD_DIM == jnp.arange(B_HEADS)[:, None]).astype(_BF)
    ffn_ins = [x, oa, la] + list(obs) + list(mls) + [expand] + list(mods) + [goa, gob, wo, gf, wg, wu, wd, gfin]
    ffn_specs = ([row(d), row(A_QW), row(A_Q_HEADS)] + [res(o.shape[1]) for o in obs]
                 + [pl.BlockSpec((1, o.shape[1], tm // o.shape[1], LANES), lambda j: (tile(j)[0], 0, tile(j)[1], 0))
                    for o in obs]
                 + [_resident(expand.shape)] + [mod_spec] * 4
                 + [_resident(a.shape) for a in (goa, gob, wo, gf, wg, wu, wd, gfin)])
    half_blk = lambda a: pl.BlockSpec((1, 1) + a.shape[2:], lambda j: (j // 2, j % 2) + (0,) * (a.ndim - 2))
    cache_blk = lambda a: pl.BlockSpec((1, a.shape[1] // 2, a.shape[2]), lambda j: (j // 2, j % 2, 0))
    (qa_g, ka_n, va_n, ca_k, ca_v, ak8, av8, qb_h, kb8, vb8, cb_k, cb_v) = sample_ins
    s_specs = [half_blk(qa_g), half_blk(ka_n), half_blk(va_n), cache_blk(ca_k), cache_blk(ca_v), half_blk(ak8),
               half_blk(av8), half_blk(qb_h), half_blk(kb8), half_blk(vb8), cache_blk(cb_k), cache_blk(cb_v)]
    o_specs = [half_blk(qa_g), half_blk(qb_h), cache_blk(ca_k), cache_blk(ca_v), cache_blk(cb_k), cache_blk(cb_v)]
    sample_outs = [jax.ShapeDtypeStruct(a.shape, _F32) for a in (qa_g, qb_h, ca_k, ca_v, cb_k, cb_v)]
    stage = pltpu.VMEM((B_W // LANES, tm, LANES), _F32)
    return pl.pallas_call(
        functools.partial(_ffn_and_sample_kernel, t_new=t_new),
        grid=(steps,),
        in_specs=ffn_specs + [pl.BlockSpec(memory_space=pltpu.SMEM)] + s_specs,
        out_specs=[row(d)] + o_specs,
        out_shape=[jax.ShapeDtypeStruct((b, s, d), _F32)] + sample_outs,
        scratch_shapes=[stage, stage, pltpu.VMEM((tm, d), _F32), pltpu.VMEM((tm, d), _BF), pltpu.VMEM((tm, d), _F32),
                        pltpu.VMEM((tm, LANES), _F32), pltpu.VMEM((tm, LANES), _F32)],
        compiler_params=_params(("arbitrary",)),
        name="ffn_and_sample",
    )(*ffn_ins, sinks, *sample_ins)


def _merge_ffn_sample_kernel(x_ref, oa_ref, ob_ref, gta_ref, shf_ref, scf_ref, gtf_ref,
                             goa_ref, gob_ref, wo_ref, gf_ref, wg_ref, wu_ref, wd_ref, gfin_ref, y_ref):
    per_row = lambda r: jnp.concatenate([r[...]] * x_ref.shape[0], axis=0)
    x1, h = _attn_residual(_TWO(x_ref), _TWO(oa_ref), _TWO(ob_ref), per_row(gta_ref), per_row(shf_ref),
                           per_row(scf_ref), goa_ref, gob_ref, wo_ref, gf_ref)
    acc = jnp.zeros_like(x1)
    for k in range(_PHASES):
        acc = acc + _ffn_chunk(h, wg_ref, wu_ref, wd_ref, slice(_FF_CUTS[k], _FF_CUTS[k + 1]))
    y_ref[...] = _rms(x1 + per_row(gtf_ref) * acc, gfin_ref[...]).reshape(y_ref.shape)


def _merge_ffn_sample(x, oa, ob, mods, goa, gob, wo, gf, wg, wu, wd, gfin):
    t, n, d = x.shape
    whole = lambda a: pl.BlockSpec(a.shape, lambda j: (0,) * a.ndim)
    weights = (goa, gob, wo, gf, wg, wu, wd, gfin)
    ins = (x, oa, ob, *mods, *weights)
    return pl.pallas_call(
        _merge_ffn_sample_kernel,
        grid=(1,),
        in_specs=[whole(a) for a in ins[:7]] + [_resident(a.shape) for a in weights],
        out_specs=whole(x),
        out_shape=jax.ShapeDtypeStruct((t, n, d), _F32),
        compiler_params=_params(("arbitrary",)),
        name="merge_ffn_sample",
    )(*ins)


def _rope_tables(pos):
    half = HEAD_DIM // 2
    inv = jnp.exp(-math.log(ROPE_THETA) * jnp.arange(half, dtype=_F32) * (2.0 / HEAD_DIM))
    ang = pos.astype(_F32)[:, None] * inv[None, :]
    cos, sin = jnp.cos(ang), jnp.sin(ang)
    return jnp.tile(cos, (1, LANES // half)), jnp.tile(jnp.concatenate([-sin, sin], axis=1), (1, LANES // HEAD_DIM))


def _window_on_lanes(cache):
    _, n, w, h, dh = cache.shape
    return cache.transpose(0, 1, 3, 4, 2).reshape(n, h * dh, w)


def _window_off_lanes(t, h):
    n, _, w = t.shape
    return t.reshape(1, n, h, HEAD_DIM, w).transpose(0, 1, 4, 2, 3)


def _halves_last_rows(new_t, pad_to):
    t, n, c = new_t.shape
    v = new_t.reshape(t, n, 2, c // 2).transpose(1, 2, 0, 3)
    return jnp.pad(v, ((0, 0), (0, 0), (SUBLANES - t, 0), (0, pad_to - c // 2)))


def kernel(x_prompt, x_sample, c_prompt, c_sample, cache_a_k, cache_a_v, cache_b_k, cache_b_v, w_ada, b_ada, g_attn, w_in, sinks, g_out_a, g_out_b, w_o, g_ffn, w_gate, w_up, w_down, g_final):
    nb, s, d = x_prompt.shape
    ns, t_new, _ = x_sample.shape
    assert w_ada.shape[0] == 1, "single trunk layer"
    l = 0
    bf = lambda w: w.astype(_BF)
    row = lambda g: g.reshape(1, -1)

    c_all = jnp.concatenate([c_prompt, c_sample], axis=0)
    c_all = jnp.pad(c_all, ((0, (-c_all.shape[0]) % SUBLANES), (0, 0)))
    mod = _adaln(c_all, w_ada[l], b_ada[l].reshape(1, -1))
    mod_p = [m.reshape(nb, 1, d) for m in jnp.split(mod[:nb], 6, axis=-1)]
    mod_s = jnp.split(mod[nb:nb + ns], 6, axis=-1)

    cos_p, sin_p = _rope_tables(jnp.arange(s, dtype=jnp.int32))
    cos_s, sin_s = _rope_tables(PAST_LEN + jnp.arange(t_new, dtype=jnp.int32))
    w_in_bf = bf(w_in[l])
    ffn_w = (row(g_out_a[l]), row(g_out_b[l]), bf(w_o[l]), row(g_ffn[l]), bf(w_gate[l]), bf(w_up[l]), bf(w_down[l]),
             row(g_final))

    wa_p, wb_p = min(A_WINDOW, s), min(B_WINDOW, s)
    (qa, ka, va, q1, k1, v1, q4, k4, v4, q16, k16, v16, ka_t, va_t, kb_t, vb_t) = _inproj_prompt(
        x_prompt, mod_p[0], mod_p[1], row(g_attn[l]), w_in_bf, cos_p, sin_p, tm=512, tail_a=wa_p, tail_b=wb_p)
    xs_t = x_sample.transpose(1, 0, 2)
    qa_s, ka_s, va_s, qb_s, kb_s, vb_s = _inproj_sample(
        xs_t, mod_s[0], mod_s[1], row(g_attn[l]), w_in_bf, cos_s.reshape(t_new, 1, LANES),
        sin_s.reshape(t_new, 1, LANES))

    oa, la = _swa_prompt(qa, ka, va, sinks[l])
    obs, mls = [], []
    for (q, k, v), (_, dil) in zip(((q1, k1, v1), (q4, k4, v4), (q16, k16, v16)), B_PATTERNS):
        o, ml = _band_attn(q, k, v, dil)
        obs.append(o)
        mls.append(ml)

    qa_g = (qa_s.reshape(t_new, ns, A_KV_HEADS, A_GROUP, HEAD_DIM).transpose(1, 2, 0, 3, 4)
            .reshape(ns, A_KV_HEADS, t_new * A_GROUP, HEAD_DIM))
    kv_g = lambda t: t.reshape(t_new, ns, A_KV_HEADS, HEAD_DIM).transpose(1, 2, 0, 3)
    qb_h = qb_s.reshape(t_new, ns, 2, B_W // 2).transpose(1, 2, 0, 3)
    sample_ins = (qa_g, kv_g(ka_s), kv_g(va_s), _window_on_lanes(cache_a_k), _window_on_lanes(cache_a_v),
                  _halves_last_rows(ka_s, LANES), _halves_last_rows(va_s, LANES), qb_h,
                  _halves_last_rows(kb_s, B_W // 2), _halves_last_rows(vb_s, B_W // 2),
                  _window_on_lanes(cache_b_k), _window_on_lanes(cache_b_v))
    y_prompt, oa_g, ob_h, na_k, na_v, nb_k, nb_v = _ffn_and_sample(
        x_prompt, oa, la, obs, mls, mod_p[2:6], *ffn_w, sinks[l], sample_ins, tm=256)
    pa_k = ka_t.reshape(1, nb, wa_p, A_KV_HEADS, HEAD_DIM)
    pa_v = va_t.reshape(1, nb, wa_p, A_KV_HEADS, HEAD_DIM)
    pb_k = kb_t.reshape(1, nb, wb_p, B_HEADS, HEAD_DIM)
    pb_v = vb_t.reshape(1, nb, wb_p, B_HEADS, HEAD_DIM)

    oa_t = (oa_g.reshape(ns, A_KV_HEADS, t_new, A_GROUP, HEAD_DIM).transpose(2, 0, 1, 3, 4)
            .reshape(t_new, ns, A_QW))
    ob_t = ob_h.transpose(2, 0, 1, 3).reshape(t_new, ns, B_W)
    y_s = _merge_ffn_sample(xs_t, oa_t, ob_t, mod_s[2:6], *ffn_w)
    y_sample = y_s.transpose(1, 0, 2)
    sa_k, sa_v = _window_off_lanes(na_k, A_KV_HEADS), _window_off_lanes(na_v, A_KV_HEADS)
    sb_k, sb_v = _window_off_lanes(nb_k, B_HEADS), _window_off_lanes(nb_v, B_HEADS)

    return (y_prompt, y_sample, pa_k, pa_v, pb_k, pb_v, sa_k, sa_v, sb_k, sb_v)
```
